```python
import math
import jax, jax.numpy as jnp
from jax import lax
import numpy as np

D_MODEL = 2048
BATCH = 1
SEQ = 8192
DEPTH = 4

N_A_LAYERS = DEPTH // 2
N_B_LAYERS = DEPTH - N_A_LAYERS

D_FF = 11 * D_MODEL // 4

GDN_HEAD_DIM = 128
GDN_K_HEADS = D_MODEL // GDN_HEAD_DIM
GDN_V_HEADS = 2 * GDN_K_HEADS
GDN_REP = GDN_V_HEADS // GDN_K_HEADS
GDN_KEY_DIM = GDN_K_HEADS * GDN_HEAD_DIM
GDN_VALUE_DIM = GDN_V_HEADS * GDN_HEAD_DIM
GDN_CONV_DIM = 2 * GDN_KEY_DIM + GDN_VALUE_DIM
GDN_IN_DIM = GDN_CONV_DIM + GDN_VALUE_DIM + 2 * GDN_V_HEADS
GDN_CONV_WIDTH = 4
GDN_CHUNK = 64

ATT_HEAD_DIM = 128
ATT_HEADS = D_MODEL // ATT_HEAD_DIM
ATT_KV_HEADS = 4
MOBA_BLOCK = 256
MOBA_TOPK = 3
MOBA_Q_CHUNK = 32

N_REL_BUCKETS = 32
REL_MAX_DISTANCE = 128

RMS_EPS = 1e-6

kernel_name = "yoco_gdn_moba_macaron_trunk"


def rms_norm(x, g):
    xf = x.astype(jnp.float32)
    y = xf * lax.rsqrt(jnp.mean(xf * xf, axis=-1, keepdims=True) + RMS_EPS)
    return (y * g.astype(jnp.float32)).astype(x.dtype)


def l2_normalize(x):
    xf = x.astype(jnp.float32)
    return xf * lax.rsqrt(jnp.sum(xf * xf, axis=-1, keepdims=True) + RMS_EPS)


def swiglu(xn, w_in, w_out):
    gate, up = jnp.split(xn @ w_in, 2, axis=-1)
    return (jax.nn.silu(gate) * up) @ w_out


def causal_depthwise_conv(x, w):
    k_width, c = w.shape
    return lax.conv_general_dilated(
        x, w[:, None, :].astype(x.dtype), window_strides=(1,), padding=[(k_width - 1, 0)],
        dimension_numbers=("NWC", "WIO", "NWC"), feature_group_count=c)


def chunk_gated_delta_rule(q, k, v, g, beta):
    B, T, H, Dk = q.shape
    Dv = v.shape[-1]
    C = GDN_CHUNK
    N = T // C
    f32 = jnp.float32

    def chunks(t):
        return t.astype(f32).reshape(B, N, C, H, -1).transpose(0, 3, 1, 2, 4)

    qc = chunks(q) * (Dk ** -0.5)
    kc = chunks(k)
    vc = chunks(v)
    bc = chunks(beta[..., None])
    gc = jnp.cumsum(chunks(g[..., None])[..., 0], axis=-1)
    incl = jnp.tril(jnp.ones((C, C), dtype=bool))
    strict = jnp.tril(jnp.ones((C, C), dtype=bool), k=-1)
    diff = gc[..., :, None] - gc[..., None, :]
    decay = jnp.where(incl, jnp.exp(jnp.where(incl, diff, 0.0)), 0.0)
    kk = jnp.einsum('bhncd,bhnmd->bhncm', kc * bc, kc)
    a_mat = jnp.eye(C, dtype=f32) + jnp.where(strict, kk * decay, 0.0)
    u = lax.linalg.triangular_solve(a_mat, vc * bc, left_side=True, lower=True, unit_diagonal=True)
    w = lax.linalg.triangular_solve(a_mat, kc * bc * jnp.exp(gc)[..., None], left_side=True,
                                    lower=True, unit_diagonal=True)
    qk = jnp.where(incl, jnp.einsum('bhncd,bhnmd->bhncm', qc, kc) * decay, 0.0)
    g_last = gc[..., -1]
    q_g = qc * jnp.exp(gc)[..., None]
    k_upd = kc * jnp.exp(g_last[..., None] - gc)[..., None]

    def step(state, xs):
        q_i, k_i, u_i, w_i, qk_i, gl_i = xs
        v_new = u_i - jnp.einsum('bhck,bhkv->bhcv', w_i, state)
        o_i = jnp.einsum('bhck,bhkv->bhcv', q_i, state) + jnp.einsum('bhcm,bhmv->bhcv', qk_i, v_new)
        state = state * jnp.exp(gl_i)[..., None, None] + jnp.einsum('bhck,bhcv->bhkv', k_i, v_new)
        return state, o_i

    xs = (jnp.moveaxis(q_g, 2, 0), jnp.moveaxis(k_upd, 2, 0), jnp.moveaxis(u, 2, 0),
          jnp.moveaxis(w, 2, 0), jnp.moveaxis(qk, 2, 0), jnp.moveaxis(g_last, 2, 0))
    state0 = jnp.zeros((B, H, Dk, Dv), f32)
    _, o = lax.scan(step, state0, xs)
    return o.transpose(1, 0, 3, 2, 4).reshape(B, T, H, Dv)


def gated_deltanet(xn, w_in, conv_w, a_log, dt_bias, out_norm, w_out):
    B, T, _ = xn.shape
    proj = xn @ w_in
    qkv = proj[..., :GDN_CONV_DIM]
    z = proj[..., GDN_CONV_DIM:GDN_CONV_DIM + GDN_VALUE_DIM]
    b = proj[..., GDN_CONV_DIM + GDN_VALUE_DIM:GDN_CONV_DIM + GDN_VALUE_DIM + GDN_V_HEADS]
    a = proj[..., GDN_CONV_DIM + GDN_VALUE_DIM + GDN_V_HEADS:]
    qkv = jax.nn.silu(causal_depthwise_conv(qkv, conv_w))
    q = qkv[..., :GDN_KEY_DIM].reshape(B, T, GDN_K_HEADS, GDN_HEAD_DIM)
    k = qkv[..., GDN_KEY_DIM:2 * GDN_KEY_DIM].reshape(B, T, GDN_K_HEADS, GDN_HEAD_DIM)
    v = qkv[..., 2 * GDN_KEY_DIM:].reshape(B, T, GDN_V_HEADS, GDN_HEAD_DIM)
    q = jnp.repeat(l2_normalize(q), GDN_REP, axis=2)
    k = jnp.repeat(l2_normalize(k), GDN_REP, axis=2)
    beta = jax.nn.sigmoid(b.astype(jnp.float32))
    g = -jnp.exp(a_log.astype(jnp.float32)) * jax.nn.softplus(a.astype(jnp.float32) + dt_bias.astype(jnp.float32))
    o = chunk_gated_delta_rule(q, k, v, g, beta)
    o = rms_norm(o, out_norm) * jax.nn.silu(z.reshape(B, T, GDN_V_HEADS, GDN_HEAD_DIM).astype(jnp.float32))
    return o.reshape(B, T, GDN_VALUE_DIM).astype(xn.dtype) @ w_out


def rel_bucket(dist):
    n = jnp.maximum(dist, 0)
    max_exact = N_REL_BUCKETS // 2
    log_ratio = jnp.log(jnp.maximum(n, max_exact).astype(jnp.float32) / max_exact) / math.log(REL_MAX_DISTANCE / max_exact)
    large = jnp.minimum(max_exact + (log_ratio * (N_REL_BUCKETS - max_exact)).astype(jnp.int32), N_REL_BUCKETS - 1)
    return jnp.where(n < max_exact, n, large)


def shared_block_kv(h, kv_norm, w_kv):
    B, T, _ = h.shape
    kv = rms_norm(h, kv_norm) @ w_kv
    nb = -(-T // MOBA_BLOCK)
    kv = jnp.pad(kv, ((0, 0), (0, nb * MOBA_BLOCK - T), (0, 0)))
    kv = kv.reshape(B, nb, MOBA_BLOCK, 2, ATT_KV_HEADS, ATT_HEAD_DIM)
    k_blocks = kv[:, :, :, 0].transpose(0, 1, 3, 2, 4)
    v_blocks = kv[:, :, :, 1].transpose(0, 1, 3, 2, 4)
    k_means = jnp.mean(k_blocks.astype(jnp.float32), axis=3).astype(k_blocks.dtype)
    return k_blocks, v_blocks, k_means


def moba_attention(q, k_blocks, v_blocks, k_means, rel_bias):
    B, T, H, Dh = q.shape
    NB = k_blocks.shape[1]
    n_top = min(MOBA_TOPK, NB)
    Qc = MOBA_Q_CHUNK
    n_chunks = T // Qc
    kv_of_head = jnp.arange(H) // (H // ATT_KV_HEADS)
    km = k_means[:, :, kv_of_head]
    bias_hb = rel_bias.T
    b_idx = jnp.arange(B)[:, None, None, None]
    kvh_idx = kv_of_head[None, None, :, None]
    head_ids = jnp.arange(H)[None, None, :, None, None]
    offs = jnp.arange(MOBA_BLOCK)
    q_chunks = jnp.moveaxis(q.reshape(B, n_chunks, Qc, H, Dh), 1, 0)
    n_sel = n_top * MOBA_BLOCK

    def attend_chunk(args):
        c, q_c = args
        q_pos = c * Qc + jnp.arange(Qc)
        own = (c * Qc) // MOBA_BLOCK
        gate = jnp.einsum('bqhd,bnhd->bqhn', q_c, km).astype(jnp.float32)
        gate = jnp.where(jnp.arange(NB) < own, gate, -jnp.inf)
        _, sel = lax.top_k(gate, n_top)
        sel_ok = jnp.arange(n_top) < own
        k_sel = k_blocks[b_idx, sel, kvh_idx]
        v_sel = v_blocks[b_idx, sel, kvh_idx]
        dist_sel = q_pos[None, :, None, None, None] - (sel[..., None] * MOBA_BLOCK + offs)
        s_sel = jnp.einsum('bqhd,bqhkjd->bqhkj', q_c, k_sel).astype(jnp.float32) + bias_hb[head_ids, rel_bucket(dist_sel)]
        s_sel = jnp.where(sel_ok[:, None], s_sel, -jnp.inf)
        k_own = lax.dynamic_index_in_dim(k_blocks, own, axis=1, keepdims=False)[:, kv_of_head]
        v_own = lax.dynamic_index_in_dim(v_blocks, own, axis=1, keepdims=False)[:, kv_of_head]
        dist_own = q_pos[:, None] - (own * MOBA_BLOCK + offs)[None, :]
        s_own = jnp.einsum('bqhd,bhjd->bqhj', q_c, k_own).astype(jnp.float32) + jnp.transpose(bias_hb[:, rel_bucket(dist_own)], (1, 0, 2))
        s_own = jnp.where((dist_own >= 0)[:, None, :], s_own, -jnp.inf)
        logits = jnp.concatenate([s_sel.reshape(B, Qc, H, n_sel), s_own], axis=-1)
        p = jax.nn.softmax(logits, axis=-1).astype(q_c.dtype)
        p_sel = p[..., :n_sel].reshape(B, Qc, H, n_top, MOBA_BLOCK)
        p_own = p[..., n_sel:]
        return (jnp.einsum('bqhkj,bqhkjd->bqhd', p_sel, v_sel)
                + jnp.einsum('bqhj,bhjd->bqhd', p_own, v_own))

    out = lax.map(attend_chunk, (jnp.arange(n_chunks), q_chunks))
    return jnp.moveaxis(out, 0, 1).reshape(B, T, H * Dh)


def moba_layer(hn, w_q, w_o, k_blocks, v_blocks, k_means, rel_bias):
    B, T, _ = hn.shape
    q = (hn @ w_q).reshape(B, T, ATT_HEADS, ATT_HEAD_DIM) * (ATT_HEAD_DIM ** -0.5)
    return moba_attention(q, k_blocks, v_blocks, k_means, rel_bias) @ w_o


def setup_inputs(seed: int = 0) -> dict:
    key = jax.random.key(seed)
    ks = jax.random.split(key, 24)
    f32 = jnp.float32

    def nrm(k, shape, fan_in):
        return jax.random.normal(k, shape, f32) * (fan_in ** -0.5)

    def gain(k, shape):
        return 1.0 + 0.02 * jax.random.normal(k, shape, f32)

    dt = jnp.exp(jax.random.uniform(ks[12], (N_A_LAYERS, GDN_V_HEADS), f32,
                                    math.log(1e-3), math.log(1e-1)))
    return {
        "x": jax.random.normal(ks[0], (BATCH, SEQ, D_MODEL), f32),
        "ffn1_norm": gain(ks[1], (DEPTH, D_MODEL)),
        "ffn1_w_in": nrm(ks[2], (DEPTH, D_MODEL, 2 * D_FF), D_MODEL),
        "ffn1_w_out": nrm(ks[3], (DEPTH, D_FF, D_MODEL), D_FF),
        "mix_norm": gain(ks[4], (DEPTH, D_MODEL)),
        "ffn2_norm": gain(ks[5], (DEPTH, D_MODEL)),
        "ffn2_w_in": nrm(ks[6], (DEPTH, D_MODEL, 2 * D_FF), D_MODEL),
        "ffn2_w_out": nrm(ks[7], (DEPTH, D_FF, D_MODEL), D_FF),
        "gdn_w_in": nrm(ks[8], (N_A_LAYERS, D_MODEL, GDN_IN_DIM), D_MODEL),
        "gdn_conv_w": nrm(ks[9], (N_A_LAYERS, GDN_CONV_WIDTH, GDN_CONV_DIM), GDN_CONV_WIDTH),
        "gdn_a_log": jnp.log(jax.random.uniform(ks[10], (N_A_LAYERS, GDN_V_HEADS), f32, 1.0, 16.0)),
        "gdn_dt_bias": dt + jnp.log(-jnp.expm1(-dt)),
        "gdn_out_norm": gain(ks[11], (N_A_LAYERS, GDN_HEAD_DIM)),
        "gdn_w_out": nrm(ks[13], (N_A_LAYERS, GDN_VALUE_DIM, D_MODEL), GDN_VALUE_DIM),
        "kv_norm": gain(ks[14], (D_MODEL,)),
        "w_kv": nrm(ks[15], (D_MODEL, 2 * ATT_KV_HEADS * ATT_HEAD_DIM), D_MODEL),
        "moba_w_q": nrm(ks[16], (N_B_LAYERS, D_MODEL, ATT_HEADS * ATT_HEAD_DIM), D_MODEL),
        "moba_w_o": nrm(ks[17], (N_B_LAYERS, ATT_HEADS * ATT_HEAD_DIM, D_MODEL), ATT_HEADS * ATT_HEAD_DIM),
        "rel_bias": 0.5 * jax.random.normal(ks[18], (N_REL_BUCKETS, ATT_HEADS), f32),
        "final_norm": gain(ks[19], (D_MODEL,)),
    }


def reference(x, ffn1_norm, ffn1_w_in, ffn1_w_out, mix_norm, ffn2_norm, ffn2_w_in, ffn2_w_out,
              gdn_w_in, gdn_conv_w, gdn_a_log, gdn_dt_bias, gdn_out_norm, gdn_w_out,
              kv_norm, w_kv, moba_w_q, moba_w_o, rel_bias, final_norm):
    h = x
    k_blocks = v_blocks = k_means = None
    for layer in range(DEPTH):
        if layer == N_A_LAYERS:
            k_blocks, v_blocks, k_means = shared_block_kv(h, kv_norm, w_kv)
        h = h + 0.5 * swiglu(rms_norm(h, ffn1_norm[layer]), ffn1_w_in[layer], ffn1_w_out[layer])
        hn = rms_norm(h, mix_norm[layer])
        if layer < N_A_LAYERS:
            h = h + gated_deltanet(hn, gdn_w_in[layer], gdn_conv_w[layer], gdn_a_log[layer],
                                   gdn_dt_bias[layer], gdn_out_norm[layer], gdn_w_out[layer])
        else:
            j = layer - N_A_LAYERS
            h = h + moba_layer(hn, moba_w_q[j], moba_w_o[j], k_blocks, v_blocks, k_means, rel_bias)
        h = h + 0.5 * swiglu(rms_norm(h, ffn2_norm[layer]), ffn2_w_in[layer], ffn2_w_out[layer])
    return rms_norm(h, final_norm)
```

```python
import functools
import math

import numpy as np
import jax
import jax.numpy as jnp
from jax import lax
from jax.experimental import pallas as pl
from jax.experimental.pallas import tpu as pltpu

F32 = jnp.float32
BF16 = jnp.bfloat16

RMS_EPS = 1e-6
GDN_HEAD_DIM = 128
GDN_CHUNK = 64
GDN_CONV_WIDTH = 4
ATT_HEAD_DIM = 128
MOBA_BLOCK = 256
MOBA_TOPK = 3
N_REL_BUCKETS = 32
REL_MAX_DISTANCE = 128
MASK_VALUE = -1e30

VMEM_LIMIT_BYTES = 52 * 1024 * 1024


def _params(*sem):
    return pltpu.CompilerParams(dimension_semantics=sem, vmem_limit_bytes=VMEM_LIMIT_BYTES)


def _rms_rows(x, g):
    ms = jnp.mean(x * x, axis=-1, keepdims=True)
    return x * lax.rsqrt(ms + RMS_EPS) * g


def _silu(x):
    return x * jax.nn.sigmoid(x)


def _norm_matmul_kernel(x_ref, g_ref, w_ref, o_ref, xn_ref):
    @pl.when(pl.program_id(1) == 0)
    def _():
        xn_ref[...] = _rms_rows(x_ref[...], g_ref[...]).astype(BF16)

    o_ref[...] = jnp.dot(xn_ref[...], w_ref[...], preferred_element_type=F32).astype(o_ref.dtype)


def norm_matmul(x, g, w, *, tm, tn, name):
    t, d = x.shape
    n = w.shape[1]
    return pl.pallas_call(
        _norm_matmul_kernel,
        grid=(t // tm, n // tn),
        in_specs=[
            pl.BlockSpec((tm, d), lambda i, j: (i, 0)),
            pl.BlockSpec((1, d), lambda i, j: (0, 0)),
            pl.BlockSpec((d, tn), lambda i, j: (0, j)),
        ],
        out_specs=pl.BlockSpec((tm, tn), lambda i, j: (i, j)),
        out_shape=jax.ShapeDtypeStruct((t, n), F32),
        scratch_shapes=[pltpu.VMEM((tm, d), BF16)],
        compiler_params=_params("parallel", "arbitrary"),
        name=name,
    )(x, g.reshape(1, d), w)


def _norm_matmul_t_kernel(x_ref, g_ref, wt_ref, o_ref, xn_ref):
    @pl.when(pl.program_id(1) == 0)
    def _():
        xn_ref[...] = _rms_rows(x_ref[...], g_ref[...]).astype(BF16)

    o_ref[...] = lax.dot_general(wt_ref[...], xn_ref[...], (((1,), (1,)), ((), ())),
                                 preferred_element_type=F32)


def norm_matmul_t(x, g, wt, *, tm, tn, name):
    t, d = x.shape
    n = wt.shape[0]
    return pl.pallas_call(
        _norm_matmul_t_kernel,
        grid=(t // tm, n // tn),
        in_specs=[
            pl.BlockSpec((tm, d), lambda i, j: (i, 0)),
            pl.BlockSpec((1, d), lambda i, j: (0, 0)),
            pl.BlockSpec((tn, d), lambda i, j: (j, 0)),
        ],
        out_specs=pl.BlockSpec((tn, tm), lambda i, j: (j, i)),
        out_shape=jax.ShapeDtypeStruct((n, t), F32),
        scratch_shapes=[pltpu.VMEM((tm, d), BF16)],
        compiler_params=_params("parallel", "arbitrary"),
        name=name,
    )(x, g.reshape(1, d), wt)


def _ffn_kernel(h_ref, g_ref, wg_ref, wu_ref, wo_ref, fg_ref, o_ref, xn_ref, acc_ref, *, final_norm):
    j = pl.program_id(1)

    @pl.when(j == 0)
    def _():
        xn_ref[...] = _rms_rows(h_ref[...], g_ref[...]).astype(BF16)
        acc_ref[...] = jnp.zeros_like(acc_ref)

    xn = xn_ref[...]
    gate = jnp.dot(xn, wg_ref[...], preferred_element_type=F32)
    up = jnp.dot(xn, wu_ref[...], preferred_element_type=F32)
    act = (_silu(gate) * up).astype(BF16)
    acc_ref[...] += jnp.dot(act, wo_ref[...], preferred_element_type=F32)

    @pl.when(j == pl.num_programs(1) - 1)
    def _():
        y = h_ref[...] + 0.5 * acc_ref[...]
        if final_norm:
            y = _rms_rows(y, fg_ref[...])
        o_ref[...] = y


def ffn(h, g, w_in, w_out, final_g=None, *, tm=512, tf=512):
    t, d = h.shape
    f = w_out.shape[0]
    nf = f // tf
    final_norm = final_g is not None
    fg = (final_g if final_norm else g).reshape(1, d)
    return pl.pallas_call(
        functools.partial(_ffn_kernel, final_norm=final_norm),
        grid=(t // tm, nf),
        in_specs=[
            pl.BlockSpec((tm, d), lambda i, j: (i, 0)),
            pl.BlockSpec((1, d), lambda i, j: (0, 0)),
            pl.BlockSpec((d, tf), lambda i, j: (0, j)),
            pl.BlockSpec((d, tf), lambda i, j: (0, j + nf)),
            pl.BlockSpec((tf, d), lambda i, j: (j, 0)),
            pl.BlockSpec((1, d), lambda i, j: (0, 0)),
        ],
        out_specs=pl.BlockSpec((tm, d), lambda i, j: (i, 0)),
        out_shape=jax.ShapeDtypeStruct((t, d), F32),
        scratch_shapes=[pltpu.VMEM((tm, d), BF16), pltpu.VMEM((tm, d), F32)],
        compiler_params=_params("parallel", "arbitrary"),
        name="ffn",
    )(h, g.reshape(1, d), w_in, w_in, w_out, fg)


def _matmul_res_kernel(a_ref, w_ref, r_ref, o_ref, ab_ref):
    @pl.when(pl.program_id(1) == 0)
    def _():
        ab_ref[...] = a_ref[...].astype(BF16)

    o_ref[...] = r_ref[...] + jnp.dot(ab_ref[...], w_ref[...], preferred_element_type=F32)


def matmul_res(a, w, r, *, tm, tn, name):
    t, k = a.shape
    n = w.shape[1]
    return pl.pallas_call(
        _matmul_res_kernel,
        grid=(t // tm, n // tn),
        in_specs=[
            pl.BlockSpec((tm, k), lambda i, j: (i, 0)),
            pl.BlockSpec((k, tn), lambda i, j: (0, j)),
            pl.BlockSpec((tm, tn), lambda i, j: (i, j)),
        ],
        out_specs=pl.BlockSpec((tm, tn), lambda i, j: (i, j)),
        out_shape=jax.ShapeDtypeStruct((t, n), F32),
        scratch_shapes=[pltpu.VMEM((tm, k), BF16)],
        compiler_params=_params("parallel", "arbitrary"),
        name=name,
    )(a, w, r)


def _conv_kernel(prev_ref, cur_ref, w_ref, o_ref, *, n_norm_blocks):
    i = pl.program_id(0)
    j = pl.program_id(1)
    tt, tc = cur_ref.shape
    halo = prev_ref.shape[0]
    prev = jnp.where(i == 0, 0.0, prev_ref[...])
    x = jnp.concatenate([prev, cur_ref[...]], axis=0)
    w = w_ref[...]
    y = w[GDN_CONV_WIDTH - 1:GDN_CONV_WIDTH] * x[halo:]
    for s in range(1, GDN_CONV_WIDTH):
        tap = GDN_CONV_WIDTH - 1 - s
        y = y + w[tap:tap + 1] * pltpu.roll(x, s, 0)[halo:]
    y = _silu(y)

    @pl.when(j < n_norm_blocks)
    def _():
        for hh in range(tc // GDN_HEAD_DIM):
            seg = y[:, hh * GDN_HEAD_DIM:(hh + 1) * GDN_HEAD_DIM]
            ss = jnp.sum(seg * seg, axis=-1, keepdims=True)
            o_ref[:, hh * GDN_HEAD_DIM:(hh + 1) * GDN_HEAD_DIM] = seg * lax.rsqrt(ss + RMS_EPS)

    @pl.when(j >= n_norm_blocks)
    def _():
        o_ref[...] = y


def gdn_conv(proj, conv_w, *, conv_dim, key_dim, tt=512, tc=512):
    t = proj.shape[0]
    halo = 8
    return pl.pallas_call(
        functools.partial(_conv_kernel, n_norm_blocks=2 * key_dim // tc),
        grid=(t // tt, conv_dim // tc),
        in_specs=[
            pl.BlockSpec((halo, tc), lambda i, j: (jnp.maximum(i * (tt // halo) - 1, 0), j)),
            pl.BlockSpec((tt, tc), lambda i, j: (i, j)),
            pl.BlockSpec((GDN_CONV_WIDTH, tc), lambda i, j: (0, j)),
        ],
        out_specs=pl.BlockSpec((tt, tc), lambda i, j: (i, j)),
        out_shape=jax.ShapeDtypeStruct((t, conv_dim), F32),
        compiler_params=_params("parallel", "parallel"),
        name="gdn_conv",
    )(proj, proj, conv_w)


def _gates_kernel(ba_ref, alog_ref, dtb_ref, beta_ref, g_ref):
    nh = beta_ref.shape[1]
    ba = ba_ref[...]
    beta_ref[...] = jax.nn.sigmoid(ba[:, :nh])
    z = ba[:, nh:] + dtb_ref[...]
    softplus = jnp.maximum(z, 0.0) + jnp.log1p(jnp.exp(-jnp.abs(z)))
    g_ref[...] = -jnp.exp(alog_ref[...]) * softplus


def gdn_gates(ba, a_log, dt_bias, *, tt=1024):
    t, two_h = ba.shape
    nh = two_h // 2
    return pl.pallas_call(
        _gates_kernel,
        grid=(t // tt,),
        in_specs=[
            pl.BlockSpec((tt, two_h), lambda i: (i, 0)),
            pl.BlockSpec((1, nh), lambda i: (0, 0)),
            pl.BlockSpec((1, nh), lambda i: (0, 0)),
        ],
        out_specs=[pl.BlockSpec((tt, nh), lambda i: (i, 0)), pl.BlockSpec((tt, nh), lambda i: (i, 0))],
        out_shape=[jax.ShapeDtypeStruct((t, nh), F32), jax.ShapeDtypeStruct((t, nh), F32)],
        compiler_params=_params("parallel"),
        name="gdn_gates",
    )(ba, a_log.reshape(1, nh), dt_bias.reshape(1, nh))


def _split_bf16(x):
    hi = x.astype(BF16)
    lo = (x - hi.astype(F32)).astype(BF16)
    return hi, lo


def _dot3(a, b):
    a_hi, a_lo = _split_bf16(a)
    b_hi, b_lo = _split_bf16(b)
    d = functools.partial(jnp.dot, preferred_element_type=F32)
    return d(a_hi, b_hi) + (d(a_hi, b_lo) + d(a_lo, b_hi))


def _dot_nt(a, b):
    return lax.dot_general(a, b, (((1,), (1,)), ((), ())), preferred_element_type=F32)


def _prep_kernel(q_ref, k_ref, v_ref, gsel_ref, bsel_ref, grow_ref,
                 u_ref, w_ref, qg_ref, kup_ref, qk_ref):
    h = pl.program_id(0)
    tp, nh = gsel_ref.shape
    c = GDN_CHUNK
    scale = GDN_HEAD_DIM ** -0.5
    lane = lax.broadcasted_iota(jnp.int32, (tp, nh), 1)
    gcol_all = jnp.sum(jnp.where(lane == h, gsel_ref[...], 0.0), axis=1, keepdims=True)
    bcol_all = jnp.sum(jnp.where(lane == h, bsel_ref[...], 0.0), axis=1, keepdims=True)
    grow_all = grow_ref[...]
    ii = lax.broadcasted_iota(jnp.int32, (c, c), 0)
    jj = lax.broadcasted_iota(jnp.int32, (c, c), 1)
    incl = ii >= jj
    strict = ii > jj
    eye = jnp.where(ii == jj, 1.0, 0.0).astype(F32)
    for n in range(tp // c):
        r = slice(n * c, (n + 1) * c)
        q = q_ref[r, :] * scale
        k = k_ref[r, :]
        v = v_ref[r, :]
        gcol = gcol_all[r]
        bcol = bcol_all[r]
        grow = grow_all[:, r]
        gc_col = jnp.sum(jnp.where(incl, grow, 0.0), axis=1, keepdims=True)
        gc_row = jnp.sum(jnp.where(ii <= jj, gcol, 0.0), axis=0, keepdims=True)
        g_last = jnp.sum(grow, axis=1, keepdims=True)
        decay = jnp.where(incl, jnp.exp(jnp.where(incl, gc_col - gc_row, 0.0)), 0.0)
        kb = k * bcol
        lmat = jnp.where(strict, _dot_nt(kb.astype(BF16), k.astype(BF16)) * decay, 0.0)
        p = -lmat
        tinv = eye + p
        m = 1
        while 2 * m < c:
            p = _dot3(p, p)
            tinv = tinv + _dot3(tinv, p)
            m *= 2
        eg = jnp.exp(gc_col)
        rhs = jnp.concatenate([v * bcol, kb * eg], axis=1)
        uw = _dot3(tinv, rhs)
        u_ref[r, :] = uw[:, :GDN_HEAD_DIM]
        w_ref[r, :] = uw[:, GDN_HEAD_DIM:]
        qk_ref[r, :] = jnp.where(incl, _dot_nt(q.astype(BF16), k.astype(BF16)) * decay, 0.0)
        qg_ref[r, :] = q * eg
        kup_ref[r, :] = k * jnp.exp(g_last - gc_col)


def gdn_prep(qkv, g, beta, g_rows, *, n_heads, rep, tp=512):
    t = qkv.shape[0]
    dh = GDN_HEAD_DIM
    nk = n_heads // rep
    hd = lambda shape: pl.BlockSpec((None,) + shape, lambda h, i: (h, i, 0))
    out = jax.ShapeDtypeStruct((n_heads, t, dh), F32)
    return pl.pallas_call(
        _prep_kernel,
        grid=(n_heads, t // tp),
        in_specs=[
            pl.BlockSpec((tp, dh), lambda h, i: (i, h // rep)),
            pl.BlockSpec((tp, dh), lambda h, i: (i, nk + h // rep)),
            pl.BlockSpec((tp, dh), lambda h, i: (i, 2 * nk + h)),
            pl.BlockSpec((tp, n_heads), lambda h, i: (i, 0)),
            pl.BlockSpec((tp, n_heads), lambda h, i: (i, 0)),
            pl.BlockSpec((None, 1, tp), lambda h, i: (h, 0, i)),
        ],
        out_specs=[hd((tp, dh)), hd((tp, dh)), hd((tp, dh)), hd((tp, dh)), hd((tp, GDN_CHUNK))],
        out_shape=[out, out, out, out, jax.ShapeDtypeStruct((n_heads, t, GDN_CHUNK), F32)],
        compiler_params=_params("parallel", "parallel"),
        name="gdn_prep",
    )(qkv, qkv, qkv, g, beta, g_rows)


def _scan_kernel(u_ref, w_ref, qg_ref, kup_ref, qk_ref, grow_ref, o_ref, s_ref):
    hb, tp, dh = u_ref.shape
    c = GDN_CHUNK

    @pl.when(pl.program_id(1) == 0)
    def _():
        s_ref[...] = jnp.zeros_like(s_ref)

    dot = functools.partial(jnp.dot, preferred_element_type=F32)
    for hh in range(hb):
        state = s_ref[hh]
        for n in range(tp // c):
            r = slice(n * c, (n + 1) * c)
            sb = state.astype(BF16)
            v_new = u_ref[hh, r, :] - dot(w_ref[hh, r, :].astype(BF16), sb)
            vb = v_new.astype(BF16)
            o = dot(qg_ref[hh, r, :].astype(BF16), sb) + dot(qk_ref[hh, r, :].astype(BF16), vb)
            o_ref[r, hh * dh:(hh + 1) * dh] = o
            gl = jnp.exp(jnp.sum(grow_ref[hh, :, r], axis=1, keepdims=True))
            state = state * gl + lax.dot_general(kup_ref[hh, r, :].astype(BF16), vb,
                                                 (((0,), (0,)), ((), ())), preferred_element_type=F32)
        s_ref[hh] = state


def gdn_scan(u, w, qg, kup, qk, g_rows, *, hb=4, tp=512):
    n_heads, t, dh = u.shape
    blk = lambda last: pl.BlockSpec((hb, tp, last), lambda h, i: (h, i, 0))
    return pl.pallas_call(
        _scan_kernel,
        grid=(n_heads // hb, t // tp),
        in_specs=[blk(dh), blk(dh), blk(dh), blk(dh), blk(GDN_CHUNK),
                  pl.BlockSpec((hb, 1, tp), lambda h, i: (h, 0, i))],
        out_specs=pl.BlockSpec((tp, hb * dh), lambda h, i: (i, h)),
        out_shape=jax.ShapeDtypeStruct((t, n_heads * dh), F32),
        scratch_shapes=[pltpu.VMEM((hb, dh, dh), F32)],
        compiler_params=_params("parallel", "arbitrary"),
        name="gdn_scan",
    )(u, w, qg, kup, qk, g_rows)


def _gdn_out_kernel(o_ref, z_ref, gn_ref, w_ref, r_ref, y_ref, a_ref):
    @pl.when(pl.program_id(1) == 0)
    def _():
        gn = gn_ref[...]
        for hh in range(o_ref.shape[1] // GDN_HEAD_DIM):
            sl = slice(hh * GDN_HEAD_DIM, (hh + 1) * GDN_HEAD_DIM)
            a_ref[:, sl] = (_rms_rows(o_ref[:, sl], gn) * _silu(z_ref[:, sl])).astype(BF16)

    y_ref[...] = r_ref[...] + jnp.dot(a_ref[...], w_ref[...], preferred_element_type=F32)


def gdn_out(o, proj, z_col_block, out_norm, w_out, h, *, tm=256, tn=512):
    t, vd = o.shape
    n = w_out.shape[1]
    return pl.pallas_call(
        _gdn_out_kernel,
        grid=(t // tm, n // tn),
        in_specs=[
            pl.BlockSpec((tm, vd), lambda i, j: (i, 0)),
            pl.BlockSpec((tm, vd), lambda i, j: (i, z_col_block)),
            pl.BlockSpec((1, GDN_HEAD_DIM), lambda i, j: (0, 0)),
            pl.BlockSpec((vd, tn), lambda i, j: (0, j)),
            pl.BlockSpec((tm, tn), lambda i, j: (i, j)),
        ],
        out_specs=pl.BlockSpec((tm, tn), lambda i, j: (i, j)),
        out_shape=jax.ShapeDtypeStruct((t, n), F32),
        scratch_shapes=[pltpu.VMEM((tm, vd), BF16)],
        compiler_params=_params("parallel", "arbitrary"),
        name="gdn_out",
    )(o, proj, out_norm.reshape(1, GDN_HEAD_DIM), w_out, h)


def _kmeans_kernel(k_ref, o_ref):
    o_ref[...] = jnp.mean(k_ref[...], axis=0, keepdims=True)


def moba_kmeans(k):
    t, kd = k.shape
    nb = t // MOBA_BLOCK
    out = pl.pallas_call(
        _kmeans_kernel,
        grid=(nb,),
        in_specs=[pl.BlockSpec((MOBA_BLOCK, kd), lambda i: (i, 0))],
        out_specs=pl.BlockSpec((None, 1, kd), lambda i: (i, 0, 0)),
        out_shape=jax.ShapeDtypeStruct((nb, 1, kd), F32),
        compiler_params=_params("parallel"),
        name="moba_kmeans",
    )(k)
    return out.reshape(nb, kd)


def _rel_bucket_table(n_dist):
    n = np.arange(n_dist)
    max_exact = N_REL_BUCKETS // 2
    ratio = np.log(np.maximum(n, max_exact).astype(np.float32) / np.float32(max_exact)) \
        / np.float32(math.log(REL_MAX_DISTANCE / max_exact))
    large = np.minimum(max_exact + (ratio.astype(np.float32) * (N_REL_BUCKETS - max_exact)).astype(np.int32),
                       N_REL_BUCKETS - 1)
    return np.where(n < max_exact, n, large).astype(np.int32)


def _bucket_tiles():
    table = _rel_bucket_table(2 * MOBA_BLOCK)
    a = np.arange(MOBA_BLOCK)[:, None]
    b = np.arange(MOBA_BLOCK)[None, :]
    d_own = b - a
    own = np.where(d_own >= 0, table[np.maximum(d_own, 0)], -1)
    prev = table[MOBA_BLOCK + b - a]
    return np.stack([own, prev]).astype(np.int32)


def _moba_kernel(rb_ref, qt_ref, k_ref, vt_ref, km_ref, bucket_ref, o_ref,
                 bias_ref, sel_ref, m_ref, l_ref, acc_ref, *, group):
    h = pl.program_id(0)
    i = pl.program_id(1)
    bs = MOBA_BLOCK
    nb = km_ref.shape[0]
    scale = ATT_HEAD_DIM ** -0.5
    dot = functools.partial(jnp.dot, preferred_element_type=F32)

    @pl.when(i == 0)
    def _():
        for t in range(2):
            bk = bucket_ref[t]
            bias = jnp.full(bk.shape, MASK_VALUE, F32)
            for b in range(N_REL_BUCKETS):
                bias = jnp.where(bk == b, rb_ref[b, h], bias)
            bias_ref[t] = bias

    qt = (qt_ref[...] * scale).astype(BF16)

    blk = lax.broadcasted_iota(jnp.int32, (nb, bs), 0)
    gate = jnp.where(blk < i, dot(km_ref[...].astype(BF16), qt), -jnp.inf)
    sel = jnp.zeros((nb, bs), F32)
    for r in range(MOBA_TOPK):
        mx = jnp.max(gate, axis=0, keepdims=True)
        first = jnp.min(jnp.where(gate == mx, blk, nb), axis=0, keepdims=True)
        pick = blk == first
        sel = jnp.where(jnp.logical_and(pick, r < i), 1.0, sel)
        gate = jnp.where(pick, -jnp.inf, gate)
    sel_ref[...] = sel

    def keys(j):
        return k_ref[pl.ds(pl.multiple_of(j * bs, bs), bs), :].astype(BF16)

    def values_t(j):
        return vt_ref[:, pl.ds(pl.multiple_of(j * bs, bs), bs)].astype(BF16)

    s = dot(keys(i), qt) + bias_ref[0]
    m0 = jnp.max(s, axis=0, keepdims=True)
    p = jnp.exp(s - m0)
    m_ref[...] = m0
    l_ref[...] = jnp.sum(p, axis=0, keepdims=True)
    acc_ref[...] = dot(values_t(i), p.astype(BF16))

    def update(j, s):
        s = s + jnp.where(sel_ref[pl.ds(j, 1), :] > 0.0, 0.0, MASK_VALUE)
        m_old = m_ref[...]
        m_new = jnp.maximum(m_old, jnp.max(s, axis=0, keepdims=True))
        alpha = jnp.exp(m_old - m_new)
        p = jnp.exp(s - m_new)
        m_ref[...] = m_new
        l_ref[...] = alpha * l_ref[...] + jnp.sum(p, axis=0, keepdims=True)
        acc_ref[...] = alpha * acc_ref[...] + dot(values_t(j), p.astype(BF16))

    @pl.when(i >= 1)
    def _():
        update(i - 1, dot(keys(i - 1), qt) + bias_ref[1])

    far_bias = rb_ref[N_REL_BUCKETS - 1, h]

    def far_body(j, carry):
        update(j, dot(keys(j), qt) + far_bias)
        return carry

    lax.fori_loop(0, i - 1, far_body, 0)

    o_ref[...] = (acc_ref[...] / l_ref[...]).T


def moba_attention(qt, k, vt, k_means, rel_bias, *, n_heads, n_kv_heads):
    dh = ATT_HEAD_DIM
    t = k.shape[0]
    bs = MOBA_BLOCK
    nb = t // bs
    group = n_heads // n_kv_heads
    assert REL_MAX_DISTANCE <= bs
    buckets = jnp.asarray(_bucket_tiles())
    return pl.pallas_call(
        functools.partial(_moba_kernel, group=group),
        grid=(n_heads, nb),
        in_specs=[
            pl.BlockSpec(memory_space=pltpu.SMEM),
            pl.BlockSpec((dh, bs), lambda h, i: (h, i)),
            pl.BlockSpec((t, dh), lambda h, i: (0, h // group)),
            pl.BlockSpec((dh, t), lambda h, i: (h // group, 0)),
            pl.BlockSpec((nb, dh), lambda h, i: (0, h // group)),
            pl.BlockSpec((2, bs, bs), lambda h, i: (0, 0, 0)),
        ],
        out_specs=pl.BlockSpec((bs, dh), lambda h, i: (i, h)),
        out_shape=jax.ShapeDtypeStruct((t, n_heads * dh), F32),
        scratch_shapes=[
            pltpu.VMEM((2, bs, bs), F32),
            pltpu.VMEM((nb, bs), F32),
            pltpu.VMEM((1, bs), F32),
            pltpu.VMEM((1, bs), F32),
            pltpu.VMEM((dh, bs), F32),
        ],
        compiler_params=_params("parallel", "arbitrary"),
        name="moba_attention",
    )(rel_bias, qt, k, vt, k_means, buckets)


def kernel(x, ffn1_norm, ffn1_w_in, ffn1_w_out, mix_norm, ffn2_norm, ffn2_w_in, ffn2_w_out,
           gdn_w_in, gdn_conv_w, gdn_a_log, gdn_dt_bias, gdn_out_norm, gdn_w_out,
           kv_norm, w_kv, moba_w_q, moba_w_o, rel_bias, final_norm):
    batch, seq, d_model = x.shape
    depth = ffn1_norm.shape[0]
    n_a = gdn_w_in.shape[0]
    n_v_heads = gdn_a_log.shape[1]
    value_dim = gdn_w_out.shape[1]
    conv_dim = gdn_conv_w.shape[2]
    key_dim = (conv_dim - value_dim) // 2
    rep = value_dim // key_dim
    n_heads = moba_w_q.shape[2] // ATT_HEAD_DIM
    kv_dim = w_kv.shape[1] // 2
    n_kv_heads = kv_dim // ATT_HEAD_DIM
    bf = lambda a: a.astype(BF16)

    outs = []
    for bi in range(batch):
        h = x[bi]
        k_nat = vt = k_means = None
        for layer in range(depth):
            if layer == n_a:
                k_nat = norm_matmul(h, kv_norm, bf(w_kv[:, :kv_dim]), tm=512, tn=kv_dim, name="kv_k")
                vt = norm_matmul_t(h, kv_norm, bf(w_kv[:, kv_dim:].T), tm=512, tn=kv_dim, name="kv_vt")
                k_means = moba_kmeans(k_nat)
            h = ffn(h, ffn1_norm[layer], bf(ffn1_w_in[layer]), bf(ffn1_w_out[layer]))
            if layer < n_a:
                w_in = gdn_w_in[layer]
                main = conv_dim + value_dim
                proj = norm_matmul(h, mix_norm[layer], bf(w_in[:, :main]), tm=512, tn=1024, name="gdn_in")
                ba = norm_matmul(h, mix_norm[layer], bf(w_in[:, main:]), tm=512, tn=2 * n_v_heads,
                                 name="gdn_in_ba")
                beta, g = gdn_gates(ba, gdn_a_log[layer], gdn_dt_bias[layer])
                qkv = gdn_conv(proj, gdn_conv_w[layer], conv_dim=conv_dim, key_dim=key_dim)
                g_rows = g.T.reshape(n_v_heads, 1, seq)
                u, w, qg, kup, qk = gdn_prep(qkv, g, beta, g_rows, n_heads=n_v_heads, rep=rep)
                o = gdn_scan(u, w, qg, kup, qk, g_rows)
                h = gdn_out(o, proj, conv_dim // value_dim, gdn_out_norm[layer], bf(gdn_w_out[layer]), h)
            else:
                j = layer - n_a
                qt = norm_matmul_t(h, mix_norm[layer], bf(moba_w_q[j].T), tm=512, tn=512, name="moba_qt")
                att = moba_attention(qt, k_nat, vt, k_means, rel_bias,
                                     n_heads=n_heads, n_kv_heads=n_kv_heads)
                h = matmul_res(att, bf(moba_w_o[j]), h, tm=512, tn=512, name="moba_out")
            last = layer == depth - 1
            h = ffn(h, ffn2_norm[layer], bf(ffn2_w_in[layer]), bf(ffn2_w_out[layer]),
                    final_norm if last else None)
        outs.append(h)
    return jnp.stack(outs)
```

```python
import functools
import math

import numpy as np
import jax
import jax.numpy as jnp
from jax import lax
from jax.experimental import pallas as pl
from jax.experimental.pallas import tpu as pltpu

F32 = jnp.float32
BF16 = jnp.bfloat16

RMS_EPS = 1e-6
GDN_HEAD_DIM = 128
GDN_CHUNK = 64
GDN_CONV_WIDTH = 4
ATT_HEAD_DIM = 128
MOBA_BLOCK = 256
MOBA_TOPK = 3
N_REL_BUCKETS = 32
REL_MAX_DISTANCE = 128
MASK_VALUE = -1e30
LOG2_E = 1.4426950408889634

VMEM_LIMIT_BYTES = 52 * 1024 * 1024


def _params(*sem):
    return pltpu.CompilerParams(dimension_semantics=sem, vmem_limit_bytes=VMEM_LIMIT_BYTES)


def _rms_rows(x, g):
    ms = jnp.mean(x * x, axis=-1, keepdims=True)
    return x * lax.rsqrt(ms + RMS_EPS) * g


def _silu(x):
    return x * jax.nn.sigmoid(x)


def _dot(a, b):
    return jnp.dot(a, b, preferred_element_type=F32)


def _dot_nt(a, b):
    return lax.dot_general(a, b, (((1,), (1,)), ((), ())), preferred_element_type=F32)


def _norm_matmul_kernel(x_ref, g_ref, w_ref, o_ref, xn_ref):
    @pl.when(pl.program_id(1) == 0)
    def _():
        xn_ref[...] = _rms_rows(x_ref[...], g_ref[...]).astype(BF16)

    o_ref[...] = _dot(xn_ref[...], w_ref[...]).astype(o_ref.dtype)


def norm_matmul(x, g, w, *, tm, tn, name):
    t, d = x.shape
    n = w.shape[1]
    return pl.pallas_call(
        _norm_matmul_kernel,
        grid=(t // tm, n // tn),
        in_specs=[
            pl.BlockSpec((tm, d), lambda i, j: (i, 0)),
            pl.BlockSpec((1, d), lambda i, j: (0, 0)),
            pl.BlockSpec((d, tn), lambda i, j: (0, j)),
        ],
        out_specs=pl.BlockSpec((tm, tn), lambda i, j: (i, j)),
        out_shape=jax.ShapeDtypeStruct((t, n), F32),
        scratch_shapes=[pltpu.VMEM((tm, d), BF16)],
        compiler_params=_params("parallel", "arbitrary"),
        name=name,
    )(x, g.reshape(1, d), w)


def _norm_matmul_t_kernel(x_ref, g_ref, wt_ref, o_ref, xn_ref):
    @pl.when(pl.program_id(1) == 0)
    def _():
        xn_ref[...] = _rms_rows(x_ref[...], g_ref[...]).astype(BF16)

    o_ref[...] = _dot_nt(wt_ref[...], xn_ref[...]).astype(o_ref.dtype)


def norm_matmul_t(x, g, wt, *, tm, tn, name, out_dtype=F32):
    t, d = x.shape
    n = wt.shape[0]
    return pl.pallas_call(
        _norm_matmul_t_kernel,
        grid=(t // tm, n // tn),
        in_specs=[
            pl.BlockSpec((tm, d), lambda i, j: (i, 0)),
            pl.BlockSpec((1, d), lambda i, j: (0, 0)),
            pl.BlockSpec((tn, d), lambda i, j: (j, 0)),
        ],
        out_specs=pl.BlockSpec((tn, tm), lambda i, j: (j, i)),
        out_shape=jax.ShapeDtypeStruct((n, t), out_dtype),
        scratch_shapes=[pltpu.VMEM((tm, d), BF16)],
        compiler_params=_params("parallel", "arbitrary"),
        name=name,
    )(x, g.reshape(1, d), wt)


def _ffn_kernel(h_ref, g_ref, wg_ref, wu_ref, wo_ref, fg_ref, o_ref, xn_ref, acc_ref, *, final_norm):
    j = pl.program_id(1)

    @pl.when(j == 0)
    def _():
        xn_ref[...] = _rms_rows(h_ref[...], g_ref[...]).astype(BF16)
        acc_ref[...] = jnp.zeros_like(acc_ref)

    xn = xn_ref[...]
    gate = _dot(xn, wg_ref[...])
    up = _dot(xn, wu_ref[...])
    act = (_silu(gate) * up).astype(BF16)
    acc_ref[...] += _dot(act, wo_ref[...])

    @pl.when(j == pl.num_programs(1) - 1)
    def _():
        y = h_ref[...] + 0.5 * acc_ref[...]
        if final_norm:
            y = _rms_rows(y, fg_ref[...])
        o_ref[...] = y


def ffn(h, g, w_in, w_out, final_g=None, *, tm=512, tf=512):
    t, d = h.shape
    f = w_out.shape[0]
    nf = f // tf
    final_norm = final_g is not None
    fg = (final_g if final_norm else g).reshape(1, d)
    return pl.pallas_call(
        functools.partial(_ffn_kernel, final_norm=final_norm),
        grid=(t // tm, nf),
        in_specs=[
            pl.BlockSpec((tm, d), lambda i, j: (i, 0)),
            pl.BlockSpec((1, d), lambda i, j: (0, 0)),
            pl.BlockSpec((d, tf), lambda i, j: (0, j)),
            pl.BlockSpec((d, tf), lambda i, j: (0, j + nf)),
            pl.BlockSpec((tf, d), lambda i, j: (j, 0)),
            pl.BlockSpec((1, d), lambda i, j: (0, 0)),
        ],
        out_specs=pl.BlockSpec((tm, d), lambda i, j: (i, 0)),
        out_shape=jax.ShapeDtypeStruct((t, d), F32),
        scratch_shapes=[pltpu.VMEM((tm, d), BF16), pltpu.VMEM((tm, d), F32)],
        compiler_params=_params("parallel", "arbitrary"),
        name="ffn",
    )(h, g.reshape(1, d), w_in, w_in, w_out, fg)


def _matmul_res_kernel(a_ref, w_ref, r_ref, o_ref, ab_ref):
    @pl.when(pl.program_id(1) == 0)
    def _():
        ab_ref[...] = a_ref[...].astype(BF16)

    o_ref[...] = r_ref[...] + _dot(ab_ref[...], w_ref[...])


def matmul_res(a, w, r, *, tm, tn, name):
    t, k = a.shape
    n = w.shape[1]
    return pl.pallas_call(
        _matmul_res_kernel,
        grid=(t // tm, n // tn),
        in_specs=[
            pl.BlockSpec((tm, k), lambda i, j: (i, 0)),
            pl.BlockSpec((k, tn), lambda i, j: (0, j)),
            pl.BlockSpec((tm, tn), lambda i, j: (i, j)),
        ],
        out_specs=pl.BlockSpec((tm, tn), lambda i, j: (i, j)),
        out_shape=jax.ShapeDtypeStruct((t, n), F32),
        scratch_shapes=[pltpu.VMEM((tm, k), BF16)],
        compiler_params=_params("parallel", "arbitrary"),
        name=name,
    )(a, w, r)


def _conv_kernel(prev_ref, cur_ref, w_ref, o_ref, *, n_norm_blocks):
    i = pl.program_id(0)
    j = pl.program_id(1)
    tt, tc = cur_ref.shape
    halo = prev_ref.shape[0]
    prev = jnp.where(i == 0, 0.0, prev_ref[...])
    x = jnp.concatenate([prev, cur_ref[...]], axis=0)
    w = w_ref[...]
    y = w[GDN_CONV_WIDTH - 1:GDN_CONV_WIDTH] * x[halo:]
    for s in range(1, GDN_CONV_WIDTH):
        tap = GDN_CONV_WIDTH - 1 - s
        y = y + w[tap:tap + 1] * pltpu.roll(x, s, 0)[halo:]
    y = _silu(y)

    @pl.when(j < n_norm_blocks)
    def _():
        for hh in range(tc // GDN_HEAD_DIM):
            seg = y[:, hh * GDN_HEAD_DIM:(hh + 1) * GDN_HEAD_DIM]
            ss = jnp.sum(seg * seg, axis=-1, keepdims=True)
            o_ref[:, hh * GDN_HEAD_DIM:(hh + 1) * GDN_HEAD_DIM] = seg * lax.rsqrt(ss + RMS_EPS)

    @pl.when(j >= n_norm_blocks)
    def _():
        o_ref[...] = y


def gdn_conv(proj, conv_w, *, conv_dim, key_dim, tt=512, tc=512):
    t = proj.shape[0]
    halo = 8
    return pl.pallas_call(
        functools.partial(_conv_kernel, n_norm_blocks=2 * key_dim // tc),
        grid=(t // tt, conv_dim // tc),
        in_specs=[
            pl.BlockSpec((halo, tc), lambda i, j: (jnp.maximum(i * (tt // halo) - 1, 0), j)),
            pl.BlockSpec((tt, tc), lambda i, j: (i, j)),
            pl.BlockSpec((GDN_CONV_WIDTH, tc), lambda i, j: (0, j)),
        ],
        out_specs=pl.BlockSpec((tt, tc), lambda i, j: (i, j)),
        out_shape=jax.ShapeDtypeStruct((t, conv_dim), F32),
        compiler_params=_params("parallel", "parallel"),
        name="gdn_conv",
    )(proj, proj, conv_w)


def _gates_kernel(ba_ref, alog_ref, dtb_ref, beta_ref, g_ref):
    nh = beta_ref.shape[1]
    ba = ba_ref[...]
    beta_ref[...] = jax.nn.sigmoid(ba[:, :nh])
    z = ba[:, nh:] + dtb_ref[...]
    softplus = jnp.maximum(z, 0.0) + jnp.log1p(jnp.exp(-jnp.abs(z)))
    g_ref[...] = -jnp.exp(alog_ref[...]) * softplus


def gdn_gates(ba, a_log, dt_bias, *, tt=1024):
    t, two_h = ba.shape
    nh = two_h // 2
    return pl.pallas_call(
        _gates_kernel,
        grid=(t // tt,),
        in_specs=[
            pl.BlockSpec((tt, two_h), lambda i: (i, 0)),
            pl.BlockSpec((1, nh), lambda i: (0, 0)),
            pl.BlockSpec((1, nh), lambda i: (0, 0)),
        ],
        out_specs=[pl.BlockSpec((tt, nh), lambda i: (i, 0)), pl.BlockSpec((tt, nh), lambda i: (i, 0))],
        out_shape=[jax.ShapeDtypeStruct((t, nh), F32), jax.ShapeDtypeStruct((t, nh), F32)],
        compiler_params=_params("parallel"),
        name="gdn_gates",
    )(ba, a_log.reshape(1, nh), dt_bias.reshape(1, nh))


def _prep_kernel(q_ref, k_ref, v_ref, gsel_ref, bsel_ref, grow_ref, wq_ref, l2_ref, u_ref, glw_ref):
    kh = pl.program_id(0)
    tp, nh = gsel_ref.shape
    c = GDN_CHUNK
    dh = GDN_HEAD_DIM
    scale = dh ** -0.5
    lane_h = lax.broadcasted_iota(jnp.int32, (tp, nh), 1)

    def column(ref, head):
        return jnp.sum(jnp.where(lane_h == head, ref[...], 0.0), axis=1, keepdims=True)

    gcol = [column(gsel_ref, 2 * kh + hh) for hh in range(2)]
    bcol = [column(bsel_ref, 2 * kh + hh) for hh in range(2)]
    grow = [grow_ref[hh] for hh in range(2)]

    ii = lax.broadcasted_iota(jnp.int32, (c, 2 * c), 0)
    ll = lax.broadcasted_iota(jnp.int32, (c, 2 * c), 1)
    jj = jnp.bitwise_and(ll, c - 1)
    head_a = ll < c
    incl = ii >= jj
    strict = ii > jj
    wi = lax.broadcasted_iota(jnp.int32, (c, 4 * c), 0)
    wl = lax.broadcasted_iota(jnp.int32, (c, 4 * c), 1)
    eye_w = jnp.where(wi == jnp.bitwise_and(wl, c - 1), 1.0, 0.0).astype(F32)
    br = lax.broadcasted_iota(jnp.int32, (4 * c, 4 * c), 0)
    bc = lax.broadcasted_iota(jnp.int32, (4 * c, 4 * c), 1)
    same_block = (br // c) == (bc // c)
    strict_block = jnp.logical_and(same_block, br != bc)

    def block_diag(packed):
        return jnp.concatenate([packed] * 4, axis=0)

    def packed_matmul(a, b):
        bd = jnp.where(same_block, block_diag(b), 0.0).astype(BF16)
        return _dot(a.astype(BF16), bd)

    n_half = tp // (2 * c)
    lmats = [[] for _ in range(n_half)]
    rhs_blocks = [[] for _ in range(n_half)]
    for half in range(n_half):
        for y in range(2):
            n = 2 * half + y
            r = slice(n * c, (n + 1) * c)
            q = q_ref[r, :] * scale
            k = k_ref[r, :]
            g_pair = jnp.where(head_a, gcol[0][r], gcol[1][r])
            b_pair = jnp.where(head_a, bcol[0][r], bcol[1][r])
            grow_pair = jnp.concatenate([grow[0][:, r], grow[1][:, r]], axis=1)
            tri = jnp.where(incl, grow_pair, 0.0)
            gc = [jnp.sum(jnp.where(head_a, tri, 0.0), axis=1, keepdims=True),
                  jnp.sum(jnp.where(head_a, 0.0, tri), axis=1, keepdims=True)]
            gc_col = jnp.where(head_a, gc[0], gc[1])
            gc_row = jnp.sum(jnp.where(ii <= jj, g_pair, 0.0), axis=0, keepdims=True)
            decay = jnp.where(incl, jnp.exp(jnp.where(incl, gc_col - gc_row, 0.0)), 0.0)
            kb16 = k.astype(BF16)
            gram = _dot_nt(jnp.concatenate([q.astype(BF16), kb16], axis=0),
                           jnp.concatenate([kb16, kb16], axis=0))
            l2_ref[n, 0:c, :] = jnp.where(incl, gram[:c] * decay, 0.0).astype(BF16)
            lmats[half].append(jnp.where(strict, gram[c:] * b_pair * decay, 0.0))
            kups = []
            for hh in range(2):
                g_last = jnp.sum(gcol[hh][r], axis=0, keepdims=True)
                eg = jnp.exp(gc[hh])
                beta = bcol[hh][r]
                kups.append(k * jnp.exp(g_last - gc[hh]))
                wq_ref[n, c:2 * c, hh * dh:(hh + 1) * dh] = (q * eg).astype(BF16)
                glw_ref[n, :, hh * dh:(hh + 1) * dh] = jnp.broadcast_to(jnp.exp(g_last), (8, dh))
                rhs_blocks[half].append(
                    jnp.concatenate([v_ref[r, hh * dh:(hh + 1) * dh] * beta, k * (beta * eg)], axis=1))
            l2_ref[n, c:c + dh, :] = jnp.concatenate(kups, axis=0).T.astype(BF16)

    ps = [-jnp.concatenate(lm, axis=1) for lm in lmats]
    tinvs = [eye_w + p for p in ps]
    m = 1
    while 2 * m < c:
        ps = [packed_matmul(p, p) for p in ps]
        tinvs = [t + packed_matmul(t, p) for t, p in zip(tinvs, ps)]
        m *= 2
    for half in range(n_half):
        rhs_all = jnp.concatenate(rhs_blocks[half], axis=0)
        t_off = jnp.where(strict_block, block_diag(tinvs[half]), 0.0).astype(BF16)
        uw = rhs_all + _dot(t_off, rhs_all.astype(BF16))
        for y in range(2):
            n = 2 * half + y
            for hh in range(2):
                blk = uw[(2 * y + hh) * c:(2 * y + hh + 1) * c]
                u_ref[n * c:(n + 1) * c, hh * dh:(hh + 1) * dh] = blk[:, :dh]
                wq_ref[n, 0:c, hh * dh:(hh + 1) * dh] = blk[:, dh:].astype(BF16)


def gdn_prep(qkv, g, beta, g_rows, *, n_heads, rep, tp=512):
    assert rep == 2
    t = qkv.shape[0]
    dh = GDN_HEAD_DIM
    c = GDN_CHUNK
    nk = n_heads // rep
    nc = tp // c
    return pl.pallas_call(
        _prep_kernel,
        grid=(nk, t // tp),
        in_specs=[
            pl.BlockSpec((tp, dh), lambda h, i: (i, h)),
            pl.BlockSpec((tp, dh), lambda h, i: (i, nk + h)),
            pl.BlockSpec((tp, 2 * dh), lambda h, i: (i, nk + h)),
            pl.BlockSpec((tp, n_heads), lambda h, i: (i, 0)),
            pl.BlockSpec((tp, n_heads), lambda h, i: (i, 0)),
            pl.BlockSpec((2, 1, tp), lambda h, i: (h, 0, i)),
        ],
        out_specs=[
            pl.BlockSpec((None, nc, 2 * c, 2 * dh), lambda h, i: (h, i, 0, 0)),
            pl.BlockSpec((None, nc, c + dh, 2 * c), lambda h, i: (h, i, 0, 0)),
            pl.BlockSpec((None, tp, 2 * dh), lambda h, i: (h, i, 0)),
            pl.BlockSpec((None, nc, 8, 2 * dh), lambda h, i: (h, i, 0, 0)),
        ],
        out_shape=[
            jax.ShapeDtypeStruct((nk, t // c, 2 * c, 2 * dh), BF16),
            jax.ShapeDtypeStruct((nk, t // c, c + dh, 2 * c), BF16),
            jax.ShapeDtypeStruct((nk, t, 2 * dh), F32),
            jax.ShapeDtypeStruct((nk, t // c, 8, 2 * dh), F32),
        ],
        compiler_params=_params("parallel", "parallel"),
        name="gdn_prep",
    )(qkv, qkv, qkv, g, beta, g_rows)


def _scan_kernel(wq_ref, l2_ref, u_ref, glw_ref, o_ref, s_ref):
    pb, nc = wq_ref.shape[0], wq_ref.shape[1]
    c = GDN_CHUNK
    dh = GDN_HEAD_DIM

    @pl.when(pl.program_id(1) == 0)
    def _():
        s_ref[...] = jnp.zeros_like(s_ref)

    def pair_diag(x):
        z = jnp.zeros((x.shape[0], dh), x.dtype)
        return jnp.concatenate([jnp.concatenate([x[:, :dh], z], axis=1),
                                jnp.concatenate([z, x[:, dh:]], axis=1)], axis=0)

    states = [s_ref[p] for p in range(pb)]
    for n in range(nc):
        r = slice(n * c, (n + 1) * c)
        for p in range(pb):
            ws = _dot(wq_ref[p, n], pair_diag(states[p].astype(BF16)))
            v_new = (u_ref[p, r, :] - ws[:c]).astype(BF16)
            upd = _dot(l2_ref[p, n], pair_diag(v_new))
            o_ref[r, p * 2 * dh:(p + 1) * 2 * dh] = ws[c:] + upd[:c]
            states[p] = states[p] * glw_ref[p, n, 0:1, :] + upd[c:]
    for p in range(pb):
        s_ref[p] = states[p]


def gdn_scan(wq, l2, u, glw, *, pb=8, tp=256):
    nk, t, two_dh = u.shape
    c = GDN_CHUNK
    nc = tp // c
    blk4 = lambda a: pl.BlockSpec((pb, nc) + a.shape[2:], lambda h, i: (h, i, 0, 0))
    return pl.pallas_call(
        _scan_kernel,
        grid=(nk // pb, t // tp),
        in_specs=[blk4(wq), blk4(l2), pl.BlockSpec((pb, tp, two_dh), lambda h, i: (h, i, 0)), blk4(glw)],
        out_specs=pl.BlockSpec((tp, pb * two_dh), lambda h, i: (i, h)),
        out_shape=jax.ShapeDtypeStruct((t, nk * two_dh), F32),
        scratch_shapes=[pltpu.VMEM((pb, two_dh // 2, two_dh), F32)],
        compiler_params=_params("parallel", "arbitrary"),
        name="gdn_scan",
    )(wq, l2, u, glw)


def _gdn_out_kernel(o_ref, z_ref, gn_ref, w_ref, r_ref, y_ref, a_ref):
    @pl.when(pl.program_id(1) == 0)
    def _():
        gn = gn_ref[...]
        for hh in range(o_ref.shape[1] // GDN_HEAD_DIM):
            sl = slice(hh * GDN_HEAD_DIM, (hh + 1) * GDN_HEAD_DIM)
            a_ref[:, sl] = (_rms_rows(o_ref[:, sl], gn) * _silu(z_ref[:, sl])).astype(BF16)

    y_ref[...] = r_ref[...] + _dot(a_ref[...], w_ref[...])


def gdn_out(o, proj, z_col_block, out_norm, w_out, h, *, tm=256, tn=512):
    t, vd = o.shape
    n = w_out.shape[1]
    return pl.pallas_call(
        _gdn_out_kernel,
        grid=(t // tm, n // tn),
        in_specs=[
            pl.BlockSpec((tm, vd), lambda i, j: (i, 0)),
            pl.BlockSpec((tm, vd), lambda i, j: (i, z_col_block)),
            pl.BlockSpec((1, GDN_HEAD_DIM), lambda i, j: (0, 0)),
            pl.BlockSpec((vd, tn), lambda i, j: (0, j)),
            pl.BlockSpec((tm, tn), lambda i, j: (i, j)),
        ],
        out_specs=pl.BlockSpec((tm, tn), lambda i, j: (i, j)),
        out_shape=jax.ShapeDtypeStruct((t, n), F32),
        scratch_shapes=[pltpu.VMEM((tm, vd), BF16)],
        compiler_params=_params("parallel", "arbitrary"),
        name="gdn_out",
    )(o, proj, out_norm.reshape(1, GDN_HEAD_DIM), w_out, h)


def _kmeans_kernel(k_ref, mean_ref, kb_ref):
    k = k_ref[...]
    mean_ref[...] = jnp.mean(k, axis=0, keepdims=True)
    kb_ref[...] = k.astype(BF16)


def moba_kmeans(k):
    t, kd = k.shape
    nb = t // MOBA_BLOCK
    means, kb = pl.pallas_call(
        _kmeans_kernel,
        grid=(nb,),
        in_specs=[pl.BlockSpec((MOBA_BLOCK, kd), lambda i: (i, 0))],
        out_specs=[pl.BlockSpec((None, 1, kd), lambda i: (i, 0, 0)),
                   pl.BlockSpec((MOBA_BLOCK, kd), lambda i: (i, 0))],
        out_shape=[jax.ShapeDtypeStruct((nb, 1, kd), F32), jax.ShapeDtypeStruct((t, kd), BF16)],
        compiler_params=_params("parallel"),
        name="moba_kmeans",
    )(k)
    return means.reshape(nb, kd), kb


def _rel_bucket_table(n_dist):
    n = np.arange(n_dist)
    max_exact = N_REL_BUCKETS // 2
    ratio = np.log(np.maximum(n, max_exact).astype(np.float32) / np.float32(max_exact)) \
        / np.float32(math.log(REL_MAX_DISTANCE / max_exact))
    large = np.minimum(max_exact + (ratio.astype(np.float32) * (N_REL_BUCKETS - max_exact)).astype(np.int32),
                       N_REL_BUCKETS - 1)
    return np.where(n < max_exact, n, large).astype(np.int32)


def _bucket_tiles():
    table = _rel_bucket_table(2 * MOBA_BLOCK)
    a = np.arange(MOBA_BLOCK)[:, None]
    b = np.arange(MOBA_BLOCK)[None, :]
    d_own = b - a
    own = np.where(d_own >= 0, table[np.maximum(d_own, 0)], -1)
    prev = table[MOBA_BLOCK + b - a]
    return np.stack([own, prev]).astype(np.int32)


def _moba_kernel(rb_ref, qt_ref, k_ref, vt_ref, km_ref, bucket_ref, o_ref,
                 bias_ref, sel_ref, m_ref, l_ref, acc_ref, *, group):
    g = pl.program_id(0)
    i = pl.program_id(1)
    bs = MOBA_BLOCK
    dh = ATT_HEAD_DIM
    nb = km_ref.shape[0]
    nq = group * bs
    scale = dh ** -0.5 * LOG2_E

    @pl.when(i == 0)
    def _():
        for hh in range(group):
            for t in range(2):
                bk = bucket_ref[t]
                bias = jnp.full(bk.shape, MASK_VALUE, F32)
                for b in range(N_REL_BUCKETS):
                    bias = jnp.where(bk == b, rb_ref[b, g * group + hh] * LOG2_E, bias)
                bias_ref[t, :, hh * bs:(hh + 1) * bs] = bias

    qt = jnp.concatenate([qt_ref[hh * dh:(hh + 1) * dh, :] for hh in range(group)], axis=1)
    qt = (qt * scale).astype(BF16)

    blk = lax.broadcasted_iota(jnp.int32, (nb, nq), 0)
    gate = jnp.where(blk < i, _dot(km_ref[...].astype(BF16), qt), -jnp.inf)
    sel = jnp.zeros((nb, nq), F32)
    for r in range(MOBA_TOPK):
        mx = jnp.max(gate, axis=0, keepdims=True)
        first = jnp.min(jnp.where(gate == mx, blk, nb), axis=0, keepdims=True)
        pick = blk == first
        sel = jnp.where(jnp.logical_and(pick, r < i), 1.0, sel)
        gate = jnp.where(pick, -jnp.inf, gate)
    sel_ref[...] = sel

    def keys(j, nblk=1):
        return k_ref[pl.ds(pl.multiple_of(j * bs, bs), nblk * bs), :]

    def values_t(j, nblk=1):
        return vt_ref[:, pl.ds(pl.multiple_of(j * bs, bs), nblk * bs)]

    s = _dot(keys(i), qt) + bias_ref[0]
    m0 = jnp.max(s, axis=0, keepdims=True)
    p = jnp.exp2(s - m0)
    m_ref[...] = m0
    l_ref[...] = jnp.sum(p, axis=0, keepdims=True)
    acc_ref[...] = _dot(values_t(i), p.astype(BF16))

    def update(j, nblk, s, bias_row):
        parts = [s[b * bs:(b + 1) * bs] + jnp.where(sel_ref[pl.ds(j + b, 1), :] > 0.0, bias_row, MASK_VALUE)
                 for b in range(nblk)]
        m_old = m_ref[...]
        m_new = m_old
        for part in parts:
            m_new = jnp.maximum(m_new, jnp.max(part, axis=0, keepdims=True))
        alpha = jnp.exp2(m_old - m_new)
        ps = [jnp.exp2(part - m_new) for part in parts]
        l_new = alpha * l_ref[...]
        for p in ps:
            l_new = l_new + jnp.sum(p, axis=0, keepdims=True)
        m_ref[...] = m_new
        l_ref[...] = l_new
        pb16 = jnp.concatenate([p.astype(BF16) for p in ps], axis=0) if nblk > 1 else ps[0].astype(BF16)
        acc_ref[...] = alpha * acc_ref[...] + _dot(values_t(j, nblk), pb16)

    @pl.when(i >= 1)
    def _():
        update(i - 1, 1, _dot(keys(i - 1), qt) + bias_ref[1], 0.0)

    far_bias = jnp.concatenate(
        [jnp.full((1, bs), rb_ref[N_REL_BUCKETS - 1, g * group + hh] * LOG2_E, F32) for hh in range(group)],
        axis=1)
    n_far = jnp.maximum(i - 1, 0)

    def far_body(jj, carry):
        update(2 * jj, 2, _dot(keys(2 * jj, 2), qt), far_bias)
        return carry

    lax.fori_loop(0, n_far // 2, far_body, 0)

    @pl.when(n_far % 2 == 1)
    def _():
        update(n_far - 1, 1, _dot(keys(n_far - 1), qt), far_bias)

    out = acc_ref[...] / l_ref[...]
    for hh in range(group):
        o_ref[:, hh * dh:(hh + 1) * dh] = out[:, hh * bs:(hh + 1) * bs].T


def moba_attention(qt, k, vt, k_means, rel_bias, *, n_heads, n_kv_heads):
    dh = ATT_HEAD_DIM
    t = k.shape[0]
    bs = MOBA_BLOCK
    nb = t // bs
    group = n_heads // n_kv_heads
    nq = group * bs
    assert REL_MAX_DISTANCE <= bs
    buckets = jnp.asarray(_bucket_tiles())
    return pl.pallas_call(
        functools.partial(_moba_kernel, group=group),
        grid=(n_kv_heads, nb),
        in_specs=[
            pl.BlockSpec(memory_space=pltpu.SMEM),
            pl.BlockSpec((group * dh, bs), lambda g, i: (g, i)),
            pl.BlockSpec((t, dh), lambda g, i: (0, g)),
            pl.BlockSpec((dh, t), lambda g, i: (g, 0)),
            pl.BlockSpec((nb, dh), lambda g, i: (0, g)),
            pl.BlockSpec((2, bs, bs), lambda g, i: (0, 0, 0)),
        ],
        out_specs=pl.BlockSpec((bs, group * dh), lambda g, i: (i, g)),
        out_shape=jax.ShapeDtypeStruct((t, n_heads * dh), F32),
        scratch_shapes=[
            pltpu.VMEM((2, bs, nq), F32),
            pltpu.VMEM((nb, nq), F32),
            pltpu.VMEM((1, nq), F32),
            pltpu.VMEM((1, nq), F32),
            pltpu.VMEM((dh, nq), F32),
        ],
        compiler_params=_params("parallel", "arbitrary"),
        name="moba_attention",
    )(rel_bias, qt, k, vt, k_means, buckets)


def kernel(x, ffn1_norm, ffn1_w_in, ffn1_w_out, mix_norm, ffn2_norm, ffn2_w_in, ffn2_w_out,
           gdn_w_in, gdn_conv_w, gdn_a_log, gdn_dt_bias, gdn_out_norm, gdn_w_out,
           kv_norm, w_kv, moba_w_q, moba_w_o, rel_bias, final_norm):
    batch, seq, d_model = x.shape
    depth = ffn1_norm.shape[0]
    n_a = gdn_w_in.shape[0]
    n_v_heads = gdn_a_log.shape[1]
    value_dim = gdn_w_out.shape[1]
    conv_dim = gdn_conv_w.shape[2]
    key_dim = (conv_dim - value_dim) // 2
    rep = value_dim // key_dim
    n_heads = moba_w_q.shape[2] // ATT_HEAD_DIM
    kv_dim = w_kv.shape[1] // 2
    n_kv_heads = kv_dim // ATT_HEAD_DIM
    bf = lambda a: a.astype(BF16)

    outs = []
    for bi in range(batch):
        h = x[bi]
        k_nat = vt = k_means = None
        for layer in range(depth):
            if layer == n_a:
                k_f32 = norm_matmul(h, kv_norm, bf(w_kv[:, :kv_dim]), tm=512, tn=kv_dim, name="kv_k")
                vt = norm_matmul_t(h, kv_norm, bf(w_kv[:, kv_dim:].T), tm=512, tn=kv_dim, name="kv_vt",
                                   out_dtype=BF16)
                k_means, k_nat = moba_kmeans(k_f32)
            h = ffn(h, ffn1_norm[layer], bf(ffn1_w_in[layer]), bf(ffn1_w_out[layer]))
            if layer < n_a:
                w_in = gdn_w_in[layer]
                main = conv_dim + value_dim
                proj = norm_matmul(h, mix_norm[layer], bf(w_in[:, :main]), tm=512, tn=1024, name="gdn_in")
                ba = norm_matmul(h, mix_norm[layer], bf(w_in[:, main:]), tm=512, tn=2 * n_v_heads,
                                 name="gdn_in_ba")
                beta, g = gdn_gates(ba, gdn_a_log[layer], gdn_dt_bias[layer])
                qkv = gdn_conv(proj, gdn_conv_w[layer], conv_dim=conv_dim, key_dim=key_dim)
                g_rows = g.T.reshape(n_v_heads, 1, seq)
                wq, l2, u, glw = gdn_prep(qkv, g, beta, g_rows, n_heads=n_v_heads, rep=rep)
                o = gdn_scan(wq, l2, u, glw)
                h = gdn_out(o, proj, conv_dim // value_dim, gdn_out_norm[layer], bf(gdn_w_out[layer]), h)
            else:
                j = layer - n_a
                qt = norm_matmul_t(h, mix_norm[layer], bf(moba_w_q[j].T), tm=512, tn=512, name="moba_qt")
                att = moba_attention(qt, k_nat, vt, k_means, rel_bias,
                                     n_heads=n_heads, n_kv_heads=n_kv_heads)
                h = matmul_res(att, bf(moba_w_o[j]), h, tm=512, tn=512, name="moba_out")
            last = layer == depth - 1
            h = ffn(h, ffn2_norm[layer], bf(ffn2_w_in[layer]), bf(ffn2_w_out[layer]),
                    final_norm if last else None)
        outs.append(h)
    return jnp.stack(outs)
```

```python
import functools
import math

import numpy as np
import jax
import jax.numpy as jnp
from jax import lax
from jax.experimental import pallas as pl
from jax.experimental.pallas import tpu as pltpu

F32 = jnp.float32
BF16 = jnp.bfloat16

RMS_EPS = 1e-6
GDN_HEAD_DIM = 128
GDN_CHUNK = 64
GDN_CONV_WIDTH = 4
ATT_HEAD_DIM = 128
MOBA_BLOCK = 256
MOBA_TOPK = 3
N_REL_BUCKETS = 32
REL_MAX_DISTANCE = 128
MASK_VALUE = -1e30
LOG2_E = 1.4426950408889634

VMEM_LIMIT_BYTES = 52 * 1024 * 1024


def _params(*sem):
    return pltpu.CompilerParams(dimension_semantics=sem, vmem_limit_bytes=VMEM_LIMIT_BYTES)


def _rms_rows(x, g):
    ms = jnp.mean(x * x, axis=-1, keepdims=True)
    return x * lax.rsqrt(ms + RMS_EPS) * g


def _silu(x):
    return x * jax.nn.sigmoid(x)


def _dot(a, b):
    return jnp.dot(a, b, preferred_element_type=F32)


def _dot_nt(a, b):
    return lax.dot_general(a, b, (((1,), (1,)), ((), ())), preferred_element_type=F32)


def _norm_matmul_kernel(x_ref, g_ref, w_ref, o_ref, xn_ref):
    @pl.when(pl.program_id(1) == 0)
    def _():
        xn_ref[...] = _rms_rows(x_ref[...], g_ref[...]).astype(BF16)

    o_ref[...] = _dot(xn_ref[...], w_ref[...].astype(BF16)).astype(o_ref.dtype)


def _row_block(tm, d):
    return pl.BlockSpec((tm, d), lambda i, j: (i, 0), pipeline_mode=pl.Buffered(1))


def norm_matmul(x, g, w, *, tm, tn, name, layer=None, col_block0=0, n=None):
    t, d = x.shape
    n = w.shape[-1] if n is None else n
    if layer is None:
        w_spec = pl.BlockSpec((d, tn), lambda i, j: (0, col_block0 + j))
    else:
        w_spec = pl.BlockSpec((None, d, tn), lambda i, j: (layer, 0, col_block0 + j))
    return pl.pallas_call(
        _norm_matmul_kernel,
        grid=(t // tm, n // tn),
        in_specs=[_row_block(tm, d), pl.BlockSpec((1, d), lambda i, j: (0, 0)), w_spec],
        out_specs=pl.BlockSpec((tm, tn), lambda i, j: (i, j)),
        out_shape=jax.ShapeDtypeStruct((t, n), F32),
        scratch_shapes=[pltpu.VMEM((tm, d), BF16)],
        compiler_params=_params("parallel", "arbitrary"),
        name=name,
    )(x, g.reshape(1, d), w)


def _norm_matmul_t_kernel(x_ref, g_ref, wt_ref, o_ref, xn_ref):
    @pl.when(pl.program_id(1) == 0)
    def _():
        xn_ref[...] = _rms_rows(x_ref[...], g_ref[...]).astype(BF16)

    o_ref[...] = _dot_nt(wt_ref[...], xn_ref[...]).astype(o_ref.dtype)


def norm_matmul_t(x, g, wt, *, tm, tn, name, out_dtype=F32):
    t, d = x.shape
    n = wt.shape[0]
    return pl.pallas_call(
        _norm_matmul_t_kernel,
        grid=(t // tm, n // tn),
        in_specs=[
            pl.BlockSpec((tm, d), lambda i, j: (i, 0)),
            pl.BlockSpec((1, d), lambda i, j: (0, 0)),
            pl.BlockSpec((tn, d), lambda i, j: (j, 0)),
        ],
        out_specs=pl.BlockSpec((tn, tm), lambda i, j: (j, i)),
        out_shape=jax.ShapeDtypeStruct((n, t), out_dtype),
        scratch_shapes=[pltpu.VMEM((tm, d), BF16)],
        compiler_params=_params("parallel", "arbitrary"),
        name=name,
    )(x, g.reshape(1, d), wt)


def _ffn_kernel(h_ref, g_ref, wg_ref, wu_ref, wo_ref, fg_ref, o_ref, xn_ref, *, final_norm):
    j = pl.program_id(1)
    tm = h_ref.shape[0]
    strip = min(tm, 256)

    @pl.when(j == 0)
    def _():
        for r0 in range(0, tm, strip):
            r = slice(r0, r0 + strip)
            xn_ref[r, :] = _rms_rows(h_ref[r, :], g_ref[...]).astype(BF16)
            o_ref[r, :] = jnp.zeros((strip, o_ref.shape[1]), F32)

    xn = xn_ref[...]
    gate = _dot(xn, wg_ref[...].astype(BF16))
    up = _dot(xn, wu_ref[...].astype(BF16))
    act = (_silu(gate) * up).astype(BF16)
    o_ref[...] += _dot(act, wo_ref[...].astype(BF16))

    @pl.when(j == pl.num_programs(1) - 1)
    def _():
        for r0 in range(0, tm, strip):
            r = slice(r0, r0 + strip)
            y = h_ref[r, :] + 0.5 * o_ref[r, :]
            if final_norm:
                y = _rms_rows(y, fg_ref[...])
            o_ref[r, :] = y


def ffn(h, g, w_in, w_out, layer, final_g=None, *, tm=1024, tf=256):
    t, d = h.shape
    f = w_out.shape[1]
    nf = f // tf
    final_norm = final_g is not None
    fg = (final_g if final_norm else g).reshape(1, d)
    return pl.pallas_call(
        functools.partial(_ffn_kernel, final_norm=final_norm),
        grid=(t // tm, nf),
        in_specs=[
            _row_block(tm, d),
            pl.BlockSpec((1, d), lambda i, j: (0, 0)),
            pl.BlockSpec((None, d, tf), lambda i, j: (layer, 0, j)),
            pl.BlockSpec((None, d, tf), lambda i, j: (layer, 0, j + nf)),
            pl.BlockSpec((None, tf, d), lambda i, j: (layer, j, 0)),
            pl.BlockSpec((1, d), lambda i, j: (0, 0)),
        ],
        out_specs=pl.BlockSpec((tm, d), lambda i, j: (i, 0)),
        out_shape=jax.ShapeDtypeStruct((t, d), F32),
        scratch_shapes=[pltpu.VMEM((tm, d), BF16)],
        compiler_params=_params("parallel", "arbitrary"),
        name="ffn",
    )(h, g.reshape(1, d), w_in, w_in, w_out, fg)


def _matmul_res_kernel(a_ref, w_ref, r_ref, o_ref):
    o_ref[...] = r_ref[...] + _dot(a_ref[...], w_ref[...])


def matmul_res(a, w, r, *, tm, tn, name):
    t, k = a.shape
    n = w.shape[1]
    return pl.pallas_call(
        _matmul_res_kernel,
        grid=(t // tm, n // tn),
        in_specs=[
            pl.BlockSpec((tm, k), lambda i, j: (i, 0)),
            pl.BlockSpec((k, tn), lambda i, j: (0, j)),
            pl.BlockSpec((tm, tn), lambda i, j: (i, j)),
        ],
        out_specs=pl.BlockSpec((tm, tn), lambda i, j: (i, j)),
        out_shape=jax.ShapeDtypeStruct((t, n), F32),
        compiler_params=_params("parallel", "arbitrary"),
        name=name,
    )(a, w, r)


def _conv_kernel(prev_ref, cur_ref, w_ref, o_ref, xs_ref, *, n_norm_blocks, strip):
    i = pl.program_id(0)
    j = pl.program_id(1)
    tt, tc = cur_ref.shape
    halo = prev_ref.shape[0]
    xs_ref[0:halo, :] = jnp.where(i == 0, 0.0, prev_ref[...])
    xs_ref[halo:, :] = cur_ref[...]
    w = w_ref[...]

    def conv_strip(r0):
        y = w[GDN_CONV_WIDTH - 1:GDN_CONV_WIDTH] * xs_ref[halo + r0:halo + r0 + strip, :]
        for s in range(1, GDN_CONV_WIDTH):
            tap = GDN_CONV_WIDTH - 1 - s
            y = y + w[tap:tap + 1] * xs_ref[halo + r0 - s:halo + r0 - s + strip, :]
        return _silu(y)

    @pl.when(j < n_norm_blocks)
    def _():
        for r0 in range(0, tt, strip):
            y = conv_strip(r0)
            for hh in range(tc // GDN_HEAD_DIM):
                seg = y[:, hh * GDN_HEAD_DIM:(hh + 1) * GDN_HEAD_DIM]
                ss = jnp.sum(seg * seg, axis=-1, keepdims=True)
                o_ref[r0:r0 + strip, hh * GDN_HEAD_DIM:(hh + 1) * GDN_HEAD_DIM] = seg * lax.rsqrt(ss + RMS_EPS)

    @pl.when(j >= n_norm_blocks)
    def _():
        for r0 in range(0, tt, strip):
            o_ref[r0:r0 + strip, :] = conv_strip(r0)


def gdn_conv(proj, conv_w, *, conv_dim, key_dim, tt=512, tc=512, strip=32):
    t = proj.shape[0]
    halo = 8
    assert halo >= GDN_CONV_WIDTH - 1
    return pl.pallas_call(
        functools.partial(_conv_kernel, n_norm_blocks=2 * key_dim // tc, strip=strip),
        grid=(t // tt, conv_dim // tc),
        in_specs=[
            pl.BlockSpec((halo, tc), lambda i, j: (jnp.maximum(i * (tt // halo) - 1, 0), j)),
            pl.BlockSpec((tt, tc), lambda i, j: (i, j)),
            pl.BlockSpec((GDN_CONV_WIDTH, tc), lambda i, j: (0, j)),
        ],
        out_specs=pl.BlockSpec((tt, tc), lambda i, j: (i, j)),
        out_shape=jax.ShapeDtypeStruct((t, conv_dim), F32),
        scratch_shapes=[pltpu.VMEM((tt + halo, tc), F32)],
        compiler_params=_params("parallel", "parallel"),
        name="gdn_conv",
    )(proj, proj, conv_w)


def _gates_kernel(ba_ref, alog_ref, dtb_ref, beta_ref, g_ref):
    nh = beta_ref.shape[1]
    ba = ba_ref[...]
    beta_ref[...] = jax.nn.sigmoid(ba[:, :nh])
    z = ba[:, nh:] + dtb_ref[...]
    softplus = jnp.maximum(z, 0.0) + jnp.log1p(jnp.exp(-jnp.abs(z)))
    g_ref[...] = -jnp.exp(alog_ref[...]) * softplus


def gdn_gates(ba, a_log, dt_bias, *, tt=1024):
    t, two_h = ba.shape
    nh = two_h // 2
    return pl.pallas_call(
        _gates_kernel,
        grid=(t // tt,),
        in_specs=[
            pl.BlockSpec((tt, two_h), lambda i: (i, 0)),
            pl.BlockSpec((1, nh), lambda i: (0, 0)),
            pl.BlockSpec((1, nh), lambda i: (0, 0)),
        ],
        out_specs=[pl.BlockSpec((tt, nh), lambda i: (i, 0)), pl.BlockSpec((tt, nh), lambda i: (i, 0))],
        out_shape=[jax.ShapeDtypeStruct((t, nh), F32), jax.ShapeDtypeStruct((t, nh), F32)],
        compiler_params=_params("parallel"),
        name="gdn_gates",
    )(ba, a_log.reshape(1, nh), dt_bias.reshape(1, nh))


def _prep_kernel(q_ref, k_ref, v_ref, gsel_ref, bsel_ref, grow_ref, wq_ref, l2_ref, u_ref, glw_ref):
    kh = pl.program_id(0)
    tp, nh = gsel_ref.shape
    c = GDN_CHUNK
    dh = GDN_HEAD_DIM
    scale = dh ** -0.5
    lane_h = lax.broadcasted_iota(jnp.int32, (tp, nh), 1)

    def column(ref, head):
        return jnp.sum(jnp.where(lane_h == head, ref[...], 0.0), axis=1, keepdims=True)

    gcol = [column(gsel_ref, 2 * kh + hh) for hh in range(2)]
    bcol = [column(bsel_ref, 2 * kh + hh) for hh in range(2)]
    grow = [grow_ref[hh] for hh in range(2)]

    ii = lax.broadcasted_iota(jnp.int32, (c, 2 * c), 0)
    ll = lax.broadcasted_iota(jnp.int32, (c, 2 * c), 1)
    jj = jnp.bitwise_and(ll, c - 1)
    head_a = ll < c
    incl = ii >= jj
    strict = ii > jj
    wi = lax.broadcasted_iota(jnp.int32, (c, 4 * c), 0)
    wl = lax.broadcasted_iota(jnp.int32, (c, 4 * c), 1)
    eye_w = jnp.where(wi == jnp.bitwise_and(wl, c - 1), 1.0, 0.0).astype(F32)
    br = lax.broadcasted_iota(jnp.int32, (4 * c, 4 * c), 0)
    bc = lax.broadcasted_iota(jnp.int32, (4 * c, 4 * c), 1)
    same_block = (br // c) == (bc // c)
    strict_block = jnp.logical_and(same_block, br != bc)

    def block_diag(packed):
        return jnp.concatenate([packed] * 4, axis=0)

    def packed_matmul(a, b):
        bd = jnp.where(same_block, block_diag(b), 0.0).astype(BF16)
        return _dot(a.astype(BF16), bd)

    n_half = tp // (2 * c)
    lmats = [[] for _ in range(n_half)]
    rhs_blocks = [[] for _ in range(n_half)]
    for half in range(n_half):
        for y in range(2):
            n = 2 * half + y
            r = slice(n * c, (n + 1) * c)
            q = q_ref[r, :] * scale
            k = k_ref[r, :]
            g_pair = jnp.where(head_a, gcol[0][r], gcol[1][r])
            b_pair = jnp.where(head_a, bcol[0][r], bcol[1][r])
            grow_pair = jnp.concatenate([grow[0][:, r], grow[1][:, r]], axis=1)
            tri = jnp.where(incl, grow_pair, 0.0)
            gc = [jnp.sum(jnp.where(head_a, tri, 0.0), axis=1, keepdims=True),
                  jnp.sum(jnp.where(head_a, 0.0, tri), axis=1, keepdims=True)]
            gc_col = jnp.where(head_a, gc[0], gc[1])
            gc_row = jnp.sum(jnp.where(ii <= jj, g_pair, 0.0), axis=0, keepdims=True)
            decay = jnp.where(incl, jnp.exp(jnp.where(incl, gc_col - gc_row, 0.0)), 0.0)
            kb16 = k.astype(BF16)
            gram = _dot_nt(jnp.concatenate([q.astype(BF16), kb16], axis=0),
                           jnp.concatenate([kb16, kb16], axis=0))
            l2_ref[n, 0:c, :] = jnp.where(incl, gram[:c] * decay, 0.0).astype(BF16)
            lmats[half].append(jnp.where(strict, gram[c:] * b_pair * decay, 0.0))
            kups = []
            for hh in range(2):
                g_last = jnp.sum(gcol[hh][r], axis=0, keepdims=True)
                eg = jnp.exp(gc[hh])
                beta = bcol[hh][r]
                kups.append(k * jnp.exp(g_last - gc[hh]))
                wq_ref[n, c:2 * c, hh * dh:(hh + 1) * dh] = (q * eg).astype(BF16)
                glw_ref[n, :, hh * dh:(hh + 1) * dh] = jnp.broadcast_to(jnp.exp(g_last), (8, dh))
                rhs_blocks[half].append(
                    jnp.concatenate([v_ref[r, hh * dh:(hh + 1) * dh] * beta, k * (beta * eg)], axis=1))
            l2_ref[n, c:c + dh, :] = jnp.concatenate(kups, axis=0).T.astype(BF16)

    ps = [-jnp.concatenate(lm, axis=1) for lm in lmats]
    tinvs = [eye_w + p for p in ps]
    m = 1
    while 2 * m < c:
        ps = [packed_matmul(p, p) for p in ps]
        tinvs = [t + packed_matmul(t, p) for t, p in zip(tinvs, ps)]
        m *= 2
    for half in range(n_half):
        rhs_all = jnp.concatenate(rhs_blocks[half], axis=0)
        t_off = jnp.where(strict_block, block_diag(tinvs[half]), 0.0).astype(BF16)
        uw = rhs_all + _dot(t_off, rhs_all.astype(BF16))
        for y in range(2):
            n = 2 * half + y
            for hh in range(2):
                blk = uw[(2 * y + hh) * c:(2 * y + hh + 1) * c]
                u_ref[n * c:(n + 1) * c, hh * dh:(hh + 1) * dh] = blk[:, :dh]
                wq_ref[n, 0:c, hh * dh:(hh + 1) * dh] = blk[:, dh:].astype(BF16)


def gdn_prep(qkv, g, beta, g_rows, *, n_heads, rep, tp=512):
    assert rep == 2
    t = qkv.shape[0]
    dh = GDN_HEAD_DIM
    c = GDN_CHUNK
    nk = n_heads // rep
    nc = tp // c
    return pl.pallas_call(
        _prep_kernel,
        grid=(nk, t // tp),
        in_specs=[
            pl.BlockSpec((tp, dh), lambda h, i: (i, h)),
            pl.BlockSpec((tp, dh), lambda h, i: (i, nk + h)),
            pl.BlockSpec((tp, 2 * dh), lambda h, i: (i, nk + h)),
            pl.BlockSpec((tp, n_heads), lambda h, i: (i, 0)),
            pl.BlockSpec((tp, n_heads), lambda h, i: (i, 0)),
            pl.BlockSpec((2, 1, tp), lambda h, i: (h, 0, i)),
        ],
        out_specs=[
            pl.BlockSpec((None, nc, 2 * c, 2 * dh), lambda h, i: (h, i, 0, 0)),
            pl.BlockSpec((None, nc, c + dh, 2 * c), lambda h, i: (h, i, 0, 0)),
            pl.BlockSpec((None, tp, 2 * dh), lambda h, i: (h, i, 0)),
            pl.BlockSpec((None, nc, 8, 2 * dh), lambda h, i: (h, i, 0, 0)),
        ],
        out_shape=[
            jax.ShapeDtypeStruct((nk, t // c, 2 * c, 2 * dh), BF16),
            jax.ShapeDtypeStruct((nk, t // c, c + dh, 2 * c), BF16),
            jax.ShapeDtypeStruct((nk, t, 2 * dh), F32),
            jax.ShapeDtypeStruct((nk, t // c, 8, 2 * dh), F32),
        ],
        compiler_params=_params("parallel", "parallel"),
        name="gdn_prep",
    )(qkv, qkv, qkv, g, beta, g_rows)


def _scan_kernel(wq_ref, l2_ref, u_ref, glw_ref, z_ref, gn_ref, o_ref, s_ref):
    pb, nc = wq_ref.shape[0], wq_ref.shape[1]
    gn = gn_ref[...]
    c = GDN_CHUNK
    dh = GDN_HEAD_DIM

    @pl.when(pl.program_id(1) == 0)
    def _():
        s_ref[...] = jnp.zeros_like(s_ref)

    def pair_diag(x):
        z = jnp.zeros((x.shape[0], dh), x.dtype)
        return jnp.concatenate([jnp.concatenate([x[:, :dh], z], axis=1),
                                jnp.concatenate([z, x[:, dh:]], axis=1)], axis=0)

    states = [s_ref[p] for p in range(pb)]
    for n in range(nc):
        r = slice(n * c, (n + 1) * c)
        for p in range(pb):
            ws = _dot(wq_ref[p, n], pair_diag(states[p].astype(BF16)))
            v_new = (u_ref[p, r, :] - ws[:c]).astype(BF16)
            upd = _dot(l2_ref[p, n], pair_diag(v_new))
            o = ws[c:] + upd[:c]
            states[p] = states[p] * glw_ref[p, n, 0:1, :] + upd[c:]
            for hh in range(2):
                cols = slice((2 * p + hh) * dh, (2 * p + hh + 1) * dh)
                gated = _rms_rows(o[:, hh * dh:(hh + 1) * dh], gn) * _silu(z_ref[r, cols])
                o_ref[r, cols] = gated.astype(BF16)
    for p in range(pb):
        s_ref[p] = states[p]


def gdn_scan(wq, l2, u, glw, proj, z_col0, out_norm, *, pb=8, tp=256):
    nk, t, two_dh = u.shape
    c = GDN_CHUNK
    nc = tp // c
    width = pb * two_dh
    blk4 = lambda a: pl.BlockSpec((pb, nc) + a.shape[2:], lambda h, i: (h, i, 0, 0))
    return pl.pallas_call(
        _scan_kernel,
        grid=(nk // pb, t // tp),
        in_specs=[blk4(wq), blk4(l2), pl.BlockSpec((pb, tp, two_dh), lambda h, i: (h, i, 0)), blk4(glw),
                  pl.BlockSpec((tp, width), lambda h, i: (i, z_col0 // width + h)),
                  pl.BlockSpec((1, GDN_HEAD_DIM), lambda h, i: (0, 0))],
        out_specs=pl.BlockSpec((tp, width), lambda h, i: (i, h)),
        out_shape=jax.ShapeDtypeStruct((t, nk * two_dh), BF16),
        scratch_shapes=[pltpu.VMEM((pb, two_dh // 2, two_dh), F32)],
        compiler_params=_params("parallel", "arbitrary"),
        name="gdn_scan",
    )(wq, l2, u, glw, proj, out_norm.reshape(1, GDN_HEAD_DIM))


def _kmeans_kernel(k_ref, mean_ref, kb_ref):
    k = k_ref[...]
    mean_ref[...] = jnp.mean(k, axis=0, keepdims=True)
    kb_ref[...] = k.astype(BF16)


def moba_kmeans(k):
    t, kd = k.shape
    nb = t // MOBA_BLOCK
    means, kb = pl.pallas_call(
        _kmeans_kernel,
        grid=(nb,),
        in_specs=[pl.BlockSpec((MOBA_BLOCK, kd), lambda i: (i, 0))],
        out_specs=[pl.BlockSpec((None, 1, kd), lambda i: (i, 0, 0)),
                   pl.BlockSpec((MOBA_BLOCK, kd), lambda i: (i, 0))],
        out_shape=[jax.ShapeDtypeStruct((nb, 1, kd), F32), jax.ShapeDtypeStruct((t, kd), BF16)],
        compiler_params=_params("parallel"),
        name="moba_kmeans",
    )(k)
    return means.reshape(nb, kd), kb


def _rel_bucket_table(n_dist):
    n = np.arange(n_dist)
    max_exact = N_REL_BUCKETS // 2
    ratio = np.log(np.maximum(n, max_exact).astype(np.float32) / np.float32(max_exact)) \
        / np.float32(math.log(REL_MAX_DISTANCE / max_exact))
    large = np.minimum(max_exact + (ratio.astype(np.float32) * (N_REL_BUCKETS - max_exact)).astype(np.int32),
                       N_REL_BUCKETS - 1)
    return np.where(n < max_exact, n, large).astype(np.int32)


def _bucket_tiles():
    table = _rel_bucket_table(2 * MOBA_BLOCK)
    a = np.arange(MOBA_BLOCK)[:, None]
    b = np.arange(MOBA_BLOCK)[None, :]
    d_own = b - a
    own = np.where(d_own >= 0, table[np.maximum(d_own, 0)], -1)
    prev = table[MOBA_BLOCK + b - a]
    return np.stack([own, prev]).astype(np.int32)


def _moba_kernel(rb_ref, qt_ref, k_ref, vt_ref, km_ref, bucket_ref, o_ref,
                 bias_ref, sel_ref, m_ref, l_ref, acc_ref, *, group):
    g = pl.program_id(0)
    i = pl.program_id(1)
    bs = MOBA_BLOCK
    dh = ATT_HEAD_DIM
    nb = km_ref.shape[0]
    nq = group * bs
    scale = dh ** -0.5 * LOG2_E

    @pl.when(i == 0)
    def _():
        for hh in range(group):
            for t in range(2):
                bk = bucket_ref[t]
                bias = jnp.full(bk.shape, MASK_VALUE, F32)
                for b in range(N_REL_BUCKETS):
                    bias = jnp.where(bk == b, rb_ref[b, g * group + hh] * LOG2_E, bias)
                bias_ref[t, :, hh * bs:(hh + 1) * bs] = bias

    qt = jnp.concatenate([qt_ref[hh * dh:(hh + 1) * dh, :] for hh in range(group)], axis=1)
    qt = (qt * scale).astype(BF16)

    blk = lax.broadcasted_iota(jnp.int32, (nb, nq), 0)
    gate = jnp.where(blk < i, _dot(km_ref[...].astype(BF16), qt), -jnp.inf)
    sel = jnp.zeros((nb, nq), F32)
    for r in range(MOBA_TOPK):
        mx = jnp.max(gate, axis=0, keepdims=True)
        first = jnp.min(jnp.where(gate == mx, blk, nb), axis=0, keepdims=True)
        pick = blk == first
        sel = jnp.where(jnp.logical_and(pick, r < i), 1.0, sel)
        gate = jnp.where(pick, -jnp.inf, gate)
    sel_ref[...] = sel

    def keys(j, nblk=1):
        return k_ref[pl.ds(pl.multiple_of(j * bs, bs), nblk * bs), :]

    def values_t(j, nblk=1):
        return vt_ref[:, pl.ds(pl.multiple_of(j * bs, bs), nblk * bs)]

    s = _dot(keys(i), qt) + bias_ref[0]
    m0 = jnp.max(s, axis=0, keepdims=True)
    p = jnp.exp2(s - m0)
    m_ref[...] = m0
    l_ref[...] = jnp.sum(p, axis=0, keepdims=True)
    acc_ref[...] = _dot(values_t(i), p.astype(BF16))

    def update(j, nblk, tile_bias, far):
        k_blk = keys(j, nblk)
        vt_blk = values_t(j, nblk)
        m_all, l_all, acc_all = m_ref[...], l_ref[...], acc_ref[...]
        sel_rows = [sel_ref[pl.ds(j + b, 1), :] for b in range(nblk)]
        m_out, l_out, acc_out = [], [], []
        scores = [_dot(k_blk, qt[:, hh * bs:(hh + 1) * bs]) for hh in range(group)]
        for hh in range(group):
            cols = slice(hh * bs, (hh + 1) * bs)
            s = scores[hh]
            if tile_bias is not None:
                s = s + bias_ref[tile_bias, :, cols]
            row_bias = rb_ref[N_REL_BUCKETS - 1, g * group + hh] * LOG2_E if far else 0.0
            parts = [s[b * bs:(b + 1) * bs] + jnp.where(sel_rows[b][:, cols] > 0.0, row_bias, MASK_VALUE)
                     for b in range(nblk)]
            m_old = m_all[:, cols]
            m_new = m_old
            for part in parts:
                m_new = jnp.maximum(m_new, jnp.max(part, axis=0, keepdims=True))
            alpha = jnp.exp2(m_old - m_new)
            ps = [jnp.exp2(part - m_new) for part in parts]
            l_new = alpha * l_all[:, cols]
            for p in ps:
                l_new = l_new + jnp.sum(p, axis=0, keepdims=True)
            pb16 = jnp.concatenate([p.astype(BF16) for p in ps], axis=0) if nblk > 1 else ps[0].astype(BF16)
            m_out.append(m_new)
            l_out.append(l_new)
            acc_out.append(alpha * acc_all[:, cols] + _dot(vt_blk, pb16))
        m_ref[...] = jnp.concatenate(m_out, axis=1)
        l_ref[...] = jnp.concatenate(l_out, axis=1)
        acc_ref[...] = jnp.concatenate(acc_out, axis=1)

    @pl.when(i >= 1)
    def _():
        update(i - 1, 1, 1, False)

    n_far = jnp.maximum(i - 1, 0)

    def far_body(jj, carry):
        update(2 * jj, 2, None, True)
        return carry

    lax.fori_loop(0, n_far // 2, far_body, 0)

    @pl.when(n_far % 2 == 1)
    def _():
        update(n_far - 1, 1, None, True)

    out = acc_ref[...] / l_ref[...]
    for hh in range(group):
        o_ref[:, hh * dh:(hh + 1) * dh] = out[:, hh * bs:(hh + 1) * bs].T.astype(BF16)


def moba_attention(qt, k, vt, k_means, rel_bias, *, n_heads, n_kv_heads):
    dh = ATT_HEAD_DIM
    t = k.shape[0]
    bs = MOBA_BLOCK
    nb = t // bs
    group = n_heads // n_kv_heads
    nq = group * bs
    assert REL_MAX_DISTANCE <= bs
    buckets = jnp.asarray(_bucket_tiles())
    return pl.pallas_call(
        functools.partial(_moba_kernel, group=group),
        grid=(n_kv_heads, nb),
        in_specs=[
            pl.BlockSpec(memory_space=pltpu.SMEM),
            pl.BlockSpec((group * dh, bs), lambda g, i: (g, i)),
            pl.BlockSpec((t, dh), lambda g, i: (0, g)),
            pl.BlockSpec((dh, t), lambda g, i: (g, 0)),
            pl.BlockSpec((nb, dh), lambda g, i: (0, g)),
            pl.BlockSpec((2, bs, bs), lambda g, i: (0, 0, 0)),
        ],
        out_specs=pl.BlockSpec((bs, group * dh), lambda g, i: (i, g)),
        out_shape=jax.ShapeDtypeStruct((t, n_heads * dh), BF16),
        scratch_shapes=[
            pltpu.VMEM((2, bs, nq), F32),
            pltpu.VMEM((nb, nq), F32),
            pltpu.VMEM((1, nq), F32),
            pltpu.VMEM((1, nq), F32),
            pltpu.VMEM((dh, nq), F32),
        ],
        compiler_params=_params("parallel", "arbitrary"),
        name="moba_attention",
    )(rel_bias, qt, k, vt, k_means, buckets)


def kernel(x, ffn1_norm, ffn1_w_in, ffn1_w_out, mix_norm, ffn2_norm, ffn2_w_in, ffn2_w_out,
           gdn_w_in, gdn_conv_w, gdn_a_log, gdn_dt_bias, gdn_out_norm, gdn_w_out,
           kv_norm, w_kv, moba_w_q, moba_w_o, rel_bias, final_norm):
    batch, seq, d_model = x.shape
    depth = ffn1_norm.shape[0]
    n_a = gdn_w_in.shape[0]
    n_v_heads = gdn_a_log.shape[1]
    value_dim = gdn_w_out.shape[1]
    conv_dim = gdn_conv_w.shape[2]
    key_dim = (conv_dim - value_dim) // 2
    rep = value_dim // key_dim
    n_heads = moba_w_q.shape[2] // ATT_HEAD_DIM
    kv_dim = w_kv.shape[1] // 2
    n_kv_heads = kv_dim // ATT_HEAD_DIM
    bf = lambda a: a.astype(BF16)

    outs = []
    for bi in range(batch):
        h = x[bi]
        k_nat = vt = k_means = None
        for layer in range(depth):
            if layer == n_a:
                k_f32 = norm_matmul(h, kv_norm, bf(w_kv[:, :kv_dim]), tm=512, tn=kv_dim, name="kv_k")
                vt = norm_matmul_t(h, kv_norm, bf(w_kv[:, kv_dim:].T), tm=512, tn=kv_dim, name="kv_vt",
                                   out_dtype=BF16)
                k_means, k_nat = moba_kmeans(k_f32)
            h = ffn(h, ffn1_norm[layer], ffn1_w_in, ffn1_w_out, layer)
            if layer < n_a:
                main = conv_dim + value_dim
                n_ba = 2 * n_v_heads
                proj = norm_matmul(h, mix_norm[layer], gdn_w_in, layer=layer, n=main, tm=2048, tn=512,
                                   name="gdn_in")
                ba = norm_matmul(h, mix_norm[layer], bf(gdn_w_in[layer, :, main:]), tm=2048, tn=n_ba,
                                 name="gdn_in_ba")
                beta, g = gdn_gates(ba, gdn_a_log[layer], gdn_dt_bias[layer])
                qkv = gdn_conv(proj, gdn_conv_w[layer], conv_dim=conv_dim, key_dim=key_dim)
                g_rows = g.T.reshape(n_v_heads, 1, seq)
                wq, l2, u, glw = gdn_prep(qkv, g, beta, g_rows, n_heads=n_v_heads, rep=rep)
                gated = gdn_scan(wq, l2, u, glw, proj, conv_dim, gdn_out_norm[layer])
                h = matmul_res(gated, bf(gdn_w_out[layer]), h, tm=512, tn=512, name="gdn_out")
            else:
                j = layer - n_a
                qt = norm_matmul_t(h, mix_norm[layer], bf(moba_w_q[j].T), tm=512, tn=512, name="moba_qt")
                att = moba_attention(qt, k_nat, vt, k_means, rel_bias,
                                     n_heads=n_heads, n_kv_heads=n_kv_heads)
                h = matmul_res(att, bf(moba_w_o[j]), h, tm=512, tn=512, name="moba_out")
            last = layer == depth - 1
            h = ffn(h, ffn2_norm[layer], ffn2_w_in, ffn2_w_out, layer, final_norm if last else None)
        outs.append(h)
    return jnp.stack(outs)
```

```python
import functools
import math

import numpy as np
import jax
import jax.numpy as jnp
from jax import lax
from jax.experimental import pallas as pl
from jax.experimental.pallas import tpu as pltpu

F32 = jnp.float32
BF16 = jnp.bfloat16

RMS_EPS = 1e-6
GDN_HEAD_DIM = 128
GDN_CHUNK = 64
GDN_CONV_WIDTH = 4
ATT_HEAD_DIM = 128
MOBA_BLOCK = 256
MOBA_TOPK = 3
N_REL_BUCKETS = 32
REL_MAX_DISTANCE = 128
MASK_VALUE = -1e30
LOG2_E = 1.4426950408889634

VMEM_LIMIT_BYTES = 58 * 1024 * 1024


def _params(*sem):
    return pltpu.CompilerParams(dimension_semantics=sem, vmem_limit_bytes=VMEM_LIMIT_BYTES)


def _rms_rows(x, g):
    ms = jnp.mean(x * x, axis=-1, keepdims=True)
    return x * lax.rsqrt(ms + RMS_EPS) * g


def _silu(x):
    return x * jax.nn.sigmoid(x)


def _dot(a, b):
    return jnp.dot(a, b, preferred_element_type=F32)


def _dot_nt(a, b):
    return lax.dot_general(a, b, (((1,), (1,)), ((), ())), preferred_element_type=F32)


def _norm_matmul_kernel(x_ref, g_ref, w_ref, o_ref, xn_ref, *, w_rows_are_outputs):
    @pl.when(pl.program_id(1) == 0)
    def _():
        xn_ref[...] = _rms_rows(x_ref[...], g_ref[...]).astype(BF16)

    w = w_ref[...].astype(BF16)
    y = _dot_nt(xn_ref[...], w) if w_rows_are_outputs else _dot(xn_ref[...], w)
    o_ref[...] = y.astype(o_ref.dtype)


def _row_block(tm, d):
    return pl.BlockSpec((tm, d), lambda i, j: (i, 0), pipeline_mode=pl.Buffered(1))


def norm_matmul(x, g, w, *, tm, tn, name, layer=None, col_block0=0, n=None):
    t, d = x.shape
    if layer is None:
        n = w.shape[-1] if n is None else n
        w_spec = pl.BlockSpec((d, tn), lambda i, j: (0, col_block0 + j))
    else:
        n = w.shape[1] if n is None else n
        w_spec = pl.BlockSpec((None, tn, d), lambda i, j: (layer, col_block0 + j, 0))
    return pl.pallas_call(
        functools.partial(_norm_matmul_kernel, w_rows_are_outputs=layer is not None),
        grid=(t // tm, n // tn),
        in_specs=[_row_block(tm, d), pl.BlockSpec((1, d), lambda i, j: (0, 0)), w_spec],
        out_specs=pl.BlockSpec((tm, tn), lambda i, j: (i, j)),
        out_shape=jax.ShapeDtypeStruct((t, n), F32),
        scratch_shapes=[pltpu.VMEM((tm, d), BF16)],
        compiler_params=_params("parallel", "arbitrary"),
        name=name,
    )(x, g.reshape(1, d), w)


def _norm_matmul_t_kernel(x_ref, g_ref, wt_ref, o_ref, xn_ref):
    @pl.when(pl.program_id(1) == 0)
    def _():
        xn_ref[...] = _rms_rows(x_ref[...], g_ref[...]).astype(BF16)

    o_ref[...] = _dot_nt(wt_ref[...], xn_ref[...]).astype(o_ref.dtype)


def norm_matmul_t(x, g, wt, *, tm, tn, name, out_dtype=F32):
    t, d = x.shape
    n = wt.shape[0]
    return pl.pallas_call(
        _norm_matmul_t_kernel,
        grid=(t // tm, n // tn),
        in_specs=[
            pl.BlockSpec((tm, d), lambda i, j: (i, 0)),
            pl.BlockSpec((1, d), lambda i, j: (0, 0)),
            pl.BlockSpec((tn, d), lambda i, j: (j, 0)),
        ],
        out_specs=pl.BlockSpec((tn, tm), lambda i, j: (j, i)),
        out_shape=jax.ShapeDtypeStruct((n, t), out_dtype),
        scratch_shapes=[pltpu.VMEM((tm, d), BF16)],
        compiler_params=_params("parallel", "arbitrary"),
        name=name,
    )(x, g.reshape(1, d), wt)


def _ffn_kernel(h_ref, g_ref, wg_ref, wu_ref, wo_ref, fg_ref, o_ref, xn_ref, *, final_norm):
    j = pl.program_id(1)
    tm = h_ref.shape[0]
    strip = min(tm, 256)

    @pl.when(j == 0)
    def _():
        for r0 in range(0, tm, strip):
            r = slice(r0, r0 + strip)
            xn_ref[r, :] = _rms_rows(h_ref[r, :], g_ref[...]).astype(BF16)
            o_ref[r, :] = jnp.zeros((strip, o_ref.shape[1]), F32)

    xn = xn_ref[...]
    tf = wg_ref.shape[1]
    half = min(tf, 256)
    acts = []
    for c0 in range(0, tf, half):
        gate = _dot(xn, wg_ref[:, c0:c0 + half].astype(BF16))
        up = _dot(xn, wu_ref[:, c0:c0 + half].astype(BF16))
        acts.append((_silu(gate) * up).astype(BF16))
    act = acts[0] if len(acts) == 1 else jnp.concatenate(acts, axis=1)
    o_ref[...] += _dot(act, wo_ref[...].astype(BF16))

    @pl.when(j == pl.num_programs(1) - 1)
    def _():
        for r0 in range(0, tm, strip):
            r = slice(r0, r0 + strip)
            y = h_ref[r, :] + 0.5 * o_ref[r, :]
            if final_norm:
                y = _rms_rows(y, fg_ref[...])
            o_ref[r, :] = y


def ffn(h, g, w_in, w_out, layer, final_g=None, *, tm=1024, tf=512):
    t, d = h.shape
    f = w_out.shape[1]
    nf = f // tf
    final_norm = final_g is not None
    fg = (final_g if final_norm else g).reshape(1, d)
    return pl.pallas_call(
        functools.partial(_ffn_kernel, final_norm=final_norm),
        grid=(t // tm, nf),
        in_specs=[
            _row_block(tm, d),
            pl.BlockSpec((1, d), lambda i, j: (0, 0)),
            pl.BlockSpec((None, d, tf), lambda i, j: (layer, 0, j)),
            pl.BlockSpec((None, d, tf), lambda i, j: (layer, 0, j + nf)),
            pl.BlockSpec((None, tf, d), lambda i, j: (layer, j, 0)),
            pl.BlockSpec((1, d), lambda i, j: (0, 0)),
        ],
        out_specs=_row_block(tm, d),
        out_shape=jax.ShapeDtypeStruct((t, d), F32),
        scratch_shapes=[pltpu.VMEM((tm, d), BF16)],
        compiler_params=_params("parallel", "arbitrary"),
        name="ffn",
    )(h, g.reshape(1, d), w_in, w_in, w_out, fg)


def _matmul_res_kernel(a_ref, w_ref, r_ref, o_ref):
    o_ref[...] = r_ref[...] + _dot(a_ref[...], w_ref[...])


def matmul_res(a, w, r, *, tm, tn, name):
    t, k = a.shape
    n = w.shape[1]
    return pl.pallas_call(
        _matmul_res_kernel,
        grid=(t // tm, n // tn),
        in_specs=[
            pl.BlockSpec((tm, k), lambda i, j: (i, 0)),
            pl.BlockSpec((k, tn), lambda i, j: (0, j)),
            pl.BlockSpec((tm, tn), lambda i, j: (i, j)),
        ],
        out_specs=pl.BlockSpec((tm, tn), lambda i, j: (i, j)),
        out_shape=jax.ShapeDtypeStruct((t, n), F32),
        compiler_params=_params("parallel", "arbitrary"),
        name=name,
    )(a, w, r)


def _conv_kernel(prev_ref, cur_ref, w_ref, o_ref, xs_ref, *, n_norm_blocks, strip):
    i = pl.program_id(0)
    j = pl.program_id(1)
    tt, tc = cur_ref.shape
    halo = prev_ref.shape[0]
    xs_ref[0:halo, :] = jnp.where(i == 0, 0.0, prev_ref[...])
    xs_ref[halo:, :] = cur_ref[...]
    w = w_ref[...]

    def conv_strip(r0):
        y = w[GDN_CONV_WIDTH - 1:GDN_CONV_WIDTH] * xs_ref[halo + r0:halo + r0 + strip, :]
        for s in range(1, GDN_CONV_WIDTH):
            tap = GDN_CONV_WIDTH - 1 - s
            y = y + w[tap:tap + 1] * xs_ref[halo + r0 - s:halo + r0 - s + strip, :]
        return _silu(y)

    @pl.when(j < n_norm_blocks)
    def _():
        for r0 in range(0, tt, strip):
            y = conv_strip(r0)
            for hh in range(tc // GDN_HEAD_DIM):
                seg = y[:, hh * GDN_HEAD_DIM:(hh + 1) * GDN_HEAD_DIM]
                ss = jnp.sum(seg * seg, axis=-1, keepdims=True)
                o_ref[r0:r0 + strip, hh * GDN_HEAD_DIM:(hh + 1) * GDN_HEAD_DIM] = seg * lax.rsqrt(ss + RMS_EPS)

    @pl.when(j >= n_norm_blocks)
    def _():
        for r0 in range(0, tt, strip):
            o_ref[r0:r0 + strip, :] = conv_strip(r0)


def gdn_conv(proj, conv_w, *, conv_dim, key_dim, tt=512, tc=512, strip=32):
    t = proj.shape[0]
    halo = 8
    assert halo >= GDN_CONV_WIDTH - 1
    return pl.pallas_call(
        functools.partial(_conv_kernel, n_norm_blocks=2 * key_dim // tc, strip=strip),
        grid=(t // tt, conv_dim // tc),
        in_specs=[
            pl.BlockSpec((halo, tc), lambda i, j: (jnp.maximum(i * (tt // halo) - 1, 0), j)),
            pl.BlockSpec((tt, tc), lambda i, j: (i, j)),
            pl.BlockSpec((GDN_CONV_WIDTH, tc), lambda i, j: (0, j)),
        ],
        out_specs=pl.BlockSpec((tt, tc), lambda i, j: (i, j)),
        out_shape=jax.ShapeDtypeStruct((t, conv_dim), F32),
        scratch_shapes=[pltpu.VMEM((tt + halo, tc), F32)],
        compiler_params=_params("parallel", "parallel"),
        name="gdn_conv",
    )(proj, proj, conv_w)


def _gates_kernel(ba_ref, alog_ref, dtb_ref, beta_ref, g_ref):
    nh = beta_ref.shape[1]
    ba = ba_ref[...]
    beta_ref[...] = jax.nn.sigmoid(ba[:, :nh])
    z = ba[:, nh:] + dtb_ref[...]
    softplus = jnp.maximum(z, 0.0) + jnp.log1p(jnp.exp(-jnp.abs(z)))
    g_ref[...] = -jnp.exp(alog_ref[...]) * softplus


def gdn_gates(ba, a_log, dt_bias, *, tt=1024):
    t, two_h = ba.shape
    nh = two_h // 2
    return pl.pallas_call(
        _gates_kernel,
        grid=(t // tt,),
        in_specs=[
            pl.BlockSpec((tt, two_h), lambda i: (i, 0)),
            pl.BlockSpec((1, nh), lambda i: (0, 0)),
            pl.BlockSpec((1, nh), lambda i: (0, 0)),
        ],
        out_specs=[pl.BlockSpec((tt, nh), lambda i: (i, 0)), pl.BlockSpec((tt, nh), lambda i: (i, 0))],
        out_shape=[jax.ShapeDtypeStruct((t, nh), F32), jax.ShapeDtypeStruct((t, nh), F32)],
        compiler_params=_params("parallel"),
        name="gdn_gates",
    )(ba, a_log.reshape(1, nh), dt_bias.reshape(1, nh))


def _prep_kernel(q_ref, k_ref, v_ref, gsel_ref, bsel_ref, grow_ref, wq_ref, l2_ref, u_ref, glw_ref):
    kh = pl.program_id(0)
    tp, nh = gsel_ref.shape
    c = GDN_CHUNK
    dh = GDN_HEAD_DIM
    scale = dh ** -0.5
    lane_h = lax.broadcasted_iota(jnp.int32, (tp, nh), 1)

    def column(ref, head):
        return jnp.sum(jnp.where(lane_h == head, ref[...], 0.0), axis=1, keepdims=True)

    gcol = [column(gsel_ref, 2 * kh + hh) for hh in range(2)]
    bcol = [column(bsel_ref, 2 * kh + hh) for hh in range(2)]
    grow = [grow_ref[hh] for hh in range(2)]

    ii = lax.broadcasted_iota(jnp.int32, (c, 2 * c), 0)
    ll = lax.broadcasted_iota(jnp.int32, (c, 2 * c), 1)
    jj = jnp.bitwise_and(ll, c - 1)
    head_a = ll < c
    incl = ii >= jj
    strict = ii > jj
    wi = lax.broadcasted_iota(jnp.int32, (c, 4 * c), 0)
    wl = lax.broadcasted_iota(jnp.int32, (c, 4 * c), 1)
    eye_w = jnp.where(wi == jnp.bitwise_and(wl, c - 1), 1.0, 0.0).astype(F32)
    br = lax.broadcasted_iota(jnp.int32, (4 * c, 4 * c), 0)
    bc = lax.broadcasted_iota(jnp.int32, (4 * c, 4 * c), 1)
    same_block = (br // c) == (bc // c)
    strict_block = jnp.logical_and(same_block, br != bc)

    def block_diag(packed):
        return jnp.concatenate([packed] * 4, axis=0)

    def packed_matmul(a, b):
        bd = jnp.where(same_block, block_diag(b), 0.0).astype(BF16)
        return _dot(a.astype(BF16), bd)

    n_half = tp // (2 * c)
    lmats = [[] for _ in range(n_half)]
    rhs_blocks = [[] for _ in range(n_half)]
    for half in range(n_half):
        for y in range(2):
            n = 2 * half + y
            r = slice(n * c, (n + 1) * c)
            q = q_ref[r, :] * scale
            k = k_ref[r, :]
            g_pair = jnp.where(head_a, gcol[0][r], gcol[1][r])
            b_pair = jnp.where(head_a, bcol[0][r], bcol[1][r])
            grow_pair = jnp.concatenate([grow[0][:, r], grow[1][:, r]], axis=1)
            tri = jnp.where(incl, grow_pair, 0.0)
            gc = [jnp.sum(jnp.where(head_a, tri, 0.0), axis=1, keepdims=True),
                  jnp.sum(jnp.where(head_a, 0.0, tri), axis=1, keepdims=True)]
            gc_col = jnp.where(head_a, gc[0], gc[1])
            gc_row = jnp.sum(jnp.where(ii <= jj, g_pair, 0.0), axis=0, keepdims=True)
            decay = jnp.where(incl, jnp.exp(jnp.where(incl, gc_col - gc_row, 0.0)), 0.0)
            kb16 = k.astype(BF16)
            gram = _dot_nt(jnp.concatenate([q.astype(BF16), kb16], axis=0),
                           jnp.concatenate([kb16, kb16], axis=0))
            l2_ref[n, 0:c, :] = jnp.where(incl, gram[:c] * decay, 0.0).astype(BF16)
            lmats[half].append(jnp.where(strict, gram[c:] * b_pair * decay, 0.0))
            kups = []
            for hh in range(2):
                g_last = jnp.sum(gcol[hh][r], axis=0, keepdims=True)
                eg = jnp.exp(gc[hh])
                beta = bcol[hh][r]
                kups.append(k * jnp.exp(g_last - gc[hh]))
                wq_ref[n, c:2 * c, hh * dh:(hh + 1) * dh] = (q * eg).astype(BF16)
                glw_ref[n, :, hh * dh:(hh + 1) * dh] = jnp.broadcast_to(jnp.exp(g_last), (8, dh))
                rhs_blocks[half].append(
                    jnp.concatenate([v_ref[r, hh * dh:(hh + 1) * dh] * beta, k * (beta * eg)], axis=1))
            l2_ref[n, c:c + dh, :] = jnp.concatenate(kups, axis=0).T.astype(BF16)

    ps = [-jnp.concatenate(lm, axis=1) for lm in lmats]
    tinvs = [eye_w + p for p in ps]
    m = 1
    while 2 * m < c:
        ps = [packed_matmul(p, p) for p in ps]
        tinvs = [t + packed_matmul(t, p) for t, p in zip(tinvs, ps)]
        m *= 2
    for half in range(n_half):
        rhs_all = jnp.concatenate(rhs_blocks[half], axis=0)
        t_off = jnp.where(strict_block, block_diag(tinvs[half]), 0.0).astype(BF16)
        uw = rhs_all + _dot(t_off, rhs_all.astype(BF16))
        for y in range(2):
            n = 2 * half + y
            for hh in range(2):
                blk = uw[(2 * y + hh) * c:(2 * y + hh + 1) * c]
                u_ref[n * c:(n + 1) * c, hh * dh:(hh + 1) * dh] = blk[:, :dh]
                wq_ref[n, 0:c, hh * dh:(hh + 1) * dh] = blk[:, dh:].astype(BF16)


def gdn_prep(qkv, g, beta, g_rows, *, n_heads, rep, tp=1024):
    assert rep == 2
    t = qkv.shape[0]
    dh = GDN_HEAD_DIM
    c = GDN_CHUNK
    nk = n_heads // rep
    nc = tp // c
    return pl.pallas_call(
        _prep_kernel,
        grid=(nk, t // tp),
        in_specs=[
            pl.BlockSpec((tp, dh), lambda h, i: (i, h)),
            pl.BlockSpec((tp, dh), lambda h, i: (i, nk + h)),
            pl.BlockSpec((tp, 2 * dh), lambda h, i: (i, nk + h)),
            pl.BlockSpec((tp, n_heads), lambda h, i: (i, 0)),
            pl.BlockSpec((tp, n_heads), lambda h, i: (i, 0)),
            pl.BlockSpec((2, 1, tp), lambda h, i: (h, 0, i)),
        ],
        out_specs=[
            pl.BlockSpec((None, nc, 2 * c, 2 * dh), lambda h, i: (h, i, 0, 0)),
            pl.BlockSpec((None, nc, c + dh, 2 * c), lambda h, i: (h, i, 0, 0)),
            pl.BlockSpec((None, tp, 2 * dh), lambda h, i: (h, i, 0)),
            pl.BlockSpec((None, nc, 8, 2 * dh), lambda h, i: (h, i, 0, 0)),
        ],
        out_shape=[
            jax.ShapeDtypeStruct((nk, t // c, 2 * c, 2 * dh), BF16),
            jax.ShapeDtypeStruct((nk, t // c, c + dh, 2 * c), BF16),
            jax.ShapeDtypeStruct((nk, t, 2 * dh), F32),
            jax.ShapeDtypeStruct((nk, t // c, 8, 2 * dh), F32),
        ],
        compiler_params=_params("parallel", "parallel"),
        name="gdn_prep",
    )(qkv, qkv, qkv, g, beta, g_rows)


def _scan_kernel(wq_ref, l2_ref, u_ref, glw_ref, z_ref, gn_ref, o_ref, s_ref):
    pb, nc = wq_ref.shape[0], wq_ref.shape[1]
    gn = gn_ref[...]
    c = GDN_CHUNK
    dh = GDN_HEAD_DIM

    @pl.when(pl.program_id(1) == 0)
    def _():
        s_ref[...] = jnp.zeros_like(s_ref)

    def pair_diag(x):
        z = jnp.zeros((x.shape[0], dh), x.dtype)
        return jnp.concatenate([jnp.concatenate([x[:, :dh], z], axis=1),
                                jnp.concatenate([z, x[:, dh:]], axis=1)], axis=0)

    states = [s_ref[p] for p in range(pb)]
    for n in range(nc):
        r = slice(n * c, (n + 1) * c)
        for p in range(pb):
            ws = _dot(wq_ref[p, n], pair_diag(states[p].astype(BF16)))
            v_new = (u_ref[p, r, :] - ws[:c]).astype(BF16)
            upd = _dot(l2_ref[p, n], pair_diag(v_new))
            o = ws[c:] + upd[:c]
            states[p] = states[p] * glw_ref[p, n, 0:1, :] + upd[c:]
            for hh in range(2):
                cols = slice((2 * p + hh) * dh, (2 * p + hh + 1) * dh)
                gated = _rms_rows(o[:, hh * dh:(hh + 1) * dh], gn) * _silu(z_ref[r, cols])
                o_ref[r, cols] = gated.astype(BF16)
    for p in range(pb):
        s_ref[p] = states[p]


def gdn_scan(wq, l2, u, glw, proj, z_col0, out_norm, *, pb=8, tp=256):
    nk, t, two_dh = u.shape
    c = GDN_CHUNK
    nc = tp // c
    width = pb * two_dh
    blk4 = lambda a: pl.BlockSpec((pb, nc) + a.shape[2:], lambda h, i: (h, i, 0, 0))
    return pl.pallas_call(
        _scan_kernel,
        grid=(nk // pb, t // tp),
        in_specs=[blk4(wq), blk4(l2), pl.BlockSpec((pb, tp, two_dh), lambda h, i: (h, i, 0)), blk4(glw),
                  pl.BlockSpec((tp, width), lambda h, i: (i, z_col0 // width + h)),
                  pl.BlockSpec((1, GDN_HEAD_DIM), lambda h, i: (0, 0))],
        out_specs=pl.BlockSpec((tp, width), lambda h, i: (i, h)),
        out_shape=jax.ShapeDtypeStruct((t, nk * two_dh), BF16),
        scratch_shapes=[pltpu.VMEM((pb, two_dh // 2, two_dh), F32)],
        compiler_params=_params("parallel", "arbitrary"),
        name="gdn_scan",
    )(wq, l2, u, glw, proj, out_norm.reshape(1, GDN_HEAD_DIM))


def _kmeans_kernel(k_ref, mean_ref, kb_ref):
    k = k_ref[...]
    mean_ref[...] = jnp.mean(k, axis=0, keepdims=True)
    kb_ref[...] = k.astype(BF16)


def moba_kmeans(k):
    t, kd = k.shape
    nb = t // MOBA_BLOCK
    means, kb = pl.pallas_call(
        _kmeans_kernel,
        grid=(nb,),
        in_specs=[pl.BlockSpec((MOBA_BLOCK, kd), lambda i: (i, 0))],
        out_specs=[pl.BlockSpec((None, 1, kd), lambda i: (i, 0, 0)),
                   pl.BlockSpec((MOBA_BLOCK, kd), lambda i: (i, 0))],
        out_shape=[jax.ShapeDtypeStruct((nb, 1, kd), F32), jax.ShapeDtypeStruct((t, kd), BF16)],
        compiler_params=_params("parallel"),
        name="moba_kmeans",
    )(k)
    return means.reshape(nb, kd), kb


def _rel_bucket_table(n_dist):
    n = np.arange(n_dist)
    max_exact = N_REL_BUCKETS // 2
    ratio = np.log(np.maximum(n, max_exact).astype(np.float32) / np.float32(max_exact)) \
        / np.float32(math.log(REL_MAX_DISTANCE / max_exact))
    large = np.minimum(max_exact + (ratio.astype(np.float32) * (N_REL_BUCKETS - max_exact)).astype(np.int32),
                       N_REL_BUCKETS - 1)
    return np.where(n < max_exact, n, large).astype(np.int32)


def _bucket_tiles():
    table = _rel_bucket_table(2 * MOBA_BLOCK)
    a = np.arange(MOBA_BLOCK)[:, None]
    b = np.arange(MOBA_BLOCK)[None, :]
    d_own = b - a
    own = np.where(d_own >= 0, table[np.maximum(d_own, 0)], -1)
    prev = table[MOBA_BLOCK + b - a]
    return np.stack([own, prev]).astype(np.int32)


def _moba_kernel(rb_ref, qt_ref, k_ref, vt_ref, km_ref, bucket_ref, o_ref,
                 bias_ref, sel_ref, m_ref, l_ref, acc_ref, *, group):
    g = pl.program_id(0)
    i = pl.program_id(1)
    bs = MOBA_BLOCK
    dh = ATT_HEAD_DIM
    nb = km_ref.shape[0]
    nq = group * bs
    scale = dh ** -0.5 * LOG2_E

    @pl.when(i == 0)
    def _():
        for hh in range(group):
            for t in range(2):
                bk = bucket_ref[t]
                bias = jnp.full(bk.shape, MASK_VALUE, F32)
                for b in range(N_REL_BUCKETS):
                    bias = jnp.where(bk == b, rb_ref[b, g * group + hh] * LOG2_E, bias)
                bias_ref[t, :, hh * bs:(hh + 1) * bs] = bias

    qt = jnp.concatenate([qt_ref[hh * dh:(hh + 1) * dh, :] for hh in range(group)], axis=1)
    qt = (qt * scale).astype(BF16)

    blk = lax.broadcasted_iota(jnp.int32, (nb, nq), 0)
    gate = jnp.where(blk < i, _dot(km_ref[...].astype(BF16), qt), -jnp.inf)
    sel = jnp.zeros((nb, nq), F32)
    for r in range(MOBA_TOPK):
        mx = jnp.max(gate, axis=0, keepdims=True)
        first = jnp.min(jnp.where(gate == mx, blk, nb), axis=0, keepdims=True)
        pick = blk == first
        sel = jnp.where(jnp.logical_and(pick, r < i), 1.0, sel)
        gate = jnp.where(pick, -jnp.inf, gate)
    sel_ref[...] = sel

    def keys(j, nblk=1):
        return k_ref[pl.ds(pl.multiple_of(j * bs, bs), nblk * bs), :]

    def values_t(j, nblk=1):
        return vt_ref[:, pl.ds(pl.multiple_of(j * bs, bs), nblk * bs)]

    OWN, PREV, FAR = 0, 1, 2

    def attend(state, units):
        staged = [[_dot(keys(j, len(kinds)), qt[:, hh * bs:(hh + 1) * bs]) for hh in range(group)]
                  for j, kinds in units]
        m_all, l_all, acc_all = state
        for (j, kinds), scores in zip(units, staged):
            vt_blk = values_t(j, len(kinds))
            sel_rows = [None if kind == OWN else sel_ref[pl.ds(j + b, 1), :] for b, kind in enumerate(kinds)]
            m_out, l_out, acc_out = [], [], []
            for hh in range(group):
                cols = slice(hh * bs, (hh + 1) * bs)
                far_bias = rb_ref[N_REL_BUCKETS - 1, g * group + hh] * LOG2_E
                parts = []
                for b, kind in enumerate(kinds):
                    part = scores[hh][b * bs:(b + 1) * bs]
                    if kind == OWN:
                        part = part + bias_ref[0, :, cols]
                    elif kind == PREV:
                        part = part + bias_ref[1, :, cols] + jnp.where(sel_rows[b][:, cols] > 0.0, 0.0, MASK_VALUE)
                    else:
                        part = part + jnp.where(sel_rows[b][:, cols] > 0.0, far_bias, MASK_VALUE)
                    parts.append(part)
                m_old = m_all[:, cols]
                m_new = m_old
                for part in parts:
                    m_new = jnp.maximum(m_new, jnp.max(part, axis=0, keepdims=True))
                alpha = jnp.exp2(m_old - m_new)
                ps = [jnp.exp2(part - m_new) for part in parts]
                l_new = alpha * l_all[:, cols]
                for p in ps:
                    l_new = l_new + jnp.sum(p, axis=0, keepdims=True)
                pb16 = jnp.concatenate([p.astype(BF16) for p in ps], axis=0) if len(ps) > 1 else ps[0].astype(BF16)
                m_out.append(m_new)
                l_out.append(l_new)
                acc_out.append(alpha * acc_all[:, cols] + _dot(vt_blk, pb16))
            m_all = jnp.concatenate(m_out, axis=1)
            l_all = jnp.concatenate(l_out, axis=1)
            acc_all = jnp.concatenate(acc_out, axis=1)
        return m_all, l_all, acc_all

    def load_state():
        return m_ref[...], l_ref[...], acc_ref[...]

    def store_state(state):
        m_ref[...], l_ref[...], acc_ref[...] = state

    empty = (jnp.full((1, nq), MASK_VALUE, F32), jnp.zeros((1, nq), F32), jnp.zeros((dh, nq), F32))

    n_old = jnp.maximum(i - 1, 0)
    odd = n_old % 2 == 1

    @pl.when(i == 0)
    def _():
        store_state(attend(empty, [(i, (OWN,))]))

    @pl.when(jnp.logical_and(i >= 1, jnp.logical_not(odd)))
    def _():
        store_state(attend(empty, [(i - 1, (PREV, OWN))]))

    @pl.when(odd)
    def _():
        store_state(attend(empty, [(i - 2, (FAR, PREV, OWN))]))

    n_far = n_old - n_old % 2

    def far_body(jj, carry):
        store_state(attend(load_state(), [(4 * jj, (FAR, FAR)), (4 * jj + 2, (FAR, FAR))]))
        return carry

    lax.fori_loop(0, n_far // 4, far_body, 0)

    @pl.when(n_far % 4 == 2)
    def _():
        store_state(attend(load_state(), [(n_far - 2, (FAR, FAR))]))

    out = acc_ref[...] / l_ref[...]
    for hh in range(group):
        o_ref[:, hh * dh:(hh + 1) * dh] = out[:, hh * bs:(hh + 1) * bs].T.astype(BF16)


def moba_attention(qt, k, vt, k_means, rel_bias, *, n_heads, n_kv_heads):
    dh = ATT_HEAD_DIM
    t = k.shape[0]
    bs = MOBA_BLOCK
    nb = t // bs
    group = n_heads // n_kv_heads
    nq = group * bs
    assert REL_MAX_DISTANCE <= bs
    buckets = jnp.asarray(_bucket_tiles())
    return pl.pallas_call(
        functools.partial(_moba_kernel, group=group),
        grid=(n_kv_heads, nb),
        in_specs=[
            pl.BlockSpec(memory_space=pltpu.SMEM),
            pl.BlockSpec((group * dh, bs), lambda g, i: (g, i)),
            pl.BlockSpec((t, dh), lambda g, i: (0, g)),
            pl.BlockSpec((dh, t), lambda g, i: (g, 0)),
            pl.BlockSpec((nb, dh), lambda g, i: (0, g)),
            pl.BlockSpec((2, bs, bs), lambda g, i: (0, 0, 0)),
        ],
        out_specs=pl.BlockSpec((bs, group * dh), lambda g, i: (i, g)),
        out_shape=jax.ShapeDtypeStruct((t, n_heads * dh), BF16),
        scratch_shapes=[
            pltpu.VMEM((2, bs, nq), F32),
            pltpu.VMEM((nb, nq), F32),
            pltpu.VMEM((1, nq), F32),
            pltpu.VMEM((1, nq), F32),
            pltpu.VMEM((dh, nq), F32),
        ],
        compiler_params=_params("parallel", "arbitrary"),
        name="moba_attention",
    )(rel_bias, qt, k, vt, k_means, buckets)


def kernel(x, ffn1_norm, ffn1_w_in, ffn1_w_out, mix_norm, ffn2_norm, ffn2_w_in, ffn2_w_out,
           gdn_w_in, gdn_conv_w, gdn_a_log, gdn_dt_bias, gdn_out_norm, gdn_w_out,
           kv_norm, w_kv, moba_w_q, moba_w_o, rel_bias, final_norm):
    batch, seq, d_model = x.shape
    depth = ffn1_norm.shape[0]
    n_a = gdn_w_in.shape[0]
    n_v_heads = gdn_a_log.shape[1]
    value_dim = gdn_w_out.shape[1]
    conv_dim = gdn_conv_w.shape[2]
    key_dim = (conv_dim - value_dim) // 2
    rep = value_dim // key_dim
    n_heads = moba_w_q.shape[2] // ATT_HEAD_DIM
    kv_dim = w_kv.shape[1] // 2
    n_kv_heads = kv_dim // ATT_HEAD_DIM
    bf = lambda a: a.astype(BF16)
    gdn_w_in_t = jnp.swapaxes(gdn_w_in, 1, 2)

    outs = []
    for bi in range(batch):
        h = x[bi]
        k_nat = vt = k_means = None
        for layer in range(depth):
            if layer == n_a:
                k_f32 = norm_matmul(h, kv_norm, bf(w_kv[:, :kv_dim]), tm=512, tn=kv_dim, name="kv_k")
                vt = norm_matmul_t(h, kv_norm, bf(w_kv[:, kv_dim:].T), tm=512, tn=kv_dim, name="kv_vt",
                                   out_dtype=BF16)
                k_means, k_nat = moba_kmeans(k_f32)
            h = ffn(h, ffn1_norm[layer], ffn1_w_in, ffn1_w_out, layer)
            if layer < n_a:
                main = conv_dim + value_dim
                n_ba = 2 * n_v_heads
                proj = norm_matmul(h, mix_norm[layer], gdn_w_in_t, layer=layer, n=main, tm=2048, tn=512,
                                   name="gdn_in")
                ba = norm_matmul(h, mix_norm[layer], gdn_w_in_t, layer=layer, n=n_ba, col_block0=main // n_ba,
                                 tm=2048, tn=n_ba, name="gdn_in_ba")
                beta, g = gdn_gates(ba, gdn_a_log[layer], gdn_dt_bias[layer])
                qkv = gdn_conv(proj, gdn_conv_w[layer], conv_dim=conv_dim, key_dim=key_dim)
                g_rows = g.T.reshape(n_v_heads, 1, seq)
                wq, l2, u, glw = gdn_prep(qkv, g, beta, g_rows, n_heads=n_v_heads, rep=rep)
                gated = gdn_scan(wq, l2, u, glw, proj, conv_dim, gdn_out_norm[layer])
                h = matmul_res(gated, bf(gdn_w_out[layer]), h, tm=512, tn=512, name="gdn_out")
            else:
                j = layer - n_a
                qt = norm_matmul_t(h, mix_norm[layer], bf(moba_w_q[j].T), tm=512, tn=512, name="moba_qt")
                att = moba_attention(qt, k_nat, vt, k_means, rel_bias,
                                     n_heads=n_heads, n_kv_heads=n_kv_heads)
                h = matmul_res(att, bf(moba_w_o[j]), h, tm=512, tn=512, name="moba_out")
            last = layer == depth - 1
            h = ffn(h, ffn2_norm[layer], ffn2_w_in, ffn2_w_out, layer, final_norm if last else None)
        outs.append(h)
    return jnp.stack(outs)
```

```python
import functools
import math

import numpy as np
import jax
import jax.numpy as jnp
from jax import lax
from jax.experimental import pallas as pl
from jax.experimental.pallas import tpu as pltpu

F32 = jnp.float32
BF16 = jnp.bfloat16

RMS_EPS = 1e-6
GDN_HEAD_DIM = 128
GDN_CHUNK = 64
GDN_CONV_WIDTH = 4
ATT_HEAD_DIM = 128
MOBA_BLOCK = 256
MOBA_TOPK = 3
N_REL_BUCKETS = 32
REL_MAX_DISTANCE = 128
MASK_VALUE = -1e30
LOG2_E = 1.4426950408889634
MOBA_Q_SCALE = ATT_HEAD_DIM ** -0.5 * LOG2_E

VMEM_LIMIT_BYTES = 58 * 1024 * 1024


def _params(*sem):
    return pltpu.CompilerParams(dimension_semantics=sem, vmem_limit_bytes=VMEM_LIMIT_BYTES)


def _rms_rows(x, g):
    ms = jnp.mean(x * x, axis=-1, keepdims=True)
    return x * lax.rsqrt(ms + RMS_EPS) * g


def _silu(x):
    return x * jax.nn.sigmoid(x)


def _dot(a, b):
    return jnp.dot(a, b, preferred_element_type=F32)


def _dot_nt(a, b):
    return lax.dot_general(a, b, (((1,), (1,)), ((), ())), preferred_element_type=F32)


def _norm_matmul_kernel(x_ref, g_ref, w_ref, o_ref, xn_ref, *, w_rows_are_outputs):
    @pl.when(pl.program_id(1) == 0)
    def _():
        xn_ref[...] = _rms_rows(x_ref[...], g_ref[...]).astype(BF16)

    w = w_ref[...].astype(BF16)
    y = _dot_nt(xn_ref[...], w) if w_rows_are_outputs else _dot(xn_ref[...], w)
    o_ref[...] = y.astype(o_ref.dtype)


def _row_block(tm, d):
    return pl.BlockSpec((tm, d), lambda i, j: (i, 0), pipeline_mode=pl.Buffered(1))


def norm_matmul(x, g, w, *, tm, tn, name, layer=None, col_block0=0, n=None):
    t, d = x.shape
    if layer is None:
        n = w.shape[-1] if n is None else n
        w_spec = pl.BlockSpec((d, tn), lambda i, j: (0, col_block0 + j))
    else:
        n = w.shape[1] if n is None else n
        w_spec = pl.BlockSpec((None, tn, d), lambda i, j: (layer, col_block0 + j, 0))
    return pl.pallas_call(
        functools.partial(_norm_matmul_kernel, w_rows_are_outputs=layer is not None),
        grid=(t // tm, n // tn),
        in_specs=[_row_block(tm, d), pl.BlockSpec((1, d), lambda i, j: (0, 0)), w_spec],
        out_specs=pl.BlockSpec((tm, tn), lambda i, j: (i, j)),
        out_shape=jax.ShapeDtypeStruct((t, n), F32),
        scratch_shapes=[pltpu.VMEM((tm, d), BF16)],
        compiler_params=_params("parallel", "arbitrary"),
        name=name,
    )(x, g.reshape(1, d), w)


def _norm_matmul_t_kernel(x_ref, g_ref, wt_ref, o_ref, xn_ref, *, out_scale):
    @pl.when(pl.program_id(1) == 0)
    def _():
        xn_ref[...] = _rms_rows(x_ref[...], g_ref[...]).astype(BF16)

    y = _dot_nt(wt_ref[...], xn_ref[...])
    if out_scale != 1.0:
        y = y * out_scale
    o_ref[...] = y.astype(o_ref.dtype)


def norm_matmul_t(x, g, wt, *, tm, tn, name, out_dtype=F32, out_scale=1.0):
    t, d = x.shape
    n = wt.shape[0]
    return pl.pallas_call(
        functools.partial(_norm_matmul_t_kernel, out_scale=out_scale),
        grid=(t // tm, n // tn),
        in_specs=[
            _row_block(tm, d),
            pl.BlockSpec((1, d), lambda i, j: (0, 0)),
            pl.BlockSpec((tn, d), lambda i, j: (j, 0)),
        ],
        out_specs=pl.BlockSpec((tn, tm), lambda i, j: (j, i)),
        out_shape=jax.ShapeDtypeStruct((n, t), out_dtype),
        scratch_shapes=[pltpu.VMEM((tm, d), BF16)],
        compiler_params=_params("parallel", "arbitrary"),
        name=name,
    )(x, g.reshape(1, d), wt)


def _ffn_kernel(h_ref, g_ref, wg_ref, wu_ref, wo_ref, fg_ref, o_ref, xn_ref, *, final_norm):
    j = pl.program_id(1)
    tm = h_ref.shape[0]
    strip = min(tm, 256)

    @pl.when(j == 0)
    def _():
        for r0 in range(0, tm, strip):
            r = slice(r0, r0 + strip)
            xn_ref[r, :] = _rms_rows(h_ref[r, :], g_ref[...]).astype(BF16)
            o_ref[r, :] = jnp.zeros((strip, o_ref.shape[1]), F32)

    xn = xn_ref[...]
    tf = wg_ref.shape[1]
    half = min(tf, 256)
    acts = []
    for c0 in range(0, tf, half):
        gate = _dot(xn, wg_ref[:, c0:c0 + half].astype(BF16))
        up = _dot(xn, wu_ref[:, c0:c0 + half].astype(BF16))
        acts.append((_silu(gate) * up).astype(BF16))
    act = acts[0] if len(acts) == 1 else jnp.concatenate(acts, axis=1)
    o_ref[...] += _dot(act, wo_ref[...].astype(BF16))

    @pl.when(j == pl.num_programs(1) - 1)
    def _():
        for r0 in range(0, tm, strip):
            r = slice(r0, r0 + strip)
            y = h_ref[r, :] + 0.5 * o_ref[r, :]
            if final_norm:
                y = _rms_rows(y, fg_ref[...])
            o_ref[r, :] = y


def ffn(h, g, w_in, w_out, layer, final_g=None, *, tm=1024, tf=512):
    t, d = h.shape
    f = w_out.shape[1]
    nf = f // tf
    final_norm = final_g is not None
    fg = (final_g if final_norm else g).reshape(1, d)
    return pl.pallas_call(
        functools.partial(_ffn_kernel, final_norm=final_norm),
        grid=(t // tm, nf),
        in_specs=[
            _row_block(tm, d),
            pl.BlockSpec((1, d), lambda i, j: (0, 0)),
            pl.BlockSpec((None, d, tf), lambda i, j: (layer, 0, j)),
            pl.BlockSpec((None, d, tf), lambda i, j: (layer, 0, j + nf)),
            pl.BlockSpec((None, tf, d), lambda i, j: (layer, j, 0)),
            pl.BlockSpec((1, d), lambda i, j: (0, 0)),
        ],
        out_specs=_row_block(tm, d),
        out_shape=jax.ShapeDtypeStruct((t, d), F32),
        scratch_shapes=[pltpu.VMEM((tm, d), BF16)],
        compiler_params=_params("parallel", "arbitrary"),
        name="ffn",
    )(h, g.reshape(1, d), w_in, w_in, w_out, fg)


def _matmul_res_kernel(a_ref, w_ref, r_ref, o_ref):
    o_ref[...] = r_ref[...] + _dot(a_ref[...], w_ref[...])


def matmul_res(a, w, r, *, tm, tn, name):
    t, k = a.shape
    n = w.shape[1]
    return pl.pallas_call(
        _matmul_res_kernel,
        grid=(t // tm, n // tn),
        in_specs=[
            pl.BlockSpec((tm, k), lambda i, j: (i, 0)),
            pl.BlockSpec((k, tn), lambda i, j: (0, j)),
            pl.BlockSpec((tm, tn), lambda i, j: (i, j)),
        ],
        out_specs=pl.BlockSpec((tm, tn), lambda i, j: (i, j)),
        out_shape=jax.ShapeDtypeStruct((t, n), F32),
        compiler_params=_params("parallel", "arbitrary"),
        name=name,
    )(a, w, r)


def _conv_kernel(prev_ref, cur_ref, w_ref, o_ref, xs_ref, *, n_norm_blocks, strip):
    i = pl.program_id(0)
    j = pl.program_id(1)
    tt, tc = cur_ref.shape
    halo = prev_ref.shape[0]
    xs_ref[0:halo, :] = jnp.where(i == 0, 0.0, prev_ref[...])
    xs_ref[halo:, :] = cur_ref[...]
    w = w_ref[...]

    def conv_strip(r0):
        y = w[GDN_CONV_WIDTH - 1:GDN_CONV_WIDTH] * xs_ref[halo + r0:halo + r0 + strip, :]
        for s in range(1, GDN_CONV_WIDTH):
            tap = GDN_CONV_WIDTH - 1 - s
            y = y + w[tap:tap + 1] * xs_ref[halo + r0 - s:halo + r0 - s + strip, :]
        return _silu(y)

    @pl.when(j < n_norm_blocks)
    def _():
        for r0 in range(0, tt, strip):
            y = conv_strip(r0)
            for hh in range(tc // GDN_HEAD_DIM):
                seg = y[:, hh * GDN_HEAD_DIM:(hh + 1) * GDN_HEAD_DIM]
                ss = jnp.sum(seg * seg, axis=-1, keepdims=True)
                o_ref[r0:r0 + strip, hh * GDN_HEAD_DIM:(hh + 1) * GDN_HEAD_DIM] = seg * lax.rsqrt(ss + RMS_EPS)

    @pl.when(j >= n_norm_blocks)
    def _():
        for r0 in range(0, tt, strip):
            o_ref[r0:r0 + strip, :] = conv_strip(r0)


def gdn_conv(proj, conv_w, *, conv_dim, key_dim, tt=512, tc=512, strip=32):
    t = proj.shape[0]
    halo = 8
    assert halo >= GDN_CONV_WIDTH - 1
    return pl.pallas_call(
        functools.partial(_conv_kernel, n_norm_blocks=2 * key_dim // tc, strip=strip),
        grid=(t // tt, conv_dim // tc),
        in_specs=[
            pl.BlockSpec((halo, tc), lambda i, j: (jnp.maximum(i * (tt // halo) - 1, 0), j)),
            pl.BlockSpec((tt, tc), lambda i, j: (i, j)),
            pl.BlockSpec((GDN_CONV_WIDTH, tc), lambda i, j: (0, j)),
        ],
        out_specs=pl.BlockSpec((tt, tc), lambda i, j: (i, j)),
        out_shape=jax.ShapeDtypeStruct((t, conv_dim), F32),
        scratch_shapes=[pltpu.VMEM((tt + halo, tc), F32)],
        compiler_params=_params("parallel", "parallel"),
        name="gdn_conv",
    )(proj, proj, conv_w)


def _gates_kernel(ba_ref, alog_ref, dtb_ref, beta_ref, g_ref):
    nh = beta_ref.shape[1]
    ba = ba_ref[...]
    beta_ref[...] = jax.nn.sigmoid(ba[:, :nh])
    z = ba[:, nh:] + dtb_ref[...]
    softplus = jnp.maximum(z, 0.0) + jnp.log1p(jnp.exp(-jnp.abs(z)))
    g_ref[...] = -jnp.exp(alog_ref[...]) * softplus


def gdn_gates(ba, a_log, dt_bias, *, tt=1024):
    t, two_h = ba.shape
    nh = two_h // 2
    return pl.pallas_call(
        _gates_kernel,
        grid=(t // tt,),
        in_specs=[
            pl.BlockSpec((tt, two_h), lambda i: (i, 0)),
            pl.BlockSpec((1, nh), lambda i: (0, 0)),
            pl.BlockSpec((1, nh), lambda i: (0, 0)),
        ],
        out_specs=[pl.BlockSpec((tt, nh), lambda i: (i, 0)), pl.BlockSpec((tt, nh), lambda i: (i, 0))],
        out_shape=[jax.ShapeDtypeStruct((t, nh), F32), jax.ShapeDtypeStruct((t, nh), F32)],
        compiler_params=_params("parallel"),
        name="gdn_gates",
    )(ba, a_log.reshape(1, nh), dt_bias.reshape(1, nh))


def _prep_kernel(q_ref, k_ref, v_ref, gsel_ref, bsel_ref, grow_ref, wq_ref, l2_ref, u_ref, glw_ref):
    kh = pl.program_id(0)
    tp, nh = gsel_ref.shape
    c = GDN_CHUNK
    dh = GDN_HEAD_DIM
    scale = dh ** -0.5
    lane_h = lax.broadcasted_iota(jnp.int32, (tp, nh), 1)

    def column(ref, head):
        return jnp.sum(jnp.where(lane_h == head, ref[...], 0.0), axis=1, keepdims=True)

    gcol = [column(gsel_ref, 2 * kh + hh) for hh in range(2)]
    bcol = [column(bsel_ref, 2 * kh + hh) for hh in range(2)]
    grow = [grow_ref[hh] for hh in range(2)]

    ii = lax.broadcasted_iota(jnp.int32, (c, 2 * c), 0)
    ll = lax.broadcasted_iota(jnp.int32, (c, 2 * c), 1)
    jj = jnp.bitwise_and(ll, c - 1)
    head_a = ll < c
    incl = ii >= jj
    strict = ii > jj
    wi = lax.broadcasted_iota(jnp.int32, (c, 4 * c), 0)
    wl = lax.broadcasted_iota(jnp.int32, (c, 4 * c), 1)
    eye_w = jnp.where(wi == jnp.bitwise_and(wl, c - 1), 1.0, 0.0).astype(F32)
    br = lax.broadcasted_iota(jnp.int32, (4 * c, 4 * c), 0)
    bc = lax.broadcasted_iota(jnp.int32, (4 * c, 4 * c), 1)
    same_block = (br // c) == (bc // c)
    strict_block = jnp.logical_and(same_block, br != bc)

    def block_diag(packed):
        return jnp.concatenate([packed] * 4, axis=0)

    def packed_matmul(a, b):
        bd = jnp.where(same_block, block_diag(b), 0.0).astype(BF16)
        return _dot(a.astype(BF16), bd)

    n_half = tp // (2 * c)
    lmats = [[] for _ in range(n_half)]
    rhs_blocks = [[] for _ in range(n_half)]
    for half in range(n_half):
        for y in range(2):
            n = 2 * half + y
            r = slice(n * c, (n + 1) * c)
            q = q_ref[r, :] * scale
            k = k_ref[r, :]
            g_pair = jnp.where(head_a, gcol[0][r], gcol[1][r])
            b_pair = jnp.where(head_a, bcol[0][r], bcol[1][r])
            grow_pair = jnp.concatenate([grow[0][:, r], grow[1][:, r]], axis=1)
            tri = jnp.where(incl, grow_pair, 0.0)
            gc = [jnp.sum(jnp.where(head_a, tri, 0.0), axis=1, keepdims=True),
                  jnp.sum(jnp.where(head_a, 0.0, tri), axis=1, keepdims=True)]
            gc_col = jnp.where(head_a, gc[0], gc[1])
            gc_row = jnp.sum(jnp.where(ii <= jj, g_pair, 0.0), axis=0, keepdims=True)
            decay = jnp.where(incl, jnp.exp(jnp.where(incl, gc_col - gc_row, 0.0)), 0.0)
            kb16 = k.astype(BF16)
            gram = _dot_nt(jnp.concatenate([q.astype(BF16), kb16], axis=0),
                           jnp.concatenate([kb16, kb16], axis=0))
            l2_ref[n, 0:c, :] = jnp.where(incl, gram[:c] * decay, 0.0).astype(BF16)
            lmats[half].append(jnp.where(strict, gram[c:] * b_pair * decay, 0.0))
            kups = []
            for hh in range(2):
                g_last = jnp.sum(gcol[hh][r], axis=0, keepdims=True)
                eg = jnp.exp(gc[hh])
                beta = bcol[hh][r]
                kups.append(k * jnp.exp(g_last - gc[hh]))
                wq_ref[n, c:2 * c, hh * dh:(hh + 1) * dh] = (q * eg).astype(BF16)
                glw_ref[n, :, hh * dh:(hh + 1) * dh] = jnp.broadcast_to(jnp.exp(g_last), (8, dh))
                rhs_blocks[half].append(
                    jnp.concatenate([v_ref[r, hh * dh:(hh + 1) * dh] * beta, k * (beta * eg)], axis=1))
            l2_ref[n, c:c + dh, :] = jnp.concatenate(kups, axis=0).T.astype(BF16)

    ps = [-jnp.concatenate(lm, axis=1) for lm in lmats]
    tinvs = [eye_w + p for p in ps]
    m = 1
    while 2 * m < c:
        ps = [packed_matmul(p, p) for p in ps]
        tinvs = [t + packed_matmul(t, p) for t, p in zip(tinvs, ps)]
        m *= 2
    for half in range(n_half):
        rhs_all = jnp.concatenate(rhs_blocks[half], axis=0)
        t_off = jnp.where(strict_block, block_diag(tinvs[half]), 0.0).astype(BF16)
        uw = rhs_all + _dot(t_off, rhs_all.astype(BF16))
        for y in range(2):
            n = 2 * half + y
            for hh in range(2):
                blk = uw[(2 * y + hh) * c:(2 * y + hh + 1) * c]
                u_ref[n * c:(n + 1) * c, hh * dh:(hh + 1) * dh] = blk[:, :dh]
                wq_ref[n, 0:c, hh * dh:(hh + 1) * dh] = blk[:, dh:].astype(BF16)


def gdn_prep(qkv, g, beta, g_rows, *, n_heads, rep, tp=2048):
    assert rep == 2
    t = qkv.shape[0]
    dh = GDN_HEAD_DIM
    c = GDN_CHUNK
    nk = n_heads // rep
    nc = tp // c
    return pl.pallas_call(
        _prep_kernel,
        grid=(nk, t // tp),
        in_specs=[
            pl.BlockSpec((tp, dh), lambda h, i: (i, h)),
            pl.BlockSpec((tp, dh), lambda h, i: (i, nk + h)),
            pl.BlockSpec((tp, 2 * dh), lambda h, i: (i, nk + h)),
            pl.BlockSpec((tp, n_heads), lambda h, i: (i, 0)),
            pl.BlockSpec((tp, n_heads), lambda h, i: (i, 0)),
            pl.BlockSpec((2, 1, tp), lambda h, i: (h, 0, i)),
        ],
        out_specs=[
            pl.BlockSpec((None, nc, 2 * c, 2 * dh), lambda h, i: (h, i, 0, 0)),
            pl.BlockSpec((None, nc, c + dh, 2 * c), lambda h, i: (h, i, 0, 0)),
            pl.BlockSpec((None, tp, 2 * dh), lambda h, i: (h, i, 0)),
            pl.BlockSpec((None, nc, 8, 2 * dh), lambda h, i: (h, i, 0, 0)),
        ],
        out_shape=[
            jax.ShapeDtypeStruct((nk, t // c, 2 * c, 2 * dh), BF16),
            jax.ShapeDtypeStruct((nk, t // c, c + dh, 2 * c), BF16),
            jax.ShapeDtypeStruct((nk, t, 2 * dh), F32),
            jax.ShapeDtypeStruct((nk, t // c, 8, 2 * dh), F32),
        ],
        compiler_params=_params("parallel", "parallel"),
        name="gdn_prep",
    )(qkv, qkv, qkv, g, beta, g_rows)


def _scan_kernel(wq_ref, l2_ref, u_ref, glw_ref, z_ref, gn_ref, o_ref, s_ref):
    pb, nc = wq_ref.shape[0], wq_ref.shape[1]
    gn = gn_ref[...]
    c = GDN_CHUNK
    dh = GDN_HEAD_DIM

    @pl.when(pl.program_id(1) == 0)
    def _():
        s_ref[...] = jnp.zeros_like(s_ref)

    def pair_diag(x):
        z = jnp.zeros((x.shape[0], dh), x.dtype)
        return jnp.concatenate([jnp.concatenate([x[:, :dh], z], axis=1),
                                jnp.concatenate([z, x[:, dh:]], axis=1)], axis=0)

    states = [s_ref[p] for p in range(pb)]
    for n in range(nc):
        r = slice(n * c, (n + 1) * c)
        for p in range(pb):
            ws = _dot(wq_ref[p, n], pair_diag(states[p].astype(BF16)))
            v_new = (u_ref[p, r, :] - ws[:c]).astype(BF16)
            upd = _dot(l2_ref[p, n], pair_diag(v_new))
            o = ws[c:] + upd[:c]
            states[p] = states[p] * glw_ref[p, n, 0:1, :] + upd[c:]
            for hh in range(2):
                cols = slice((2 * p + hh) * dh, (2 * p + hh + 1) * dh)
                gated = _rms_rows(o[:, hh * dh:(hh + 1) * dh], gn) * _silu(z_ref[r, cols])
                o_ref[r, cols] = gated.astype(BF16)
    for p in range(pb):
        s_ref[p] = states[p]


def gdn_scan(wq, l2, u, glw, proj, z_col0, out_norm, *, pb=8, tp=256):
    nk, t, two_dh = u.shape
    c = GDN_CHUNK
    nc = tp // c
    width = pb * two_dh
    blk4 = lambda a: pl.BlockSpec((pb, nc) + a.shape[2:], lambda h, i: (h, i, 0, 0))
    return pl.pallas_call(
        _scan_kernel,
        grid=(nk // pb, t // tp),
        in_specs=[blk4(wq), blk4(l2), pl.BlockSpec((pb, tp, two_dh), lambda h, i: (h, i, 0)), blk4(glw),
                  pl.BlockSpec((tp, width), lambda h, i: (i, z_col0 // width + h)),
                  pl.BlockSpec((1, GDN_HEAD_DIM), lambda h, i: (0, 0))],
        out_specs=pl.BlockSpec((tp, width), lambda h, i: (i, h)),
        out_shape=jax.ShapeDtypeStruct((t, nk * two_dh), BF16),
        scratch_shapes=[pltpu.VMEM((pb, two_dh // 2, two_dh), F32)],
        compiler_params=_params("parallel", "arbitrary"),
        name="gdn_scan",
    )(wq, l2, u, glw, proj, out_norm.reshape(1, GDN_HEAD_DIM))


def _kmeans_kernel(k_ref, mean_ref, kb_ref):
    k = k_ref[...]
    mean_ref[...] = jnp.mean(k, axis=0, keepdims=True)
    kb_ref[...] = k.astype(BF16)


def moba_kmeans(k):
    t, kd = k.shape
    nb = t // MOBA_BLOCK
    means, kb = pl.pallas_call(
        _kmeans_kernel,
        grid=(nb,),
        in_specs=[pl.BlockSpec((MOBA_BLOCK, kd), lambda i: (i, 0))],
        out_specs=[pl.BlockSpec((None, 1, kd), lambda i: (i, 0, 0)),
                   pl.BlockSpec((MOBA_BLOCK, kd), lambda i: (i, 0))],
        out_shape=[jax.ShapeDtypeStruct((nb, 1, kd), F32), jax.ShapeDtypeStruct((t, kd), BF16)],
        compiler_params=_params("parallel"),
        name="moba_kmeans",
    )(k)
    return means.reshape(nb, kd), kb


def _rel_bucket_table(n_dist):
    n = np.arange(n_dist)
    max_exact = N_REL_BUCKETS // 2
    ratio = np.log(np.maximum(n, max_exact).astype(np.float32) / np.float32(max_exact)) \
        / np.float32(math.log(REL_MAX_DISTANCE / max_exact))
    large = np.minimum(max_exact + (ratio.astype(np.float32) * (N_REL_BUCKETS - max_exact)).astype(np.int32),
                       N_REL_BUCKETS - 1)
    return np.where(n < max_exact, n, large).astype(np.int32)


def _bucket_tiles():
    table = _rel_bucket_table(2 * MOBA_BLOCK)
    a = np.arange(MOBA_BLOCK)[:, None]
    b = np.arange(MOBA_BLOCK)[None, :]
    d_own = b - a
    own = np.where(d_own >= 0, table[np.maximum(d_own, 0)], -1)
    prev = table[MOBA_BLOCK + b - a]
    return np.stack([own, prev]).astype(np.int32)


def _moba_kernel(rb_ref, qt_ref, k_ref, vt_ref, km_ref, bucket_ref, o_ref,
                 bias_ref, sel_ref, m_ref, l_ref, acc_ref, *, group):
    g = pl.program_id(0)
    i = pl.program_id(1)
    bs = MOBA_BLOCK
    dh = ATT_HEAD_DIM
    nb = km_ref.shape[0]
    nq = group * bs

    @pl.when(i == 0)
    def _():
        for hh in range(group):
            for t in range(2):
                bk = bucket_ref[t]
                bias = jnp.full(bk.shape, MASK_VALUE, F32)
                for b in range(N_REL_BUCKETS):
                    bias = jnp.where(bk == b, rb_ref[b, g * group + hh] * LOG2_E, bias)
                bias_ref[t, :, hh * bs:(hh + 1) * bs] = bias

    qt = jnp.concatenate([qt_ref[hh * dh:(hh + 1) * dh, :] for hh in range(group)], axis=1)

    blk = lax.broadcasted_iota(jnp.int32, (nb, nq), 0)
    gate = jnp.where(blk < i, _dot(km_ref[...].astype(BF16), qt), -jnp.inf)
    sel = jnp.zeros((nb, nq), F32)
    for r in range(MOBA_TOPK):
        mx = jnp.max(gate, axis=0, keepdims=True)
        first = jnp.min(jnp.where(gate == mx, blk, nb), axis=0, keepdims=True)
        pick = blk == first
        sel = jnp.where(jnp.logical_and(pick, r < i), 1.0, sel)
        gate = jnp.where(pick, -jnp.inf, gate)
    sel_ref[...] = sel

    def keys(j, nblk=1):
        return k_ref[pl.ds(pl.multiple_of(j * bs, bs), nblk * bs), :]

    def values_t(j, nblk=1):
        return vt_ref[:, pl.ds(pl.multiple_of(j * bs, bs), nblk * bs)]

    OWN, PREV, FAR = 0, 1, 2
    SLAB = 32
    LOOKAHEAD = 4

    def attend(state, units):
        m_all, l_all, acc_all = state
        m_cur = [m_all[:, hh * bs:(hh + 1) * bs] for hh in range(group)]
        l_cur = [l_all[:, hh * bs:(hh + 1) * bs] for hh in range(group)]
        acc_cur = [acc_all[:, hh * bs:(hh + 1) * bs] for hh in range(group)]
        tasks = [(u, hh) for u in range(len(units)) for hh in range(group)]

        def score(t):
            u, hh = tasks[t]
            j, kinds = units[u]
            return _dot(keys(j, len(kinds)), qt[:, hh * bs:(hh + 1) * bs])

        pending = {t: score(t) for t in range(min(LOOKAHEAD, len(tasks)))}
        for t, (u, hh) in enumerate(tasks):
            if t + LOOKAHEAD < len(tasks):
                pending[t + LOOKAHEAD] = score(t + LOOKAHEAD)
            scores = pending.pop(t)
            j, kinds = units[u]
            cols = slice(hh * bs, (hh + 1) * bs)
            far_bias = rb_ref[N_REL_BUCKETS - 1, g * group + hh] * LOG2_E
            parts = []
            run_max = None
            for b, kind in enumerate(kinds):
                if kind != OWN:
                    row = jnp.where(sel_ref[pl.ds(j + b, 1), cols] > 0.0, far_bias if kind == FAR else 0.0,
                                    MASK_VALUE)
                    row = jnp.broadcast_to(row, (SLAB, bs))
                for r0 in range(0, bs, SLAB):
                    part = scores[b * bs + r0:b * bs + r0 + SLAB]
                    if kind == OWN:
                        part = part + bias_ref[0, r0:r0 + SLAB, cols]
                    elif kind == PREV:
                        part = part + bias_ref[1, r0:r0 + SLAB, cols] + row
                    else:
                        part = part + row
                    parts.append(part)
                    run_max = part if run_max is None else jnp.maximum(run_max, part)
            m_new = jnp.maximum(m_cur[hh], jnp.max(run_max, axis=0, keepdims=True))
            alpha = jnp.exp2(m_cur[hh] - m_new)
            m_slab = jnp.broadcast_to(m_new, (SLAB, bs))
            run_sum = None
            packed = []
            for part in parts:
                p = jnp.exp2(part - m_slab)
                run_sum = p if run_sum is None else run_sum + p
                packed.append(p.astype(BF16))
            m_cur[hh] = m_new
            l_cur[hh] = alpha * l_cur[hh] + jnp.sum(run_sum, axis=0, keepdims=True)
            acc_cur[hh] = alpha * acc_cur[hh] + _dot(values_t(j, len(kinds)), jnp.concatenate(packed, axis=0))
        return (jnp.concatenate(m_cur, axis=1), jnp.concatenate(l_cur, axis=1), jnp.concatenate(acc_cur, axis=1))

    def load_state():
        return m_ref[...], l_ref[...], acc_ref[...]

    def store_state(state):
        m_ref[...], l_ref[...], acc_ref[...] = state

    empty = (jnp.full((1, nq), MASK_VALUE, F32), jnp.zeros((1, nq), F32), jnp.zeros((dh, nq), F32))

    n_old = jnp.maximum(i - 1, 0)
    odd = n_old % 2 == 1

    @pl.when(i == 0)
    def _():
        store_state(attend(empty, [(i, (OWN,))]))

    @pl.when(jnp.logical_and(i >= 1, jnp.logical_not(odd)))
    def _():
        store_state(attend(empty, [(i - 1, (PREV, OWN))]))

    @pl.when(odd)
    def _():
        store_state(attend(empty, [(i - 2, (FAR, PREV, OWN))]))

    n_far = n_old - n_old % 2

    def far_units(j0, n_units):
        return [(j0 + 2 * u, (FAR, FAR)) for u in range(n_units)]

    def far_body(jj, carry):
        store_state(attend(load_state(), far_units(4 * jj, 2)))
        return carry

    lax.fori_loop(0, n_far // 4, far_body, 0)

    @pl.when(n_far % 4 == 2)
    def _():
        store_state(attend(load_state(), far_units(n_far - 2, 1)))

    out = acc_ref[...] / l_ref[...]
    for hh in range(group):
        o_ref[:, hh * dh:(hh + 1) * dh] = out[:, hh * bs:(hh + 1) * bs].T.astype(BF16)


def moba_attention(qt, k, vt, k_means, rel_bias, *, n_heads, n_kv_heads):
    dh = ATT_HEAD_DIM
    t = k.shape[0]
    bs = MOBA_BLOCK
    nb = t // bs
    group = n_heads // n_kv_heads
    nq = group * bs
    assert REL_MAX_DISTANCE <= bs
    buckets = jnp.asarray(_bucket_tiles())
    return pl.pallas_call(
        functools.partial(_moba_kernel, group=group),
        grid=(n_kv_heads, nb),
        in_specs=[
            pl.BlockSpec(memory_space=pltpu.SMEM),
            pl.BlockSpec((group * dh, bs), lambda g, i: (g, i)),
            pl.BlockSpec((t, dh), lambda g, i: (0, g)),
            pl.BlockSpec((dh, t), lambda g, i: (g, 0)),
            pl.BlockSpec((nb, dh), lambda g, i: (0, g)),
            pl.BlockSpec((2, bs, bs), lambda g, i: (0, 0, 0)),
        ],
        out_specs=pl.BlockSpec((bs, group * dh), lambda g, i: (i, g)),
        out_shape=jax.ShapeDtypeStruct((t, n_heads * dh), BF16),
        scratch_shapes=[
            pltpu.VMEM((2, bs, nq), F32),
            pltpu.VMEM((nb, nq), F32),
            pltpu.VMEM((1, nq), F32),
            pltpu.VMEM((1, nq), F32),
            pltpu.VMEM((dh, nq), F32),
        ],
        compiler_params=_params("parallel", "arbitrary"),
        name="moba_attention",
    )(rel_bias, qt, k, vt, k_means, buckets)


def kernel(x, ffn1_norm, ffn1_w_in, ffn1_w_out, mix_norm, ffn2_norm, ffn2_w_in, ffn2_w_out,
           gdn_w_in, gdn_conv_w, gdn_a_log, gdn_dt_bias, gdn_out_norm, gdn_w_out,
           kv_norm, w_kv, moba_w_q, moba_w_o, rel_bias, final_norm):
    batch, seq, d_model = x.shape
    depth = ffn1_norm.shape[0]
    n_a = gdn_w_in.shape[0]
    n_v_heads = gdn_a_log.shape[1]
    value_dim = gdn_w_out.shape[1]
    conv_dim = gdn_conv_w.shape[2]
    key_dim = (conv_dim - value_dim) // 2
    rep = value_dim // key_dim
    n_heads = moba_w_q.shape[2] // ATT_HEAD_DIM
    kv_dim = w_kv.shape[1] // 2
    n_kv_heads = kv_dim // ATT_HEAD_DIM
    bf = lambda a: a.astype(BF16)
    gdn_w_in_t = jnp.swapaxes(gdn_w_in, 1, 2)

    outs = []
    for bi in range(batch):
        h = x[bi]
        k_nat = vt = k_means = None
        for layer in range(depth):
            if layer == n_a:
                k_f32 = norm_matmul(h, kv_norm, bf(w_kv[:, :kv_dim]), tm=1024, tn=kv_dim, name="kv_k")
                vt = norm_matmul_t(h, kv_norm, bf(w_kv[:, kv_dim:].T), tm=1024, tn=kv_dim, name="kv_vt",
                                   out_dtype=BF16)
                k_means, k_nat = moba_kmeans(k_f32)
            h = ffn(h, ffn1_norm[layer], ffn1_w_in, ffn1_w_out, layer)
            if layer < n_a:
                main = conv_dim + value_dim
                n_ba = 2 * n_v_heads
                proj = norm_matmul(h, mix_norm[layer], gdn_w_in_t, layer=layer, n=main, tm=2048, tn=512,
                                   name="gdn_in")
                ba = norm_matmul(h, mix_norm[layer], gdn_w_in_t, layer=layer, n=n_ba, col_block0=main // n_ba,
                                 tm=2048, tn=n_ba, name="gdn_in_ba")
                beta, g = gdn_gates(ba, gdn_a_log[layer], gdn_dt_bias[layer])
                qkv = gdn_conv(proj, gdn_conv_w[layer], conv_dim=conv_dim, key_dim=key_dim)
                g_rows = g.T.reshape(n_v_heads, 1, seq)
                wq, l2, u, glw = gdn_prep(qkv, g, beta, g_rows, n_heads=n_v_heads, rep=rep)
                gated = gdn_scan(wq, l2, u, glw, proj, conv_dim, gdn_out_norm[layer])
                h = matmul_res(gated, bf(gdn_w_out[layer]), h, tm=512, tn=1024, name="gdn_out")
            else:
                j = layer - n_a
                qt = norm_matmul_t(h, mix_norm[layer], bf(moba_w_q[j].T), tm=1024, tn=1024, name="moba_qt",
                                   out_dtype=BF16, out_scale=MOBA_Q_SCALE)
                att = moba_attention(qt, k_nat, vt, k_means, rel_bias,
                                     n_heads=n_heads, n_kv_heads=n_kv_heads)
                h = matmul_res(att, bf(moba_w_o[j]), h, tm=1024, tn=1024, name="moba_out")
            last = layer == depth - 1
            h = ffn(h, ffn2_norm[layer], ffn2_w_in, ffn2_w_out, layer, final_norm if last else None)
        outs.append(h)
    return jnp.stack(outs)
```

```python
import functools
import math

import numpy as np
import jax
import jax.numpy as jnp
from jax import lax
from jax.experimental import pallas as pl
from jax.experimental.pallas import tpu as pltpu

F32 = jnp.float32
BF16 = jnp.bfloat16

RMS_EPS = 1e-6
GDN_HEAD_DIM = 128
GDN_CHUNK = 64
GDN_CONV_WIDTH = 4
ATT_HEAD_DIM = 128
MOBA_BLOCK = 256
MOBA_TOPK = 3
N_REL_BUCKETS = 32
REL_MAX_DISTANCE = 128
MASK_VALUE = -1e30
LOG2_E = 1.4426950408889634
MOBA_Q_SCALE = ATT_HEAD_DIM ** -0.5 * LOG2_E
MOBA_L_ROWS = 16

VMEM_LIMIT_BYTES = 58 * 1024 * 1024


def _params(*sem):
    return pltpu.CompilerParams(dimension_semantics=sem, vmem_limit_bytes=VMEM_LIMIT_BYTES)


def _rms_rows(x, g):
    ms = jnp.mean(x * x, axis=-1, keepdims=True)
    return x * lax.rsqrt(ms + RMS_EPS) * g


def _silu(x):
    return x * jax.nn.sigmoid(x)


def _dot(a, b):
    return jnp.dot(a, b, preferred_element_type=F32)


def _dot_nt(a, b):
    return lax.dot_general(a, b, (((1,), (1,)), ((), ())), preferred_element_type=F32)


def _norm_matmul_kernel(x_ref, g_ref, w_ref, o_ref, xn_ref, *, w_rows_are_outputs):
    @pl.when(pl.program_id(1) == 0)
    def _():
        xn_ref[...] = _rms_rows(x_ref[...], g_ref[...]).astype(BF16)

    w = w_ref[...].astype(BF16)
    y = _dot_nt(xn_ref[...], w) if w_rows_are_outputs else _dot(xn_ref[...], w)
    o_ref[...] = y.astype(o_ref.dtype)


def _row_block(tm, d):
    return pl.BlockSpec((tm, d), lambda i, j: (i, 0), pipeline_mode=pl.Buffered(1))


def norm_matmul(x, g, w, *, tm, tn, name, layer=None, col_block0=0, n=None):
    t, d = x.shape
    if layer is None:
        n = w.shape[-1] if n is None else n
        w_spec = pl.BlockSpec((d, tn), lambda i, j: (0, col_block0 + j))
    else:
        n = w.shape[1] if n is None else n
        w_spec = pl.BlockSpec((None, tn, d), lambda i, j: (layer, col_block0 + j, 0))
    return pl.pallas_call(
        functools.partial(_norm_matmul_kernel, w_rows_are_outputs=layer is not None),
        grid=(t // tm, n // tn),
        in_specs=[_row_block(tm, d), pl.BlockSpec((1, d), lambda i, j: (0, 0)), w_spec],
        out_specs=pl.BlockSpec((tm, tn), lambda i, j: (i, j)),
        out_shape=jax.ShapeDtypeStruct((t, n), F32),
        scratch_shapes=[pltpu.VMEM((tm, d), BF16)],
        compiler_params=_params("parallel", "arbitrary"),
        name=name,
    )(x, g.reshape(1, d), w)


def _norm_matmul_t_kernel(x_ref, g_ref, wt_ref, o_ref, xn_ref, *, out_scale):
    @pl.when(pl.program_id(1) == 0)
    def _():
        xn_ref[...] = _rms_rows(x_ref[...], g_ref[...]).astype(BF16)

    y = _dot_nt(wt_ref[...], xn_ref[...])
    if out_scale != 1.0:
        y = y * out_scale
    o_ref[...] = y.astype(o_ref.dtype)


def norm_matmul_t(x, g, wt, *, tm, tn, name, out_dtype=F32, out_scale=1.0):
    t, d = x.shape
    n = wt.shape[0]
    return pl.pallas_call(
        functools.partial(_norm_matmul_t_kernel, out_scale=out_scale),
        grid=(t // tm, n // tn),
        in_specs=[
            _row_block(tm, d),
            pl.BlockSpec((1, d), lambda i, j: (0, 0)),
            pl.BlockSpec((tn, d), lambda i, j: (j, 0)),
        ],
        out_specs=pl.BlockSpec((tn, tm), lambda i, j: (j, i)),
        out_shape=jax.ShapeDtypeStruct((n, t), out_dtype),
        scratch_shapes=[pltpu.VMEM((tm, d), BF16)],
        compiler_params=_params("parallel", "arbitrary"),
        name=name,
    )(x, g.reshape(1, d), wt)


def _ffn_kernel(h_ref, g_ref, wg_ref, wu_ref, wo_ref, fg_ref, o_ref, xn_ref, *, final_norm):
    j = pl.program_id(1)
    tm = h_ref.shape[0]
    strip = min(tm, 256)

    @pl.when(j == 0)
    def _():
        for r0 in range(0, tm, strip):
            r = slice(r0, r0 + strip)
            xn_ref[r, :] = _rms_rows(h_ref[r, :], g_ref[...]).astype(BF16)
            o_ref[r, :] = jnp.zeros((strip, o_ref.shape[1]), F32)

    xn = xn_ref[...]
    tf = wg_ref.shape[1]
    half = min(tf, 256)
    acts = []
    for c0 in range(0, tf, half):
        gate = _dot(xn, wg_ref[:, c0:c0 + half].astype(BF16))
        up = _dot(xn, wu_ref[:, c0:c0 + half].astype(BF16))
        acts.append((_silu(gate) * up).astype(BF16))
    act = acts[0] if len(acts) == 1 else jnp.concatenate(acts, axis=1)
    o_ref[...] += _dot(act, wo_ref[...].astype(BF16))

    @pl.when(j == pl.num_programs(1) - 1)
    def _():
        for r0 in range(0, tm, strip):
            r = slice(r0, r0 + strip)
            y = h_ref[r, :] + 0.5 * o_ref[r, :]
            if final_norm:
                y = _rms_rows(y, fg_ref[...])
            o_ref[r, :] = y


def ffn(h, g, w_in, w_out, layer, final_g=None, *, tm=1024, tf=512):
    t, d = h.shape
    f = w_out.shape[1]
    nf = f // tf
    final_norm = final_g is not None
    fg = (final_g if final_norm else g).reshape(1, d)
    return pl.pallas_call(
        functools.partial(_ffn_kernel, final_norm=final_norm),
        grid=(t // tm, nf),
        in_specs=[
            _row_block(tm, d),
            pl.BlockSpec((1, d), lambda i, j: (0, 0)),
            pl.BlockSpec((None, d, tf), lambda i, j: (layer, 0, j)),
            pl.BlockSpec((None, d, tf), lambda i, j: (layer, 0, j + nf)),
            pl.BlockSpec((None, tf, d), lambda i, j: (layer, j, 0)),
            pl.BlockSpec((1, d), lambda i, j: (0, 0)),
        ],
        out_specs=_row_block(tm, d),
        out_shape=jax.ShapeDtypeStruct((t, d), F32),
        scratch_shapes=[pltpu.VMEM((tm, d), BF16)],
        compiler_params=_params("parallel", "arbitrary"),
        name="ffn",
    )(h, g.reshape(1, d), w_in, w_in, w_out, fg)


def _matmul_res_kernel(a_ref, w_ref, r_ref, o_ref):
    o_ref[...] = r_ref[...] + _dot(a_ref[...], w_ref[...])


def matmul_res(a, w, r, *, tm, tn, name):
    t, k = a.shape
    n = w.shape[1]
    return pl.pallas_call(
        _matmul_res_kernel,
        grid=(t // tm, n // tn),
        in_specs=[
            pl.BlockSpec((tm, k), lambda i, j: (i, 0)),
            pl.BlockSpec((k, tn), lambda i, j: (0, j)),
            pl.BlockSpec((tm, tn), lambda i, j: (i, j)),
        ],
        out_specs=pl.BlockSpec((tm, tn), lambda i, j: (i, j)),
        out_shape=jax.ShapeDtypeStruct((t, n), F32),
        compiler_params=_params("parallel", "arbitrary"),
        name=name,
    )(a, w, r)


def _conv_kernel(prev_ref, cur_ref, w_ref, o_ref, xs_ref, *, n_norm_blocks, strip):
    i = pl.program_id(0)
    j = pl.program_id(1)
    tt, tc = cur_ref.shape
    halo = prev_ref.shape[0]
    xs_ref[0:halo, :] = jnp.where(i == 0, 0.0, prev_ref[...])
    xs_ref[halo:, :] = cur_ref[...]
    w = w_ref[...]

    def conv_strip(r0):
        y = w[GDN_CONV_WIDTH - 1:GDN_CONV_WIDTH] * xs_ref[halo + r0:halo + r0 + strip, :]
        for s in range(1, GDN_CONV_WIDTH):
            tap = GDN_CONV_WIDTH - 1 - s
            y = y + w[tap:tap + 1] * xs_ref[halo + r0 - s:halo + r0 - s + strip, :]
        return _silu(y)

    @pl.when(j < n_norm_blocks)
    def _():
        for r0 in range(0, tt, strip):
            y = conv_strip(r0)
            for hh in range(tc // GDN_HEAD_DIM):
                seg = y[:, hh * GDN_HEAD_DIM:(hh + 1) * GDN_HEAD_DIM]
                ss = jnp.sum(seg * seg, axis=-1, keepdims=True)
                o_ref[r0:r0 + strip, hh * GDN_HEAD_DIM:(hh + 1) * GDN_HEAD_DIM] = seg * lax.rsqrt(ss + RMS_EPS)

    @pl.when(j >= n_norm_blocks)
    def _():
        for r0 in range(0, tt, strip):
            o_ref[r0:r0 + strip, :] = conv_strip(r0)


def gdn_conv(proj, conv_w, *, conv_dim, key_dim, tt=1024, tc=1024, strip=16):
    t = proj.shape[0]
    halo = 8
    assert halo >= GDN_CONV_WIDTH - 1
    return pl.pallas_call(
        functools.partial(_conv_kernel, n_norm_blocks=2 * key_dim // tc, strip=strip),
        grid=(t // tt, conv_dim // tc),
        in_specs=[
            pl.BlockSpec((halo, tc), lambda i, j: (jnp.maximum(i * (tt // halo) - 1, 0), j)),
            pl.BlockSpec((tt, tc), lambda i, j: (i, j)),
            pl.BlockSpec((GDN_CONV_WIDTH, tc), lambda i, j: (0, j)),
        ],
        out_specs=pl.BlockSpec((tt, tc), lambda i, j: (i, j)),
        out_shape=jax.ShapeDtypeStruct((t, conv_dim), F32),
        scratch_shapes=[pltpu.VMEM((tt + halo, tc), F32)],
        compiler_params=_params("parallel", "parallel"),
        name="gdn_conv",
    )(proj, proj, conv_w)


def _gates_kernel(ba_ref, alog_ref, dtb_ref, beta_ref, g_ref):
    nh = beta_ref.shape[1]
    ba = ba_ref[...]
    beta_ref[...] = jax.nn.sigmoid(ba[:, :nh])
    z = ba[:, nh:] + dtb_ref[...]
    softplus = jnp.maximum(z, 0.0) + jnp.log1p(jnp.exp(-jnp.abs(z)))
    g_ref[...] = -jnp.exp(alog_ref[...]) * softplus


def gdn_gates(ba, a_log, dt_bias, *, tt=1024):
    t, two_h = ba.shape
    nh = two_h // 2
    return pl.pallas_call(
        _gates_kernel,
        grid=(t // tt,),
        in_specs=[
            pl.BlockSpec((tt, two_h), lambda i: (i, 0)),
            pl.BlockSpec((1, nh), lambda i: (0, 0)),
            pl.BlockSpec((1, nh), lambda i: (0, 0)),
        ],
        out_specs=[pl.BlockSpec((tt, nh), lambda i: (i, 0)), pl.BlockSpec((tt, nh), lambda i: (i, 0))],
        out_shape=[jax.ShapeDtypeStruct((t, nh), F32), jax.ShapeDtypeStruct((t, nh), F32)],
        compiler_params=_params("parallel"),
        name="gdn_gates",
    )(ba, a_log.reshape(1, nh), dt_bias.reshape(1, nh))


def _prep_kernel(q_ref, k_ref, v_ref, gsel_ref, bsel_ref, grow_ref, wq_ref, l2_ref, u_ref, glw_ref):
    kh = pl.program_id(0)
    tp, nh = gsel_ref.shape
    c = GDN_CHUNK
    dh = GDN_HEAD_DIM
    scale = dh ** -0.5
    lane_h = lax.broadcasted_iota(jnp.int32, (tp, nh), 1)

    def column(ref, head):
        return jnp.sum(jnp.where(lane_h == head, ref[...], 0.0), axis=1, keepdims=True)

    gcol = [column(gsel_ref, 2 * kh + hh) for hh in range(2)]
    bcol = [column(bsel_ref, 2 * kh + hh) for hh in range(2)]
    grow = [grow_ref[hh] for hh in range(2)]

    ii = lax.broadcasted_iota(jnp.int32, (c, 2 * c), 0)
    ll = lax.broadcasted_iota(jnp.int32, (c, 2 * c), 1)
    jj = jnp.bitwise_and(ll, c - 1)
    head_a = ll < c
    incl = ii >= jj
    strict = ii > jj
    wi = lax.broadcasted_iota(jnp.int32, (c, 4 * c), 0)
    wl = lax.broadcasted_iota(jnp.int32, (c, 4 * c), 1)
    eye_w = jnp.where(wi == jnp.bitwise_and(wl, c - 1), 1.0, 0.0).astype(F32)
    br = lax.broadcasted_iota(jnp.int32, (4 * c, 4 * c), 0)
    bc = lax.broadcasted_iota(jnp.int32, (4 * c, 4 * c), 1)
    same_block = (br // c) == (bc // c)
    strict_block = jnp.logical_and(same_block, br != bc)

    def block_diag(packed):
        return jnp.concatenate([packed] * 4, axis=0)

    def packed_matmul(a, b):
        bd = jnp.where(same_block, block_diag(b), 0.0).astype(BF16)
        return _dot(a.astype(BF16), bd)

    n_half = tp // (2 * c)
    lmats = [[] for _ in range(n_half)]
    rhs_blocks = [[] for _ in range(n_half)]
    for half in range(n_half):
        for y in range(2):
            n = 2 * half + y
            r = slice(n * c, (n + 1) * c)
            q = q_ref[r, :] * scale
            k = k_ref[r, :]
            g_pair = jnp.where(head_a, gcol[0][r], gcol[1][r])
            b_pair = jnp.where(head_a, bcol[0][r], bcol[1][r])
            grow_pair = jnp.concatenate([grow[0][:, r], grow[1][:, r]], axis=1)
            tri = jnp.where(incl, grow_pair, 0.0)
            gc = [jnp.sum(jnp.where(head_a, tri, 0.0), axis=1, keepdims=True),
                  jnp.sum(jnp.where(head_a, 0.0, tri), axis=1, keepdims=True)]
            gc_col = jnp.where(head_a, gc[0], gc[1])
            gc_row = jnp.sum(jnp.where(ii <= jj, g_pair, 0.0), axis=0, keepdims=True)
            decay = jnp.where(incl, jnp.exp(jnp.where(incl, gc_col - gc_row, 0.0)), 0.0)
            kb16 = k.astype(BF16)
            gram = _dot_nt(jnp.concatenate([q.astype(BF16), kb16], axis=0),
                           jnp.concatenate([kb16, kb16], axis=0))
            l2_ref[n, 0:c, :] = jnp.where(incl, gram[:c] * decay, 0.0).astype(BF16)
            lmats[half].append(jnp.where(strict, gram[c:] * b_pair * decay, 0.0))
            kups = []
            for hh in range(2):
                g_last = jnp.sum(gcol[hh][r], axis=0, keepdims=True)
                eg = jnp.exp(gc[hh])
                beta = bcol[hh][r]
                kups.append(k * jnp.exp(g_last - gc[hh]))
                wq_ref[n, c:2 * c, hh * dh:(hh + 1) * dh] = (q * eg).astype(BF16)
                glw_ref[n, :, hh * dh:(hh + 1) * dh] = jnp.broadcast_to(jnp.exp(g_last), (8, dh))
                rhs_blocks[half].append(
                    jnp.concatenate([v_ref[r, hh * dh:(hh + 1) * dh] * beta, k * (beta * eg)], axis=1))
            l2_ref[n, c:c + dh, :] = jnp.concatenate(kups, axis=0).T.astype(BF16)

    ps = [-jnp.concatenate(lm, axis=1) for lm in lmats]
    tinvs = [eye_w + p for p in ps]
    m = 1
    while 2 * m < c:
        ps = [packed_matmul(p, p) for p in ps]
        tinvs = [t + packed_matmul(t, p) for t, p in zip(tinvs, ps)]
        m *= 2
    for half in range(n_half):
        rhs_all = jnp.concatenate(rhs_blocks[half], axis=0)
        t_off = jnp.where(strict_block, block_diag(tinvs[half]), 0.0).astype(BF16)
        uw = rhs_all + _dot(t_off, rhs_all.astype(BF16))
        for y in range(2):
            n = 2 * half + y
            for hh in range(2):
                blk = uw[(2 * y + hh) * c:(2 * y + hh + 1) * c]
                u_ref[n * c:(n + 1) * c, hh * dh:(hh + 1) * dh] = blk[:, :dh]
                wq_ref[n, 0:c, hh * dh:(hh + 1) * dh] = blk[:, dh:].astype(BF16)


def gdn_prep(qkv, g, beta, g_rows, *, n_heads, rep, tp=2048):
    assert rep == 2
    t = qkv.shape[0]
    dh = GDN_HEAD_DIM
    c = GDN_CHUNK
    nk = n_heads // rep
    nc = tp // c
    return pl.pallas_call(
        _prep_kernel,
        grid=(nk, t // tp),
        in_specs=[
            pl.BlockSpec((tp, dh), lambda h, i: (i, h)),
            pl.BlockSpec((tp, dh), lambda h, i: (i, nk + h)),
            pl.BlockSpec((tp, 2 * dh), lambda h, i: (i, nk + h)),
            pl.BlockSpec((tp, n_heads), lambda h, i: (i, 0)),
            pl.BlockSpec((tp, n_heads), lambda h, i: (i, 0)),
            pl.BlockSpec((2, 1, tp), lambda h, i: (h, 0, i)),
        ],
        out_specs=[
            pl.BlockSpec((None, nc, 2 * c, 2 * dh), lambda h, i: (h, i, 0, 0)),
            pl.BlockSpec((None, nc, c + dh, 2 * c), lambda h, i: (h, i, 0, 0)),
            pl.BlockSpec((None, tp, 2 * dh), lambda h, i: (h, i, 0)),
            pl.BlockSpec((None, nc, 8, 2 * dh), lambda h, i: (h, i, 0, 0)),
        ],
        out_shape=[
            jax.ShapeDtypeStruct((nk, t // c, 2 * c, 2 * dh), BF16),
            jax.ShapeDtypeStruct((nk, t // c, c + dh, 2 * c), BF16),
            jax.ShapeDtypeStruct((nk, t, 2 * dh), F32),
            jax.ShapeDtypeStruct((nk, t // c, 8, 2 * dh), F32),
        ],
        compiler_params=_params("parallel", "parallel"),
        name="gdn_prep",
    )(qkv, qkv, qkv, g, beta, g_rows)


def _scan_kernel(wq_ref, l2_ref, u_ref, glw_ref, z_ref, gn_ref, o_ref, s_ref):
    pb, nc = wq_ref.shape[0], wq_ref.shape[1]
    gn = gn_ref[...]
    c = GDN_CHUNK
    dh = GDN_HEAD_DIM

    @pl.when(pl.program_id(1) == 0)
    def _():
        s_ref[...] = jnp.zeros_like(s_ref)

    def pair_diag(x):
        z = jnp.zeros((x.shape[0], dh), x.dtype)
        return jnp.concatenate([jnp.concatenate([x[:, :dh], z], axis=1),
                                jnp.concatenate([z, x[:, dh:]], axis=1)], axis=0)

    states = [s_ref[p] for p in range(pb)]
    for n in range(nc):
        r = slice(n * c, (n + 1) * c)
        for p in range(pb):
            ws = _dot(wq_ref[p, n], pair_diag(states[p].astype(BF16)))
            v_new = (u_ref[p, r, :] - ws[:c]).astype(BF16)
            upd = _dot(l2_ref[p, n], pair_diag(v_new))
            o = ws[c:] + upd[:c]
            states[p] = states[p] * glw_ref[p, n, 0:1, :] + upd[c:]
            for hh in range(2):
                cols = slice((2 * p + hh) * dh, (2 * p + hh + 1) * dh)
                gated = _rms_rows(o[:, hh * dh:(hh + 1) * dh], gn) * _silu(z_ref[r, cols])
                o_ref[r, cols] = gated.astype(BF16)
    for p in range(pb):
        s_ref[p] = states[p]


def gdn_scan(wq, l2, u, glw, proj, z_col0, out_norm, *, pb=16, tp=256):
    nk, t, two_dh = u.shape
    c = GDN_CHUNK
    nc = tp // c
    width = pb * two_dh
    blk4 = lambda a: pl.BlockSpec((pb, nc) + a.shape[2:], lambda h, i: (h, i, 0, 0))
    return pl.pallas_call(
        _scan_kernel,
        grid=(nk // pb, t // tp),
        in_specs=[blk4(wq), blk4(l2), pl.BlockSpec((pb, tp, two_dh), lambda h, i: (h, i, 0)), blk4(glw),
                  pl.BlockSpec((tp, width), lambda h, i: (i, z_col0 // width + h)),
                  pl.BlockSpec((1, GDN_HEAD_DIM), lambda h, i: (0, 0))],
        out_specs=pl.BlockSpec((tp, width), lambda h, i: (i, h)),
        out_shape=jax.ShapeDtypeStruct((t, nk * two_dh), BF16),
        scratch_shapes=[pltpu.VMEM((pb, two_dh // 2, two_dh), F32)],
        compiler_params=_params("parallel", "arbitrary"),
        name="gdn_scan",
    )(wq, l2, u, glw, proj, out_norm.reshape(1, GDN_HEAD_DIM))


def _kmeans_kernel(k_ref, mean_ref, kb_ref):
    k = k_ref[...]
    mean_ref[...] = jnp.mean(k, axis=0, keepdims=True)
    kb_ref[...] = k.astype(BF16)


def moba_kmeans(k):
    t, kd = k.shape
    nb = t // MOBA_BLOCK
    means, kb = pl.pallas_call(
        _kmeans_kernel,
        grid=(nb,),
        in_specs=[pl.BlockSpec((MOBA_BLOCK, kd), lambda i: (i, 0))],
        out_specs=[pl.BlockSpec((None, 1, kd), lambda i: (i, 0, 0)),
                   pl.BlockSpec((MOBA_BLOCK, kd), lambda i: (i, 0))],
        out_shape=[jax.ShapeDtypeStruct((nb, 1, kd), F32), jax.ShapeDtypeStruct((t, kd), BF16)],
        compiler_params=_params("parallel"),
        name="moba_kmeans",
    )(k)
    return means.reshape(nb, kd), kb


def _rel_bucket_table(n_dist):
    n = np.arange(n_dist)
    max_exact = N_REL_BUCKETS // 2
    ratio = np.log(np.maximum(n, max_exact).astype(np.float32) / np.float32(max_exact)) \
        / np.float32(math.log(REL_MAX_DISTANCE / max_exact))
    large = np.minimum(max_exact + (ratio.astype(np.float32) * (N_REL_BUCKETS - max_exact)).astype(np.int32),
                       N_REL_BUCKETS - 1)
    return np.where(n < max_exact, n, large).astype(np.int32)


def _bucket_tiles():
    table = _rel_bucket_table(2 * MOBA_BLOCK)
    a = np.arange(MOBA_BLOCK)[:, None]
    b = np.arange(MOBA_BLOCK)[None, :]
    d_own = b - a
    own = np.where(d_own >= 0, table[np.maximum(d_own, 0)], -1)
    prev = table[MOBA_BLOCK + b - a]
    return np.stack([own, prev]).astype(np.int32)


def _moba_kernel(rb_ref, qt_ref, k_ref, vt_ref, km_ref, bucket_ref, o_ref,
                 bias_ref, sel_ref, m_ref, acc_ref, *, group):
    g = pl.program_id(0)
    i = pl.program_id(1)
    bs = MOBA_BLOCK
    dh = ATT_HEAD_DIM
    nb = km_ref.shape[0]
    nq = group * bs

    @pl.when(i == 0)
    def _():
        for hh in range(group):
            for t in range(2):
                bk = bucket_ref[t]
                bias = jnp.full(bk.shape, MASK_VALUE, F32)
                for b in range(N_REL_BUCKETS):
                    bias = jnp.where(bk == b, rb_ref[b, g * group + hh] * LOG2_E, bias)
                bias_ref[t, :, hh * bs:(hh + 1) * bs] = bias

    qt = jnp.concatenate([qt_ref[hh * dh:(hh + 1) * dh, :] for hh in range(group)], axis=1)

    blk = lax.broadcasted_iota(jnp.int32, (nb, nq), 0)
    gate = jnp.where(blk < i, _dot(km_ref[...].astype(BF16), qt), -jnp.inf)
    sel = jnp.zeros((nb, nq), F32)
    for r in range(MOBA_TOPK):
        mx = jnp.max(gate, axis=0, keepdims=True)
        first = jnp.min(jnp.where(gate == mx, blk, nb), axis=0, keepdims=True)
        pick = blk == first
        sel = jnp.where(jnp.logical_and(pick, r < i), 1.0, sel)
        gate = jnp.where(pick, -jnp.inf, gate)
    sel_ref[...] = sel

    def keys(j, nblk=1):
        return k_ref[pl.ds(pl.multiple_of(j * bs, bs), nblk * bs), :]

    def values_t(j, nblk=1):
        return vt_ref[:, pl.ds(pl.multiple_of(j * bs, bs), nblk * bs)]

    OWN, PREV, FAR = 0, 1, 2
    SLAB = 32
    LOOKAHEAD = 4
    L_ROWS = MOBA_L_ROWS

    def attend(state, units):
        m_all, acc_all = state
        m_cur = [m_all[:, hh * bs:(hh + 1) * bs] for hh in range(group)]
        acc_cur = [acc_all[:, hh * bs:(hh + 1) * bs] for hh in range(group)]
        tasks = [(u, hh) for u in range(len(units)) for hh in range(group)]

        def score(t):
            u, hh = tasks[t]
            j, kinds = units[u]
            return _dot(keys(j, len(kinds)), qt[:, hh * bs:(hh + 1) * bs])

        pending = {t: score(t) for t in range(min(LOOKAHEAD, len(tasks)))}
        for t, (u, hh) in enumerate(tasks):
            if t + LOOKAHEAD < len(tasks):
                pending[t + LOOKAHEAD] = score(t + LOOKAHEAD)
            scores = pending.pop(t)
            j, kinds = units[u]
            cols = slice(hh * bs, (hh + 1) * bs)
            far_bias = rb_ref[N_REL_BUCKETS - 1, g * group + hh] * LOG2_E
            parts = []
            run_max = None
            for b, kind in enumerate(kinds):
                if kind != OWN:
                    row = jnp.where(sel_ref[pl.ds(j + b, 1), cols] > 0.0, far_bias if kind == FAR else 0.0,
                                    MASK_VALUE)
                    row = jnp.broadcast_to(row, (SLAB, bs))
                for r0 in range(0, bs, SLAB):
                    part = scores[b * bs + r0:b * bs + r0 + SLAB]
                    if kind == OWN:
                        part = part + bias_ref[0, r0:r0 + SLAB, cols]
                    elif kind == PREV:
                        part = part + bias_ref[1, r0:r0 + SLAB, cols] + row
                    else:
                        part = part + row
                    parts.append(part)
                    run_max = part if run_max is None else jnp.maximum(run_max, part)
            m_new = jnp.maximum(m_cur[hh], jnp.max(run_max, axis=0, keepdims=True))
            alpha = jnp.exp2(m_cur[hh] - m_new)
            m_slab = jnp.broadcast_to(m_new, (SLAB, bs))
            packed = [jnp.exp2(part - m_slab).astype(BF16) for part in parts]
            vt_ones = jnp.concatenate([values_t(j, len(kinds)), jnp.ones((L_ROWS, len(kinds) * bs), BF16)], axis=0)
            m_cur[hh] = m_new
            acc_cur[hh] = alpha * acc_cur[hh] + _dot(vt_ones, jnp.concatenate(packed, axis=0))
        return jnp.concatenate(m_cur, axis=1), jnp.concatenate(acc_cur, axis=1)

    def load_state():
        return m_ref[...], acc_ref[...]

    def store_state(state):
        m_ref[...], acc_ref[...] = state

    empty = (jnp.full((1, nq), MASK_VALUE, F32), jnp.zeros((dh + L_ROWS, nq), F32))

    n_old = jnp.maximum(i - 1, 0)
    odd = n_old % 2 == 1

    @pl.when(i == 0)
    def _():
        store_state(attend(empty, [(i, (OWN,))]))

    @pl.when(jnp.logical_and(i >= 1, jnp.logical_not(odd)))
    def _():
        store_state(attend(empty, [(i - 1, (PREV, OWN))]))

    @pl.when(odd)
    def _():
        store_state(attend(empty, [(i - 2, (FAR, PREV, OWN))]))

    n_far = n_old - n_old % 2

    def far_units(j0, n_units):
        return [(j0 + 2 * u, (FAR, FAR)) for u in range(n_units)]

    def far_body(jj, carry):
        store_state(attend(load_state(), far_units(4 * jj, 2)))
        return carry

    lax.fori_loop(0, n_far // 4, far_body, 0)

    @pl.when(n_far % 4 == 2)
    def _():
        store_state(attend(load_state(), far_units(n_far - 2, 1)))

    out = acc_ref[0:dh, :] / acc_ref[dh:dh + 1, :]
    for hh in range(group):
        o_ref[:, hh * dh:(hh + 1) * dh] = out[:, hh * bs:(hh + 1) * bs].T.astype(BF16)


def moba_attention(qt, k, vt, k_means, rel_bias, *, n_heads, n_kv_heads):
    dh = ATT_HEAD_DIM
    t = k.shape[0]
    bs = MOBA_BLOCK
    nb = t // bs
    group = n_heads // n_kv_heads
    nq = group * bs
    assert REL_MAX_DISTANCE <= bs
    buckets = jnp.asarray(_bucket_tiles())
    return pl.pallas_call(
        functools.partial(_moba_kernel, group=group),
        grid=(n_kv_heads, nb),
        in_specs=[
            pl.BlockSpec(memory_space=pltpu.SMEM),
            pl.BlockSpec((group * dh, bs), lambda g, i: (g, i)),
            pl.BlockSpec((t, dh), lambda g, i: (0, g)),
            pl.BlockSpec((dh, t), lambda g, i: (g, 0)),
            pl.BlockSpec((nb, dh), lambda g, i: (0, g)),
            pl.BlockSpec((2, bs, bs), lambda g, i: (0, 0, 0)),
        ],
        out_specs=pl.BlockSpec((bs, group * dh), lambda g, i: (i, g)),
        out_shape=jax.ShapeDtypeStruct((t, n_heads * dh), BF16),
        scratch_shapes=[
            pltpu.VMEM((2, bs, nq), F32),
            pltpu.VMEM((nb, nq), F32),
            pltpu.VMEM((1, nq), F32),
            pltpu.VMEM((dh + MOBA_L_ROWS, nq), F32),
        ],
        compiler_params=_params("parallel", "arbitrary"),
        name="moba_attention",
    )(rel_bias, qt, k, vt, k_means, buckets)


def kernel(x, ffn1_norm, ffn1_w_in, ffn1_w_out, mix_norm, ffn2_norm, ffn2_w_in, ffn2_w_out,
           gdn_w_in, gdn_conv_w, gdn_a_log, gdn_dt_bias, gdn_out_norm, gdn_w_out,
           kv_norm, w_kv, moba_w_q, moba_w_o, rel_bias, final_norm):
    batch, seq, d_model = x.shape
    depth = ffn1_norm.shape[0]
    n_a = gdn_w_in.shape[0]
    n_v_heads = gdn_a_log.shape[1]
    value_dim = gdn_w_out.shape[1]
    conv_dim = gdn_conv_w.shape[2]
    key_dim = (conv_dim - value_dim) // 2
    rep = value_dim // key_dim
    n_heads = moba_w_q.shape[2] // ATT_HEAD_DIM
    kv_dim = w_kv.shape[1] // 2
    n_kv_heads = kv_dim // ATT_HEAD_DIM
    bf = lambda a: a.astype(BF16)
    gdn_w_in_t = jnp.swapaxes(gdn_w_in, 1, 2)

    outs = []
    for bi in range(batch):
        h = x[bi]
        k_nat = vt = k_means = None
        for layer in range(depth):
            if layer == n_a:
                k_f32 = norm_matmul(h, kv_norm, bf(w_kv[:, :kv_dim]), tm=1024, tn=kv_dim, name="kv_k")
                vt = norm_matmul_t(h, kv_norm, bf(w_kv[:, kv_dim:].T), tm=1024, tn=kv_dim, name="kv_vt",
                                   out_dtype=BF16)
                k_means, k_nat = moba_kmeans(k_f32)
            h = ffn(h, ffn1_norm[layer], ffn1_w_in, ffn1_w_out, layer)
            if layer < n_a:
                main = conv_dim + value_dim
                n_ba = 2 * n_v_heads
                proj = norm_matmul(h, mix_norm[layer], gdn_w_in_t, layer=layer, n=main, tm=2048, tn=512,
                                   name="gdn_in")
                ba = norm_matmul(h, mix_norm[layer], gdn_w_in_t, layer=layer, n=n_ba, col_block0=main // n_ba,
                                 tm=2048, tn=n_ba, name="gdn_in_ba")
                beta, g = gdn_gates(ba, gdn_a_log[layer], gdn_dt_bias[layer])
                qkv = gdn_conv(proj, gdn_conv_w[layer], conv_dim=conv_dim, key_dim=key_dim)
                g_rows = g.T.reshape(n_v_heads, 1, seq)
                wq, l2, u, glw = gdn_prep(qkv, g, beta, g_rows, n_heads=n_v_heads, rep=rep)
                gated = gdn_scan(wq, l2, u, glw, proj, conv_dim, gdn_out_norm[layer])
                h = matmul_res(gated, bf(gdn_w_out[layer]), h, tm=512, tn=1024, name="gdn_out")
            else:
                j = layer - n_a
                qt = norm_matmul_t(h, mix_norm[layer], bf(moba_w_q[j].T), tm=1024, tn=1024, name="moba_qt",
                                   out_dtype=BF16, out_scale=MOBA_Q_SCALE)
                att = moba_attention(qt, k_nat, vt, k_means, rel_bias,
                                     n_heads=n_heads, n_kv_heads=n_kv_heads)
                h = matmul_res(att, bf(moba_w_o[j]), h, tm=1024, tn=1024, name="moba_out")
            last = layer == depth - 1
            h = ffn(h, ffn2_norm[layer], ffn2_w_in, ffn2_w_out, layer, final_norm if last else None)
        outs.append(h)
    return jnp.stack(outs)
```

```python
import functools
import math

import numpy as np
import jax
import jax.numpy as jnp
from jax import lax
from jax.experimental import pallas as pl
from jax.experimental.pallas import tpu as pltpu

F32 = jnp.float32
BF16 = jnp.bfloat16

RMS_EPS = 1e-6
GDN_HEAD_DIM = 128
GDN_CHUNK = 64
GDN_CONV_WIDTH = 4
ATT_HEAD_DIM = 128
MOBA_BLOCK = 256
MOBA_TOPK = 3
N_REL_BUCKETS = 32
REL_MAX_DISTANCE = 128
MASK_VALUE = -1e30
LOG2_E = 1.4426950408889634
MOBA_Q_SCALE = ATT_HEAD_DIM ** -0.5 * LOG2_E
MOBA_L_ROWS = 16

VMEM_LIMIT_BYTES = 58 * 1024 * 1024


def _params(*sem):
    return pltpu.CompilerParams(dimension_semantics=sem, vmem_limit_bytes=VMEM_LIMIT_BYTES)


def _rms_rows(x, g):
    ms = jnp.mean(x * x, axis=-1, keepdims=True)
    return x * lax.rsqrt(ms + RMS_EPS) * g


def _silu(x):
    return x * jax.nn.sigmoid(x)


def _dot(a, b):
    return jnp.dot(a, b, preferred_element_type=F32)


def _dot_nt(a, b):
    return lax.dot_general(a, b, (((1,), (1,)), ((), ())), preferred_element_type=F32)


def _row_block(tm, d):
    return pl.BlockSpec((tm, d), lambda i, j: (i, 0), pipeline_mode=pl.Buffered(1))


def _gdn_in_kernel(x_ref, g_ref, w_ref, wt_ref, o_ref, ot_ref, xn_ref):
    @pl.when(pl.program_id(1) == 0)
    def _():
        xn_ref[...] = _rms_rows(x_ref[...], g_ref[...]).astype(BF16)
        ot_ref[...] = _dot_nt(xn_ref[...], wt_ref[...].astype(BF16))

    o_ref[...] = _dot_nt(xn_ref[...], w_ref[...].astype(BF16))


def gdn_in_proj(x, g, w_t, layer, *, n_main, n_tail, tm, tn):
    t, d = x.shape
    assert n_main % tn == 0 and n_main % n_tail == 0
    return pl.pallas_call(
        _gdn_in_kernel,
        grid=(t // tm, n_main // tn),
        in_specs=[
            _row_block(tm, d),
            pl.BlockSpec((1, d), lambda i, j: (0, 0)),
            pl.BlockSpec((None, tn, d), lambda i, j: (layer, j, 0)),
            pl.BlockSpec((None, n_tail, d), lambda i, j: (layer, n_main // n_tail, 0)),
        ],
        out_specs=[pl.BlockSpec((tm, tn), lambda i, j: (i, j)),
                   pl.BlockSpec((tm, n_tail), lambda i, j: (i, 0))],
        out_shape=[jax.ShapeDtypeStruct((t, n_main), F32), jax.ShapeDtypeStruct((t, n_tail), F32)],
        scratch_shapes=[pltpu.VMEM((tm, d), BF16)],
        compiler_params=_params("parallel", "arbitrary"),
        name="gdn_in",
    )(x, g.reshape(1, d), w_t, w_t)


def _norm_matmul_t_kernel(x_ref, g_ref, wt_ref, o_ref, xn_ref, *, out_scale):
    @pl.when(pl.program_id(1) == 0)
    def _():
        xn_ref[...] = _rms_rows(x_ref[...], g_ref[...]).astype(BF16)

    y = _dot_nt(wt_ref[...], xn_ref[...])
    if out_scale != 1.0:
        y = y * out_scale
    o_ref[...] = y.astype(o_ref.dtype)


def norm_matmul_t(x, g, wt, layer, *, tm, tn, name, out_dtype=F32, out_scale=1.0):
    t, d = x.shape
    n = wt.shape[1]
    return pl.pallas_call(
        functools.partial(_norm_matmul_t_kernel, out_scale=out_scale),
        grid=(t // tm, n // tn),
        in_specs=[
            _row_block(tm, d),
            pl.BlockSpec((1, d), lambda i, j: (0, 0)),
            pl.BlockSpec((None, tn, d), lambda i, j: (layer, j, 0)),
        ],
        out_specs=pl.BlockSpec((tn, tm), lambda i, j: (j, i)),
        out_shape=jax.ShapeDtypeStruct((n, t), out_dtype),
        scratch_shapes=[pltpu.VMEM((tm, d), BF16)],
        compiler_params=_params("parallel", "arbitrary"),
        name=name,
    )(x, g.reshape(1, d), wt)


def _ffn_kernel(h_ref, g_ref, wg_ref, wu_ref, wo_ref, fg_ref, o_ref, xn_ref, *, final_norm):
    j = pl.program_id(1)
    tm = h_ref.shape[0]
    strip = min(tm, 256)

    @pl.when(j == 0)
    def _():
        for r0 in range(0, tm, strip):
            r = slice(r0, r0 + strip)
            xn_ref[r, :] = _rms_rows(h_ref[r, :], g_ref[...]).astype(BF16)
            o_ref[r, :] = jnp.zeros((strip, o_ref.shape[1]), F32)

    xn = xn_ref[...]
    tf = wg_ref.shape[1]
    half = min(tf, 256)
    acts = []
    for c0 in range(0, tf, half):
        gate = _dot(xn, wg_ref[:, c0:c0 + half].astype(BF16))
        up = _dot(xn, wu_ref[:, c0:c0 + half].astype(BF16))
        acts.append((_silu(gate) * up).astype(BF16))
    act = acts[0] if len(acts) == 1 else jnp.concatenate(acts, axis=1)
    o_ref[...] += _dot(act, wo_ref[...].astype(BF16))

    @pl.when(j == pl.num_programs(1) - 1)
    def _():
        for r0 in range(0, tm, strip):
            r = slice(r0, r0 + strip)
            y = h_ref[r, :] + 0.5 * o_ref[r, :]
            if final_norm:
                y = _rms_rows(y, fg_ref[...])
            o_ref[r, :] = y


def ffn(h, g, w_in, w_out, layer, final_g=None, *, tm=1024, tf=512):
    t, d = h.shape
    f = w_out.shape[1]
    nf = f // tf
    final_norm = final_g is not None
    fg = (final_g if final_norm else g).reshape(1, d)
    return pl.pallas_call(
        functools.partial(_ffn_kernel, final_norm=final_norm),
        grid=(t // tm, nf),
        in_specs=[
            _row_block(tm, d),
            pl.BlockSpec((1, d), lambda i, j: (0, 0)),
            pl.BlockSpec((None, d, tf), lambda i, j: (layer, 0, j)),
            pl.BlockSpec((None, d, tf), lambda i, j: (layer, 0, j + nf)),
            pl.BlockSpec((None, tf, d), lambda i, j: (layer, j, 0)),
            pl.BlockSpec((1, d), lambda i, j: (0, 0)),
        ],
        out_specs=_row_block(tm, d),
        out_shape=jax.ShapeDtypeStruct((t, d), F32),
        scratch_shapes=[pltpu.VMEM((tm, d), BF16)],
        compiler_params=_params("parallel", "arbitrary"),
        name="ffn",
    )(h, g.reshape(1, d), w_in, w_in, w_out, fg)


def _matmul_res_kernel(a_ref, w_ref, r_ref, o_ref):
    o_ref[...] = r_ref[...] + _dot(a_ref[...], w_ref[...])


def matmul_res(a, w, layer, r, *, tm, tn, name):
    t, k = a.shape
    n = w.shape[2]
    return pl.pallas_call(
        _matmul_res_kernel,
        grid=(t // tm, n // tn),
        in_specs=[
            pl.BlockSpec((tm, k), lambda i, j: (i, 0)),
            pl.BlockSpec((None, k, tn), lambda i, j: (layer, 0, j)),
            pl.BlockSpec((tm, tn), lambda i, j: (i, j)),
        ],
        out_specs=pl.BlockSpec((tm, tn), lambda i, j: (i, j)),
        out_shape=jax.ShapeDtypeStruct((t, n), F32),
        compiler_params=_params("parallel", "arbitrary"),
        name=name,
    )(a, w, r)


def _conv_kernel(prev_ref, cur_ref, w_ref, o_ref, xs_ref, *, n_norm_blocks, strip):
    i = pl.program_id(0)
    j = pl.program_id(1)
    tt, tc = cur_ref.shape
    halo = prev_ref.shape[0]
    xs_ref[0:halo, :] = jnp.where(i == 0, 0.0, prev_ref[...])
    xs_ref[halo:, :] = cur_ref[...]
    w = w_ref[...]

    def conv_strip(r0):
        y = w[GDN_CONV_WIDTH - 1:GDN_CONV_WIDTH] * xs_ref[halo + r0:halo + r0 + strip, :]
        for s in range(1, GDN_CONV_WIDTH):
            tap = GDN_CONV_WIDTH - 1 - s
            y = y + w[tap:tap + 1] * xs_ref[halo + r0 - s:halo + r0 - s + strip, :]
        return _silu(y)

    @pl.when(j < n_norm_blocks)
    def _():
        for r0 in range(0, tt, strip):
            y = conv_strip(r0)
            for hh in range(tc // GDN_HEAD_DIM):
                seg = y[:, hh * GDN_HEAD_DIM:(hh + 1) * GDN_HEAD_DIM]
                ss = jnp.sum(seg * seg, axis=-1, keepdims=True)
                o_ref[r0:r0 + strip, hh * GDN_HEAD_DIM:(hh + 1) * GDN_HEAD_DIM] = seg * lax.rsqrt(ss + RMS_EPS)

    @pl.when(j >= n_norm_blocks)
    def _():
        for r0 in range(0, tt, strip):
            o_ref[r0:r0 + strip, :] = conv_strip(r0)


def gdn_conv(proj, conv_w, *, conv_dim, key_dim, tt=1024, tc=1024, strip=16):
    t = proj.shape[0]
    halo = 8
    assert halo >= GDN_CONV_WIDTH - 1
    return pl.pallas_call(
        functools.partial(_conv_kernel, n_norm_blocks=2 * key_dim // tc, strip=strip),
        grid=(t // tt, conv_dim // tc),
        in_specs=[
            pl.BlockSpec((halo, tc), lambda i, j: (jnp.maximum(i * (tt // halo) - 1, 0), j)),
            pl.BlockSpec((tt, tc), lambda i, j: (i, j)),
            pl.BlockSpec((GDN_CONV_WIDTH, tc), lambda i, j: (0, j)),
        ],
        out_specs=pl.BlockSpec((tt, tc), lambda i, j: (i, j)),
        out_shape=jax.ShapeDtypeStruct((t, conv_dim), F32),
        scratch_shapes=[pltpu.VMEM((tt + halo, tc), F32)],
        compiler_params=_params("parallel", "parallel"),
        name="gdn_conv",
    )(proj, proj, conv_w)


def _gates_kernel(ba_ref, alog_ref, dtb_ref, beta_ref, g_ref):
    nh = beta_ref.shape[1]
    ba = ba_ref[...]
    beta_ref[...] = jax.nn.sigmoid(ba[:, :nh])
    z = ba[:, nh:] + dtb_ref[...]
    softplus = jnp.maximum(z, 0.0) + jnp.log1p(jnp.exp(-jnp.abs(z)))
    g_ref[...] = -jnp.exp(alog_ref[...]) * softplus


def gdn_gates(ba, a_log, dt_bias, *, tt=1024):
    t, two_h = ba.shape
    nh = two_h // 2
    return pl.pallas_call(
        _gates_kernel,
        grid=(t // tt,),
        in_specs=[
            pl.BlockSpec((tt, two_h), lambda i: (i, 0)),
            pl.BlockSpec((1, nh), lambda i: (0, 0)),
            pl.BlockSpec((1, nh), lambda i: (0, 0)),
        ],
        out_specs=[pl.BlockSpec((tt, nh), lambda i: (i, 0)), pl.BlockSpec((tt, nh), lambda i: (i, 0))],
        out_shape=[jax.ShapeDtypeStruct((t, nh), F32), jax.ShapeDtypeStruct((t, nh), F32)],
        compiler_params=_params("parallel"),
        name="gdn_gates",
    )(ba, a_log.reshape(1, nh), dt_bias.reshape(1, nh))


def _prep_kernel(q_ref, k_ref, v_ref, gsel_ref, bsel_ref, grow_ref, wq_ref, l2_ref, u_ref, glw_ref):
    kh = pl.program_id(0)
    tp, nh = gsel_ref.shape
    c = GDN_CHUNK
    dh = GDN_HEAD_DIM
    scale = dh ** -0.5
    lane_h = lax.broadcasted_iota(jnp.int32, (tp, nh), 1)

    def column(ref, head):
        return jnp.sum(jnp.where(lane_h == head, ref[...], 0.0), axis=1, keepdims=True)

    gcol = [column(gsel_ref, 2 * kh + hh) for hh in range(2)]
    bcol = [column(bsel_ref, 2 * kh + hh) for hh in range(2)]
    grow = [grow_ref[hh] for hh in range(2)]

    ii = lax.broadcasted_iota(jnp.int32, (c, 2 * c), 0)
    ll = lax.broadcasted_iota(jnp.int32, (c, 2 * c), 1)
    jj = jnp.bitwise_and(ll, c - 1)
    head_a = ll < c
    incl = ii >= jj
    strict = ii > jj
    wi = lax.broadcasted_iota(jnp.int32, (c, 4 * c), 0)
    wl = lax.broadcasted_iota(jnp.int32, (c, 4 * c), 1)
    eye_w = jnp.where(wi == jnp.bitwise_and(wl, c - 1), 1.0, 0.0).astype(F32)
    br = lax.broadcasted_iota(jnp.int32, (4 * c, 4 * c), 0)
    bc = lax.broadcasted_iota(jnp.int32, (4 * c, 4 * c), 1)
    same_block = (br // c) == (bc // c)
    strict_block = jnp.logical_and(same_block, br != bc)

    def block_diag(packed):
        return jnp.concatenate([packed] * 4, axis=0)

    def packed_matmul(a, b):
        bd = jnp.where(same_block, block_diag(b), 0.0).astype(BF16)
        return _dot(a.astype(BF16), bd)

    n_half = tp // (2 * c)
    lmats = [[] for _ in range(n_half)]
    rhs_blocks = [[] for _ in range(n_half)]
    for half in range(n_half):
        for y in range(2):
            n = 2 * half + y
            r = slice(n * c, (n + 1) * c)
            q = q_ref[r, :] * scale
            k = k_ref[r, :]
            g_pair = jnp.where(head_a, gcol[0][r], gcol[1][r])
            b_pair = jnp.where(head_a, bcol[0][r], bcol[1][r])
            grow_pair = jnp.concatenate([grow[0][:, r], grow[1][:, r]], axis=1)
            tri = jnp.where(incl, grow_pair, 0.0)
            gc = [jnp.sum(jnp.where(head_a, tri, 0.0), axis=1, keepdims=True),
                  jnp.sum(jnp.where(head_a, 0.0, tri), axis=1, keepdims=True)]
            gc_col = jnp.where(head_a, gc[0], gc[1])
            gc_row = jnp.sum(jnp.where(ii <= jj, g_pair, 0.0), axis=0, keepdims=True)
            decay = jnp.where(incl, jnp.exp(jnp.where(incl, gc_col - gc_row, 0.0)), 0.0)
            kb16 = k.astype(BF16)
            gram = _dot_nt(jnp.concatenate([q.astype(BF16), kb16], axis=0),
                           jnp.concatenate([kb16, kb16], axis=0))
            l2_ref[n, 0:c, :] = jnp.where(incl, gram[:c] * decay, 0.0).astype(BF16)
            lmats[half].append(jnp.where(strict, gram[c:] * b_pair * decay, 0.0))
            kups = []
            for hh in range(2):
                g_last = jnp.sum(gcol[hh][r], axis=0, keepdims=True)
                eg = jnp.exp(gc[hh])
                beta = bcol[hh][r]
                kups.append(k * jnp.exp(g_last - gc[hh]))
                wq_ref[n, c:2 * c, hh * dh:(hh + 1) * dh] = (q * eg).astype(BF16)
                glw_ref[n, :, hh * dh:(hh + 1) * dh] = jnp.broadcast_to(jnp.exp(g_last), (8, dh))
                rhs_blocks[half].append(
                    jnp.concatenate([v_ref[r, hh * dh:(hh + 1) * dh] * beta, k * (beta * eg)], axis=1))
            l2_ref[n, c:c + dh, :] = jnp.concatenate(kups, axis=0).T.astype(BF16)

    ps = [-jnp.concatenate(lm, axis=1) for lm in lmats]
    tinvs = [eye_w + p for p in ps]
    m = 1
    while 2 * m < c:
        ps = [packed_matmul(p, p) for p in ps]
        tinvs = [t + packed_matmul(t, p) for t, p in zip(tinvs, ps)]
        m *= 2
    for half in range(n_half):
        rhs_all = jnp.concatenate(rhs_blocks[half], axis=0)
        t_off = jnp.where(strict_block, block_diag(tinvs[half]), 0.0).astype(BF16)
        uw = rhs_all + _dot(t_off, rhs_all.astype(BF16))
        for y in range(2):
            n = 2 * half + y
            for hh in range(2):
                blk = uw[(2 * y + hh) * c:(2 * y + hh + 1) * c]
                u_ref[n * c:(n + 1) * c, hh * dh:(hh + 1) * dh] = blk[:, :dh]
                wq_ref[n, 0:c, hh * dh:(hh + 1) * dh] = blk[:, dh:].astype(BF16)


def gdn_prep(qkv, g, beta, g_rows, *, n_heads, rep, tp=2048):
    assert rep == 2
    t = qkv.shape[0]
    dh = GDN_HEAD_DIM
    c = GDN_CHUNK
    nk = n_heads // rep
    nc = tp // c
    return pl.pallas_call(
        _prep_kernel,
        grid=(nk, t // tp),
        in_specs=[
            pl.BlockSpec((tp, dh), lambda h, i: (i, h)),
            pl.BlockSpec((tp, dh), lambda h, i: (i, nk + h)),
            pl.BlockSpec((tp, 2 * dh), lambda h, i: (i, nk + h)),
            pl.BlockSpec((tp, n_heads), lambda h, i: (i, 0)),
            pl.BlockSpec((tp, n_heads), lambda h, i: (i, 0)),
            pl.BlockSpec((2, 1, tp), lambda h, i: (h, 0, i)),
        ],
        out_specs=[
            pl.BlockSpec((None, nc, 2 * c, 2 * dh), lambda h, i: (h, i, 0, 0)),
            pl.BlockSpec((None, nc, c + dh, 2 * c), lambda h, i: (h, i, 0, 0)),
            pl.BlockSpec((None, tp, 2 * dh), lambda h, i: (h, i, 0)),
            pl.BlockSpec((None, nc, 8, 2 * dh), lambda h, i: (h, i, 0, 0)),
        ],
        out_shape=[
            jax.ShapeDtypeStruct((nk, t // c, 2 * c, 2 * dh), BF16),
            jax.ShapeDtypeStruct((nk, t // c, c + dh, 2 * c), BF16),
            jax.ShapeDtypeStruct((nk, t, 2 * dh), F32),
            jax.ShapeDtypeStruct((nk, t // c, 8, 2 * dh), F32),
        ],
        compiler_params=_params("parallel", "parallel"),
        name="gdn_prep",
    )(qkv, qkv, qkv, g, beta, g_rows)


def _scan_kernel(wq_ref, l2_ref, u_ref, glw_ref, z_ref, gn_ref, o_ref, s_ref):
    pb, nc = wq_ref.shape[0], wq_ref.shape[1]
    gn = gn_ref[...]
    c = GDN_CHUNK
    dh = GDN_HEAD_DIM

    @pl.when(pl.program_id(1) == 0)
    def _():
        s_ref[...] = jnp.zeros_like(s_ref)

    def pair_diag(x):
        z = jnp.zeros((x.shape[0], dh), x.dtype)
        return jnp.concatenate([jnp.concatenate([x[:, :dh], z], axis=1),
                                jnp.concatenate([z, x[:, dh:]], axis=1)], axis=0)

    states = [s_ref[p] for p in range(pb)]
    for n in range(nc):
        r = slice(n * c, (n + 1) * c)
        for p in range(pb):
            ws = _dot(wq_ref[p, n], pair_diag(states[p].astype(BF16)))
            v_new = (u_ref[p, r, :] - ws[:c]).astype(BF16)
            upd = _dot(l2_ref[p, n], pair_diag(v_new))
            o = ws[c:] + upd[:c]
            states[p] = states[p] * glw_ref[p, n, 0:1, :] + upd[c:]
            for hh in range(2):
                cols = slice((2 * p + hh) * dh, (2 * p + hh + 1) * dh)
                gated = _rms_rows(o[:, hh * dh:(hh + 1) * dh], gn) * _silu(z_ref[r, cols])
                o_ref[r, cols] = gated.astype(BF16)
    for p in range(pb):
        s_ref[p] = states[p]


def gdn_scan(wq, l2, u, glw, proj, z_col0, out_norm, *, pb=16, tp=256):
    nk, t, two_dh = u.shape
    c = GDN_CHUNK
    nc = tp // c
    width = pb * two_dh
    blk4 = lambda a: pl.BlockSpec((pb, nc) + a.shape[2:], lambda h, i: (h, i, 0, 0))
    return pl.pallas_call(
        _scan_kernel,
        grid=(nk // pb, t // tp),
        in_specs=[blk4(wq), blk4(l2), pl.BlockSpec((pb, tp, two_dh), lambda h, i: (h, i, 0)), blk4(glw),
                  pl.BlockSpec((tp, width), lambda h, i: (i, z_col0 // width + h)),
                  pl.BlockSpec((1, GDN_HEAD_DIM), lambda h, i: (0, 0))],
        out_specs=pl.BlockSpec((tp, width), lambda h, i: (i, h)),
        out_shape=jax.ShapeDtypeStruct((t, nk * two_dh), BF16),
        scratch_shapes=[pltpu.VMEM((pb, two_dh // 2, two_dh), F32)],
        compiler_params=_params("parallel", "arbitrary"),
        name="gdn_scan",
    )(wq, l2, u, glw, proj, out_norm.reshape(1, GDN_HEAD_DIM))


def _moba_kv_kernel(x_ref, g_ref, wk_ref, wvt_ref, kb_ref, vt_ref, mean_ref):
    xn = _rms_rows(x_ref[...], g_ref[...]).astype(BF16)
    k = _dot(xn, wk_ref[...])
    kb_ref[...] = k.astype(BF16)
    for b in range(mean_ref.shape[0]):
        mean_ref[b] = jnp.mean(k[b * MOBA_BLOCK:(b + 1) * MOBA_BLOCK], axis=0, keepdims=True)
    vt_ref[...] = _dot_nt(wvt_ref[...], xn).astype(BF16)


def moba_kv(x, g, wk, wvt, *, tm):
    t, d = x.shape
    kv = wk.shape[1]
    nb = t // MOBA_BLOCK
    bpt = tm // MOBA_BLOCK
    kb, vt, means = pl.pallas_call(
        _moba_kv_kernel,
        grid=(t // tm,),
        in_specs=[
            pl.BlockSpec((tm, d), lambda i: (i, 0)),
            pl.BlockSpec((1, d), lambda i: (0, 0)),
            pl.BlockSpec((d, kv), lambda i: (0, 0)),
            pl.BlockSpec((kv, d), lambda i: (0, 0)),
        ],
        out_specs=[pl.BlockSpec((tm, kv), lambda i: (i, 0)),
                   pl.BlockSpec((kv, tm), lambda i: (0, i)),
                   pl.BlockSpec((bpt, 1, kv), lambda i: (i, 0, 0))],
        out_shape=[jax.ShapeDtypeStruct((t, kv), BF16), jax.ShapeDtypeStruct((kv, t), BF16),
                   jax.ShapeDtypeStruct((nb, 1, kv), F32)],
        compiler_params=_params("parallel"),
        name="moba_kv",
    )(x, g.reshape(1, d), wk, wvt)
    return kb, vt, means.reshape(nb, kv)


def _rel_bucket_table(n_dist):
    n = np.arange(n_dist)
    max_exact = N_REL_BUCKETS // 2
    ratio = np.log(np.maximum(n, max_exact).astype(np.float32) / np.float32(max_exact)) \
        / np.float32(math.log(REL_MAX_DISTANCE / max_exact))
    large = np.minimum(max_exact + (ratio.astype(np.float32) * (N_REL_BUCKETS - max_exact)).astype(np.int32),
                       N_REL_BUCKETS - 1)
    return np.where(n < max_exact, n, large).astype(np.int32)


def _bucket_tiles():
    table = _rel_bucket_table(2 * MOBA_BLOCK)
    a = np.arange(MOBA_BLOCK)[:, None]
    b = np.arange(MOBA_BLOCK)[None, :]
    d_own = b - a
    own = np.where(d_own >= 0, table[np.maximum(d_own, 0)], -1)
    prev = table[MOBA_BLOCK + b - a]
    return np.stack([own, prev]).astype(np.int32)


def _moba_kernel(rb_ref, qt_ref, k_ref, vt_ref, km_ref, bucket_ref, o_ref,
                 bias_ref, sel_ref, m_ref, acc_ref, *, group):
    g = pl.program_id(0)
    i = pl.program_id(1)
    bs = MOBA_BLOCK
    dh = ATT_HEAD_DIM
    nb = km_ref.shape[0]
    nq = group * bs

    @pl.when(i == 0)
    def _():
        for hh in range(group):
            for t in range(2):
                bk = bucket_ref[t]
                bias = jnp.full(bk.shape, MASK_VALUE, F32)
                for b in range(N_REL_BUCKETS):
                    bias = jnp.where(bk == b, rb_ref[b, g * group + hh] * LOG2_E, bias)
                bias_ref[t, :, hh * bs:(hh + 1) * bs] = bias

    qt = jnp.concatenate([qt_ref[hh * dh:(hh + 1) * dh, :] for hh in range(group)], axis=1)

    blk = lax.broadcasted_iota(jnp.int32, (nb, nq), 0)
    gate = jnp.where(blk < i, _dot(km_ref[...].astype(BF16), qt), -jnp.inf)
    sel = jnp.zeros((nb, nq), F32)
    for r in range(MOBA_TOPK):
        mx = jnp.max(gate, axis=0, keepdims=True)
        first = jnp.min(jnp.where(gate == mx, blk, nb), axis=0, keepdims=True)
        pick = blk == first
        sel = jnp.where(jnp.logical_and(pick, r < i), 1.0, sel)
        gate = jnp.where(pick, -jnp.inf, gate)
    sel_ref[...] = sel

    def keys(j, nblk=1):
        return k_ref[pl.ds(pl.multiple_of(j * bs, bs), nblk * bs), :]

    def values_t(j, nblk=1):
        return vt_ref[:, pl.ds(pl.multiple_of(j * bs, bs), nblk * bs)]

    OWN, PREV, FAR = 0, 1, 2
    SLAB = 32
    LOOKAHEAD = 4
    L_ROWS = MOBA_L_ROWS

    def attend(state, units):
        m_all, acc_all = state
        m_cur = [m_all[:, hh * bs:(hh + 1) * bs] for hh in range(group)]
        acc_cur = [acc_all[:, hh * bs:(hh + 1) * bs] for hh in range(group)]
        tasks = [(u, hh) for u in range(len(units)) for hh in range(group)]

        def score(t):
            u, hh = tasks[t]
            j, kinds = units[u]
            return _dot(keys(j, len(kinds)), qt[:, hh * bs:(hh + 1) * bs])

        pending = {t: score(t) for t in range(min(LOOKAHEAD, len(tasks)))}
        for t, (u, hh) in enumerate(tasks):
            if t + LOOKAHEAD < len(tasks):
                pending[t + LOOKAHEAD] = score(t + LOOKAHEAD)
            scores = pending.pop(t)
            j, kinds = units[u]
            cols = slice(hh * bs, (hh + 1) * bs)
            far_bias = rb_ref[N_REL_BUCKETS - 1, g * group + hh] * LOG2_E
            parts = []
            run_max = None
            for b, kind in enumerate(kinds):
                if kind != OWN:
                    row = jnp.where(sel_ref[pl.ds(j + b, 1), cols] > 0.0, far_bias if kind == FAR else 0.0,
                                    MASK_VALUE)
                    row = jnp.broadcast_to(row, (SLAB, bs))
                for r0 in range(0, bs, SLAB):
                    part = scores[b * bs + r0:b * bs + r0 + SLAB]
                    if kind == OWN:
                        part = part + bias_ref[0, r0:r0 + SLAB, cols]
                    elif kind == PREV:
                        part = part + bias_ref[1, r0:r0 + SLAB, cols] + row
                    else:
                        part = part + row
                    parts.append(part)
                    run_max = part if run_max is None else jnp.maximum(run_max, part)
            m_new = jnp.maximum(m_cur[hh], jnp.max(run_max, axis=0, keepdims=True))
            alpha = jnp.exp2(m_cur[hh] - m_new)
            m_slab = jnp.broadcast_to(m_new, (SLAB, bs))
            packed = [jnp.exp2(part - m_slab).astype(BF16) for part in parts]
            vt_ones = jnp.concatenate([values_t(j, len(kinds)), jnp.ones((L_ROWS, len(kinds) * bs), BF16)], axis=0)
            m_cur[hh] = m_new
            acc_cur[hh] = alpha * acc_cur[hh] + _dot(vt_ones, jnp.concatenate(packed, axis=0))
        return jnp.concatenate(m_cur, axis=1), jnp.concatenate(acc_cur, axis=1)

    def load_state():
        return m_ref[...], acc_ref[...]

    def store_state(state):
        m_ref[...], acc_ref[...] = state

    empty = (jnp.full((1, nq), MASK_VALUE, F32), jnp.zeros((dh + L_ROWS, nq), F32))

    n_old = jnp.maximum(i - 1, 0)
    odd = n_old % 2 == 1

    @pl.when(i == 0)
    def _():
        store_state(attend(empty, [(i, (OWN,))]))

    @pl.when(jnp.logical_and(i >= 1, jnp.logical_not(odd)))
    def _():
        store_state(attend(empty, [(i - 1, (PREV, OWN))]))

    @pl.when(odd)
    def _():
        store_state(attend(empty, [(i - 2, (FAR, PREV, OWN))]))

    n_far = n_old - n_old % 2

    def far_units(j0, n_units):
        return [(j0 + 2 * u, (FAR, FAR)) for u in range(n_units)]

    def far_body(jj, carry):
        store_state(attend(load_state(), far_units(4 * jj, 2)))
        return carry

    lax.fori_loop(0, n_far // 4, far_body, 0)

    @pl.when(n_far % 4 == 2)
    def _():
        store_state(attend(load_state(), far_units(n_far - 2, 1)))

    out = acc_ref[0:dh, :] / acc_ref[dh:dh + 1, :]
    for hh in range(group):
        o_ref[:, hh * dh:(hh + 1) * dh] = out[:, hh * bs:(hh + 1) * bs].T.astype(BF16)


def moba_attention(qt, k, vt, k_means, rel_bias, *, n_heads, n_kv_heads):
    dh = ATT_HEAD_DIM
    t = k.shape[0]
    bs = MOBA_BLOCK
    nb = t // bs
    group = n_heads // n_kv_heads
    nq = group * bs
    assert REL_MAX_DISTANCE <= bs
    buckets = jnp.asarray(_bucket_tiles())
    return pl.pallas_call(
        functools.partial(_moba_kernel, group=group),
        grid=(n_kv_heads, nb),
        in_specs=[
            pl.BlockSpec(memory_space=pltpu.SMEM),
            pl.BlockSpec((group * dh, bs), lambda g, i: (g, i)),
            pl.BlockSpec((t, dh), lambda g, i: (0, g)),
            pl.BlockSpec((dh, t), lambda g, i: (g, 0)),
            pl.BlockSpec((nb, dh), lambda g, i: (0, g)),
            pl.BlockSpec((2, bs, bs), lambda g, i: (0, 0, 0)),
        ],
        out_specs=pl.BlockSpec((bs, group * dh), lambda g, i: (i, g)),
        out_shape=jax.ShapeDtypeStruct((t, n_heads * dh), BF16),
        scratch_shapes=[
            pltpu.VMEM((2, bs, nq), F32),
            pltpu.VMEM((nb, nq), F32),
            pltpu.VMEM((1, nq), F32),
            pltpu.VMEM((dh + MOBA_L_ROWS, nq), F32),
        ],
        compiler_params=_params("parallel", "arbitrary"),
        name="moba_attention",
    )(rel_bias, qt, k, vt, k_means, buckets)


def kernel(x, ffn1_norm, ffn1_w_in, ffn1_w_out, mix_norm, ffn2_norm, ffn2_w_in, ffn2_w_out,
           gdn_w_in, gdn_conv_w, gdn_a_log, gdn_dt_bias, gdn_out_norm, gdn_w_out,
           kv_norm, w_kv, moba_w_q, moba_w_o, rel_bias, final_norm):
    batch, seq, d_model = x.shape
    depth = ffn1_norm.shape[0]
    n_a = gdn_w_in.shape[0]
    n_v_heads = gdn_a_log.shape[1]
    value_dim = gdn_w_out.shape[1]
    conv_dim = gdn_conv_w.shape[2]
    key_dim = (conv_dim - value_dim) // 2
    rep = value_dim // key_dim
    n_heads = moba_w_q.shape[2] // ATT_HEAD_DIM
    kv_dim = w_kv.shape[1] // 2
    n_kv_heads = kv_dim // ATT_HEAD_DIM
    bf = lambda a: a.astype(BF16)
    gdn_w_in_t = jnp.swapaxes(gdn_w_in, 1, 2)
    gdn_w_out_bf = bf(gdn_w_out)
    moba_w_q_t = bf(jnp.swapaxes(moba_w_q, 1, 2))
    moba_w_o_bf = bf(moba_w_o)

    outs = []
    for bi in range(batch):
        h = x[bi]
        k_nat = vt = k_means = None
        for layer in range(depth):
            if layer == n_a:
                k_nat, vt, k_means = moba_kv(h, kv_norm, bf(w_kv[:, :kv_dim]), bf(w_kv[:, kv_dim:].T), tm=1024)
            h = ffn(h, ffn1_norm[layer], ffn1_w_in, ffn1_w_out, layer)
            if layer < n_a:
                main = conv_dim + value_dim
                n_ba = 2 * n_v_heads
                proj, ba = gdn_in_proj(h, mix_norm[layer], gdn_w_in_t, layer, n_main=main, n_tail=n_ba,
                                       tm=2048, tn=512)
                beta, g = gdn_gates(ba, gdn_a_log[layer], gdn_dt_bias[layer])
                qkv = gdn_conv(proj, gdn_conv_w[layer], conv_dim=conv_dim, key_dim=key_dim)
                g_rows = g.T.reshape(n_v_heads, 1, seq)
                wq, l2, u, glw = gdn_prep(qkv, g, beta, g_rows, n_heads=n_v_heads, rep=rep)
                gated = gdn_scan(wq, l2, u, glw, proj, conv_dim, gdn_out_norm[layer])
                h = matmul_res(gated, gdn_w_out_bf, layer, h, tm=512, tn=1024, name="gdn_out")
            else:
                j = layer - n_a
                qt = norm_matmul_t(h, mix_norm[layer], moba_w_q_t, j, tm=1024, tn=1024, name="moba_qt",
                                   out_dtype=BF16, out_scale=MOBA_Q_SCALE)
                att = moba_attention(qt, k_nat, vt, k_means, rel_bias,
                                     n_heads=n_heads, n_kv_heads=n_kv_heads)
                h = matmul_res(att, moba_w_o_bf, j, h, tm=1024, tn=1024, name="moba_out")
            last = layer == depth - 1
            h = ffn(h, ffn2_norm[layer], ffn2_w_in, ffn2_w_out, layer, final_norm if last else None)
        outs.append(h)
    return jnp.stack(outs)
```

```python
import functools
import math
from typing import NamedTuple

import numpy as np
import jax
import jax.numpy as jnp
from jax import lax
from jax.experimental import pallas as pl
from jax.experimental.pallas import tpu as pltpu

F32 = jnp.float32
BF16 = jnp.bfloat16

RMS_EPS = 1e-6
GDN_HEAD_DIM = 128
GDN_CHUNK = 64
GDN_CONV_WIDTH = 4
ATT_HEAD_DIM = 128
MOBA_BLOCK = 256
MOBA_TOPK = 3
N_REL_BUCKETS = 32
REL_MAX_DISTANCE = 128
MASK_VALUE = -1e30
LOG2_E = 1.4426950408889634
MOBA_Q_SCALE = ATT_HEAD_DIM ** -0.5 * LOG2_E

V7X_VMEM_BYTES = 64 * 1024 * 1024
V7X_BF16_SUBLANE_TILE = 16
MOBA_L_ROWS = V7X_BF16_SUBLANE_TILE

VMEM_LIMIT_BYTES = V7X_VMEM_BYTES - 6 * 1024 * 1024


class Tiles(NamedTuple):
    ffn_rows: int = 1024
    ffn_cols: int = 512
    gdn_in_rows: int = 2048
    gdn_in_cols: int = 512
    gates_rows: int = 1024
    conv_rows: int = 1024
    conv_cols: int = 1024
    conv_strip: int = 16
    prep_rows: int = 2048
    scan_pairs: int = 16
    scan_rows: int = 256
    gdn_out_rows: int = 512
    gdn_out_cols: int = 1024
    kv_rows: int = 1024
    qt_rows: int = 1024
    qt_cols: int = 1024
    moba_out_rows: int = 1024
    moba_out_cols: int = 1024


TILES = Tiles()


def _params(*sem):
    return pltpu.CompilerParams(dimension_semantics=sem, vmem_limit_bytes=VMEM_LIMIT_BYTES)


def _rms_rows(x, g):
    ms = jnp.mean(x * x, axis=-1, keepdims=True)
    return x * lax.rsqrt(ms + RMS_EPS) * g


def _silu(x):
    return x * jax.nn.sigmoid(x)


def _dot(a, b):
    return jnp.dot(a, b, preferred_element_type=F32)


def _dot_nt(a, b):
    return lax.dot_general(a, b, (((1,), (1,)), ((), ())), preferred_element_type=F32)


def _row_block(tm, d):
    return pl.BlockSpec((tm, d), lambda i, j: (i, 0), pipeline_mode=pl.Buffered(1))


def _gdn_in_kernel(x_ref, g_ref, w_ref, wt_ref, o_ref, ot_ref, xn_ref):
    @pl.when(pl.program_id(1) == 0)
    def _():
        xn_ref[...] = _rms_rows(x_ref[...], g_ref[...]).astype(BF16)
        ot_ref[...] = _dot_nt(xn_ref[...], wt_ref[...].astype(BF16))

    o_ref[...] = _dot_nt(xn_ref[...], w_ref[...].astype(BF16))


def gdn_in_proj(x, g, w_t, layer, *, n_main, n_tail, tm, tn):
    t, d = x.shape
    assert n_main % tn == 0 and n_main % n_tail == 0
    return pl.pallas_call(
        _gdn_in_kernel,
        grid=(t // tm, n_main // tn),
        in_specs=[
            _row_block(tm, d),
            pl.BlockSpec((1, d), lambda i, j: (0, 0)),
            pl.BlockSpec((None, tn, d), lambda i, j: (layer, j, 0)),
            pl.BlockSpec((None, n_tail, d), lambda i, j: (layer, n_main // n_tail, 0)),
        ],
        out_specs=[pl.BlockSpec((tm, tn), lambda i, j: (i, j)),
                   pl.BlockSpec((tm, n_tail), lambda i, j: (i, 0))],
        out_shape=[jax.ShapeDtypeStruct((t, n_main), F32), jax.ShapeDtypeStruct((t, n_tail), F32)],
        scratch_shapes=[pltpu.VMEM((tm, d), BF16)],
        compiler_params=_params("parallel", "arbitrary"),
        name="gdn_in",
    )(x, g.reshape(1, d), w_t, w_t)


def _norm_matmul_t_kernel(x_ref, g_ref, wt_ref, o_ref, xn_ref, *, out_scale):
    @pl.when(pl.program_id(1) == 0)
    def _():
        xn_ref[...] = _rms_rows(x_ref[...], g_ref[...]).astype(BF16)

    y = _dot_nt(wt_ref[...], xn_ref[...])
    if out_scale != 1.0:
        y = y * out_scale
    o_ref[...] = y.astype(o_ref.dtype)


def norm_matmul_t(x, g, wt, layer, *, tm, tn, name, out_dtype=F32, out_scale=1.0):
    t, d = x.shape
    n = wt.shape[1]
    return pl.pallas_call(
        functools.partial(_norm_matmul_t_kernel, out_scale=out_scale),
        grid=(t // tm, n // tn),
        in_specs=[
            _row_block(tm, d),
            pl.BlockSpec((1, d), lambda i, j: (0, 0)),
            pl.BlockSpec((None, tn, d), lambda i, j: (layer, j, 0)),
        ],
        out_specs=pl.BlockSpec((tn, tm), lambda i, j: (j, i)),
        out_shape=jax.ShapeDtypeStruct((n, t), out_dtype),
        scratch_shapes=[pltpu.VMEM((tm, d), BF16)],
        compiler_params=_params("parallel", "arbitrary"),
        name=name,
    )(x, g.reshape(1, d), wt)


def _ffn_kernel(h_ref, g_ref, wg_ref, wu_ref, wo_ref, fg_ref, o_ref, xn_ref, *, final_norm):
    j = pl.program_id(1)
    tm = h_ref.shape[0]
    strip = min(tm, 256)

    @pl.when(j == 0)
    def _():
        for r0 in range(0, tm, strip):
            r = slice(r0, r0 + strip)
            xn_ref[r, :] = _rms_rows(h_ref[r, :], g_ref[...]).astype(BF16)
            o_ref[r, :] = jnp.zeros((strip, o_ref.shape[1]), F32)

    xn = xn_ref[...]
    tf = wg_ref.shape[1]
    half = min(tf, 256)
    acts = []
    for c0 in range(0, tf, half):
        gate = _dot(xn, wg_ref[:, c0:c0 + half].astype(BF16))
        up = _dot(xn, wu_ref[:, c0:c0 + half].astype(BF16))
        acts.append((_silu(gate) * up).astype(BF16))
    act = acts[0] if len(acts) == 1 else jnp.concatenate(acts, axis=1)
    o_ref[...] += _dot(act, wo_ref[...].astype(BF16))

    @pl.when(j == pl.num_programs(1) - 1)
    def _():
        for r0 in range(0, tm, strip):
            r = slice(r0, r0 + strip)
            y = h_ref[r, :] + 0.5 * o_ref[r, :]
            if final_norm:
                y = _rms_rows(y, fg_ref[...])
            o_ref[r, :] = y


def ffn(h, g, w_in, w_out, layer, final_g=None, *, tm, tf):
    t, d = h.shape
    f = w_out.shape[1]
    nf = f // tf
    final_norm = final_g is not None
    fg = (final_g if final_norm else g).reshape(1, d)
    return pl.pallas_call(
        functools.partial(_ffn_kernel, final_norm=final_norm),
        grid=(t // tm, nf),
        in_specs=[
            _row_block(tm, d),
            pl.BlockSpec((1, d), lambda i, j: (0, 0)),
            pl.BlockSpec((None, d, tf), lambda i, j: (layer, 0, j)),
            pl.BlockSpec((None, d, tf), lambda i, j: (layer, 0, j + nf)),
            pl.BlockSpec((None, tf, d), lambda i, j: (layer, j, 0)),
            pl.BlockSpec((1, d), lambda i, j: (0, 0)),
        ],
        out_specs=_row_block(tm, d),
        out_shape=jax.ShapeDtypeStruct((t, d), F32),
        scratch_shapes=[pltpu.VMEM((tm, d), BF16)],
        compiler_params=_params("parallel", "arbitrary"),
        name="ffn",
    )(h, g.reshape(1, d), w_in, w_in, w_out, fg)


def _matmul_res_kernel(a_ref, w_ref, r_ref, o_ref):
    o_ref[...] = r_ref[...] + _dot(a_ref[...], w_ref[...])


def matmul_res(a, w, layer, r, *, tm, tn, name):
    t, k = a.shape
    n = w.shape[2]
    return pl.pallas_call(
        _matmul_res_kernel,
        grid=(t // tm, n // tn),
        in_specs=[
            pl.BlockSpec((tm, k), lambda i, j: (i, 0)),
            pl.BlockSpec((None, k, tn), lambda i, j: (layer, 0, j)),
            pl.BlockSpec((tm, tn), lambda i, j: (i, j)),
        ],
        out_specs=pl.BlockSpec((tm, tn), lambda i, j: (i, j)),
        out_shape=jax.ShapeDtypeStruct((t, n), F32),
        compiler_params=_params("parallel", "arbitrary"),
        name=name,
    )(a, w, r)


def _conv_kernel(prev_ref, cur_ref, w_ref, o_ref, xs_ref, *, n_norm_blocks, strip):
    i = pl.program_id(0)
    j = pl.program_id(1)
    tt, tc = cur_ref.shape
    halo = prev_ref.shape[0]
    xs_ref[0:halo, :] = jnp.where(i == 0, 0.0, prev_ref[...])
    xs_ref[halo:, :] = cur_ref[...]
    w = w_ref[...]

    def conv_strip(r0):
        y = w[GDN_CONV_WIDTH - 1:GDN_CONV_WIDTH] * xs_ref[halo + r0:halo + r0 + strip, :]
        for s in range(1, GDN_CONV_WIDTH):
            tap = GDN_CONV_WIDTH - 1 - s
            y = y + w[tap:tap + 1] * xs_ref[halo + r0 - s:halo + r0 - s + strip, :]
        return _silu(y)

    @pl.when(j < n_norm_blocks)
    def _():
        for r0 in range(0, tt, strip):
            y = conv_strip(r0)
            for hh in range(tc // GDN_HEAD_DIM):
                seg = y[:, hh * GDN_HEAD_DIM:(hh + 1) * GDN_HEAD_DIM]
                ss = jnp.sum(seg * seg, axis=-1, keepdims=True)
                o_ref[r0:r0 + strip, hh * GDN_HEAD_DIM:(hh + 1) * GDN_HEAD_DIM] = seg * lax.rsqrt(ss + RMS_EPS)

    @pl.when(j >= n_norm_blocks)
    def _():
        for r0 in range(0, tt, strip):
            o_ref[r0:r0 + strip, :] = conv_strip(r0)


def gdn_conv(proj, conv_w, *, conv_dim, key_dim, tt, tc, strip):
    t = proj.shape[0]
    halo = 8
    assert halo >= GDN_CONV_WIDTH - 1
    return pl.pallas_call(
        functools.partial(_conv_kernel, n_norm_blocks=2 * key_dim // tc, strip=strip),
        grid=(t // tt, conv_dim // tc),
        in_specs=[
            pl.BlockSpec((halo, tc), lambda i, j: (jnp.maximum(i * (tt // halo) - 1, 0), j)),
            pl.BlockSpec((tt, tc), lambda i, j: (i, j)),
            pl.BlockSpec((GDN_CONV_WIDTH, tc), lambda i, j: (0, j)),
        ],
        out_specs=pl.BlockSpec((tt, tc), lambda i, j: (i, j)),
        out_shape=jax.ShapeDtypeStruct((t, conv_dim), F32),
        scratch_shapes=[pltpu.VMEM((tt + halo, tc), F32)],
        compiler_params=_params("parallel", "parallel"),
        name="gdn_conv",
    )(proj, proj, conv_w)


def _gates_kernel(ba_ref, alog_ref, dtb_ref, beta_ref, g_ref):
    nh = beta_ref.shape[1]
    ba = ba_ref[...]
    beta_ref[...] = jax.nn.sigmoid(ba[:, :nh])
    z = ba[:, nh:] + dtb_ref[...]
    softplus = jnp.maximum(z, 0.0) + jnp.log1p(jnp.exp(-jnp.abs(z)))
    g_ref[...] = -jnp.exp(alog_ref[...]) * softplus


def gdn_gates(ba, a_log, dt_bias, *, tt):
    t, two_h = ba.shape
    nh = two_h // 2
    return pl.pallas_call(
        _gates_kernel,
        grid=(t // tt,),
        in_specs=[
            pl.BlockSpec((tt, two_h), lambda i: (i, 0)),
            pl.BlockSpec((1, nh), lambda i: (0, 0)),
            pl.BlockSpec((1, nh), lambda i: (0, 0)),
        ],
        out_specs=[pl.BlockSpec((tt, nh), lambda i: (i, 0)), pl.BlockSpec((tt, nh), lambda i: (i, 0))],
        out_shape=[jax.ShapeDtypeStruct((t, nh), F32), jax.ShapeDtypeStruct((t, nh), F32)],
        compiler_params=_params("parallel"),
        name="gdn_gates",
    )(ba, a_log.reshape(1, nh), dt_bias.reshape(1, nh))


def _prep_kernel(q_ref, k_ref, v_ref, gsel_ref, bsel_ref, grow_ref, wq_ref, l2_ref, u_ref, glw_ref):
    kh = pl.program_id(0)
    tp, nh = gsel_ref.shape
    c = GDN_CHUNK
    dh = GDN_HEAD_DIM
    scale = dh ** -0.5
    lane_h = lax.broadcasted_iota(jnp.int32, (tp, nh), 1)

    def column(ref, head):
        return jnp.sum(jnp.where(lane_h == head, ref[...], 0.0), axis=1, keepdims=True)

    gcol = [column(gsel_ref, 2 * kh + hh) for hh in range(2)]
    bcol = [column(bsel_ref, 2 * kh + hh) for hh in range(2)]
    grow = [grow_ref[hh] for hh in range(2)]

    ii = lax.broadcasted_iota(jnp.int32, (c, 2 * c), 0)
    ll = lax.broadcasted_iota(jnp.int32, (c, 2 * c), 1)
    jj = jnp.bitwise_and(ll, c - 1)
    head_a = ll < c
    incl = ii >= jj
    strict = ii > jj
    wi = lax.broadcasted_iota(jnp.int32, (c, 4 * c), 0)
    wl = lax.broadcasted_iota(jnp.int32, (c, 4 * c), 1)
    eye_w = jnp.where(wi == jnp.bitwise_and(wl, c - 1), 1.0, 0.0).astype(F32)
    br = lax.broadcasted_iota(jnp.int32, (4 * c, 4 * c), 0)
    bc = lax.broadcasted_iota(jnp.int32, (4 * c, 4 * c), 1)
    same_block = (br // c) == (bc // c)
    strict_block = jnp.logical_and(same_block, br != bc)

    def block_diag(packed):
        return jnp.concatenate([packed] * 4, axis=0)

    def packed_matmul(a, b):
        bd = jnp.where(same_block, block_diag(b), 0.0).astype(BF16)
        return _dot(a.astype(BF16), bd)

    n_half = tp // (2 * c)
    lmats = [[] for _ in range(n_half)]
    rhs_blocks = [[] for _ in range(n_half)]
    for half in range(n_half):
        for y in range(2):
            n = 2 * half + y
            r = slice(n * c, (n + 1) * c)
            q = q_ref[r, :] * scale
            k = k_ref[r, :]
            g_pair = jnp.where(head_a, gcol[0][r], gcol[1][r])
            b_pair = jnp.where(head_a, bcol[0][r], bcol[1][r])
            grow_pair = jnp.concatenate([grow[0][:, r], grow[1][:, r]], axis=1)
            tri = jnp.where(incl, grow_pair, 0.0)
            gc = [jnp.sum(jnp.where(head_a, tri, 0.0), axis=1, keepdims=True),
                  jnp.sum(jnp.where(head_a, 0.0, tri), axis=1, keepdims=True)]
            gc_col = jnp.where(head_a, gc[0], gc[1])
            gc_row = jnp.sum(jnp.where(ii <= jj, g_pair, 0.0), axis=0, keepdims=True)
            decay = jnp.where(incl, jnp.exp(jnp.where(incl, gc_col - gc_row, 0.0)), 0.0)
            kb16 = k.astype(BF16)
            gram = _dot_nt(jnp.concatenate([q.astype(BF16), kb16], axis=0),
                           jnp.concatenate([kb16, kb16], axis=0))
            l2_ref[n, 0:c, :] = jnp.where(incl, gram[:c] * decay, 0.0).astype(BF16)
            lmats[half].append(jnp.where(strict, gram[c:] * b_pair * decay, 0.0))
            kups = []
            for hh in range(2):
                g_last = jnp.sum(gcol[hh][r], axis=0, keepdims=True)
                eg = jnp.exp(gc[hh])
                beta = bcol[hh][r]
                kups.append(k * jnp.exp(g_last - gc[hh]))
                wq_ref[n, c:2 * c, hh * dh:(hh + 1) * dh] = (q * eg).astype(BF16)
                glw_ref[n, :, hh * dh:(hh + 1) * dh] = jnp.broadcast_to(jnp.exp(g_last), (8, dh))
                rhs_blocks[half].append(
                    jnp.concatenate([v_ref[r, hh * dh:(hh + 1) * dh] * beta, k * (beta * eg)], axis=1))
            l2_ref[n, c:c + dh, :] = jnp.concatenate(kups, axis=0).T.astype(BF16)

    ps = [-jnp.concatenate(lm, axis=1) for lm in lmats]
    tinvs = [eye_w + p for p in ps]
    m = 1
    while 2 * m < c:
        ps = [packed_matmul(p, p) for p in ps]
        tinvs = [t + packed_matmul(t, p) for t, p in zip(tinvs, ps)]
        m *= 2
    for half in range(n_half):
        rhs_all = jnp.concatenate(rhs_blocks[half], axis=0)
        t_off = jnp.where(strict_block, block_diag(tinvs[half]), 0.0).astype(BF16)
        uw = rhs_all + _dot(t_off, rhs_all.astype(BF16))
        for y in range(2):
            n = 2 * half + y
            for hh in range(2):
                blk = uw[(2 * y + hh) * c:(2 * y + hh + 1) * c]
                u_ref[n * c:(n + 1) * c, hh * dh:(hh + 1) * dh] = blk[:, :dh]
                wq_ref[n, 0:c, hh * dh:(hh + 1) * dh] = blk[:, dh:].astype(BF16)


def gdn_prep(qkv, g, beta, g_rows, *, n_heads, rep, tp):
    assert rep == 2
    t = qkv.shape[0]
    dh = GDN_HEAD_DIM
    c = GDN_CHUNK
    nk = n_heads // rep
    nc = tp // c
    return pl.pallas_call(
        _prep_kernel,
        grid=(nk, t // tp),
        in_specs=[
            pl.BlockSpec((tp, dh), lambda h, i: (i, h)),
            pl.BlockSpec((tp, dh), lambda h, i: (i, nk + h)),
            pl.BlockSpec((tp, 2 * dh), lambda h, i: (i, nk + h)),
            pl.BlockSpec((tp, n_heads), lambda h, i: (i, 0)),
            pl.BlockSpec((tp, n_heads), lambda h, i: (i, 0)),
            pl.BlockSpec((2, 1, tp), lambda h, i: (h, 0, i)),
        ],
        out_specs=[
            pl.BlockSpec((None, nc, 2 * c, 2 * dh), lambda h, i: (h, i, 0, 0)),
            pl.BlockSpec((None, nc, c + dh, 2 * c), lambda h, i: (h, i, 0, 0)),
            pl.BlockSpec((None, tp, 2 * dh), lambda h, i: (h, i, 0)),
            pl.BlockSpec((None, nc, 8, 2 * dh), lambda h, i: (h, i, 0, 0)),
        ],
        out_shape=[
            jax.ShapeDtypeStruct((nk, t // c, 2 * c, 2 * dh), BF16),
            jax.ShapeDtypeStruct((nk, t // c, c + dh, 2 * c), BF16),
            jax.ShapeDtypeStruct((nk, t, 2 * dh), F32),
            jax.ShapeDtypeStruct((nk, t // c, 8, 2 * dh), F32),
        ],
        compiler_params=_params("parallel", "parallel"),
        name="gdn_prep",
    )(qkv, qkv, qkv, g, beta, g_rows)


def _scan_kernel(wq_ref, l2_ref, u_ref, glw_ref, z_ref, gn_ref, o_ref, s_ref):
    pb, nc = wq_ref.shape[0], wq_ref.shape[1]
    gn = gn_ref[...]
    c = GDN_CHUNK
    dh = GDN_HEAD_DIM

    @pl.when(pl.program_id(1) == 0)
    def _():
        s_ref[...] = jnp.zeros_like(s_ref)

    def pair_diag(x):
        z = jnp.zeros((x.shape[0], dh), x.dtype)
        return jnp.concatenate([jnp.concatenate([x[:, :dh], z], axis=1),
                                jnp.concatenate([z, x[:, dh:]], axis=1)], axis=0)

    states = [s_ref[p] for p in range(pb)]
    for n in range(nc):
        r = slice(n * c, (n + 1) * c)
        for p in range(pb):
            ws = _dot(wq_ref[p, n], pair_diag(states[p].astype(BF16)))
            v_new = (u_ref[p, r, :] - ws[:c]).astype(BF16)
            upd = _dot(l2_ref[p, n], pair_diag(v_new))
            o = ws[c:] + upd[:c]
            states[p] = states[p] * glw_ref[p, n, 0:1, :] + upd[c:]
            for hh in range(2):
                cols = slice((2 * p + hh) * dh, (2 * p + hh + 1) * dh)
                gated = _rms_rows(o[:, hh * dh:(hh + 1) * dh], gn) * _silu(z_ref[r, cols])
                o_ref[r, cols] = gated.astype(BF16)
    for p in range(pb):
        s_ref[p] = states[p]


def gdn_scan(wq, l2, u, glw, proj, z_col0, out_norm, *, pb, tp):
    nk, t, two_dh = u.shape
    c = GDN_CHUNK
    nc = tp // c
    width = pb * two_dh
    blk4 = lambda a: pl.BlockSpec((pb, nc) + a.shape[2:], lambda h, i: (h, i, 0, 0))
    return pl.pallas_call(
        _scan_kernel,
        grid=(nk // pb, t // tp),
        in_specs=[blk4(wq), blk4(l2), pl.BlockSpec((pb, tp, two_dh), lambda h, i: (h, i, 0)), blk4(glw),
                  pl.BlockSpec((tp, width), lambda h, i: (i, z_col0 // width + h)),
                  pl.BlockSpec((1, GDN_HEAD_DIM), lambda h, i: (0, 0))],
        out_specs=pl.BlockSpec((tp, width), lambda h, i: (i, h)),
        out_shape=jax.ShapeDtypeStruct((t, nk * two_dh), BF16),
        scratch_shapes=[pltpu.VMEM((pb, two_dh // 2, two_dh), F32)],
        compiler_params=_params("parallel", "arbitrary"),
        name="gdn_scan",
    )(wq, l2, u, glw, proj, out_norm.reshape(1, GDN_HEAD_DIM))


def _moba_kv_kernel(x_ref, g_ref, wk_ref, wvt_ref, kb_ref, vt_ref, mean_ref):
    xn = _rms_rows(x_ref[...], g_ref[...]).astype(BF16)
    k = _dot(xn, wk_ref[...])
    kb_ref[...] = k.astype(BF16)
    for b in range(mean_ref.shape[0]):
        mean_ref[b] = jnp.mean(k[b * MOBA_BLOCK:(b + 1) * MOBA_BLOCK], axis=0, keepdims=True)
    vt_ref[...] = _dot_nt(wvt_ref[...], xn).astype(BF16)


def moba_kv(x, g, wk, wvt, *, tm):
    t, d = x.shape
    kv = wk.shape[1]
    nb = t // MOBA_BLOCK
    bpt = tm // MOBA_BLOCK
    kb, vt, means = pl.pallas_call(
        _moba_kv_kernel,
        grid=(t // tm,),
        in_specs=[
            pl.BlockSpec((tm, d), lambda i: (i, 0)),
            pl.BlockSpec((1, d), lambda i: (0, 0)),
            pl.BlockSpec((d, kv), lambda i: (0, 0)),
            pl.BlockSpec((kv, d), lambda i: (0, 0)),
        ],
        out_specs=[pl.BlockSpec((tm, kv), lambda i: (i, 0)),
                   pl.BlockSpec((kv, tm), lambda i: (0, i)),
                   pl.BlockSpec((bpt, 1, kv), lambda i: (i, 0, 0))],
        out_shape=[jax.ShapeDtypeStruct((t, kv), BF16), jax.ShapeDtypeStruct((kv, t), BF16),
                   jax.ShapeDtypeStruct((nb, 1, kv), F32)],
        compiler_params=_params("parallel"),
        name="moba_kv",
    )(x, g.reshape(1, d), wk, wvt)
    return kb, vt, means.reshape(nb, kv)


def _rel_bucket_table(n_dist):
    n = np.arange(n_dist)
    max_exact = N_REL_BUCKETS // 2
    ratio = np.log(np.maximum(n, max_exact).astype(np.float32) / np.float32(max_exact)) \
        / np.float32(math.log(REL_MAX_DISTANCE / max_exact))
    large = np.minimum(max_exact + (ratio.astype(np.float32) * (N_REL_BUCKETS - max_exact)).astype(np.int32),
                       N_REL_BUCKETS - 1)
    return np.where(n < max_exact, n, large).astype(np.int32)


def _bucket_tiles():
    table = _rel_bucket_table(2 * MOBA_BLOCK)
    a = np.arange(MOBA_BLOCK)[:, None]
    b = np.arange(MOBA_BLOCK)[None, :]
    d_own = b - a
    own = np.where(d_own >= 0, table[np.maximum(d_own, 0)], -1)
    prev = table[MOBA_BLOCK + b - a]
    return np.stack([own, prev]).astype(np.int32)


def _moba_kernel(rb_ref, qt_ref, k_ref, vt_ref, km_ref, bucket_ref, o_ref,
                 bias_ref, sel_ref, m_ref, acc_ref, *, group):
    g = pl.program_id(0)
    i = pl.program_id(1)
    bs = MOBA_BLOCK
    dh = ATT_HEAD_DIM
    nb = km_ref.shape[0]
    nq = group * bs

    @pl.when(i == 0)
    def _():
        for hh in range(group):
            for t in range(2):
                bk = bucket_ref[t]
                bias = jnp.full(bk.shape, MASK_VALUE, F32)
                for b in range(N_REL_BUCKETS):
                    bias = jnp.where(bk == b, rb_ref[b, g * group + hh] * LOG2_E, bias)
                bias_ref[t, :, hh * bs:(hh + 1) * bs] = bias

    qt = jnp.concatenate([qt_ref[hh * dh:(hh + 1) * dh, :] for hh in range(group)], axis=1)

    blk = lax.broadcasted_iota(jnp.int32, (nb, nq), 0)
    gate = jnp.where(blk < i, _dot(km_ref[...].astype(BF16), qt), -jnp.inf)
    sel = jnp.zeros((nb, nq), F32)
    for r in range(MOBA_TOPK):
        mx = jnp.max(gate, axis=0, keepdims=True)
        first = jnp.min(jnp.where(gate == mx, blk, nb), axis=0, keepdims=True)
        pick = blk == first
        sel = jnp.where(jnp.logical_and(pick, r < i), 1.0, sel)
        gate = jnp.where(pick, -jnp.inf, gate)
    sel_ref[...] = sel

    def keys(j, nblk=1):
        return k_ref[pl.ds(pl.multiple_of(j * bs, bs), nblk * bs), :]

    def values_t(j, nblk=1):
        return vt_ref[:, pl.ds(pl.multiple_of(j * bs, bs), nblk * bs)]

    OWN, PREV, FAR = 0, 1, 2
    SLAB = 32
    LOOKAHEAD = 4
    L_ROWS = MOBA_L_ROWS

    def attend(state, units):
        m_all, acc_all = state
        m_cur = [m_all[:, hh * bs:(hh + 1) * bs] for hh in range(group)]
        acc_cur = [acc_all[:, hh * bs:(hh + 1) * bs] for hh in range(group)]
        tasks = [(u, hh) for u in range(len(units)) for hh in range(group)]

        def score(t):
            u, hh = tasks[t]
            j, kinds = units[u]
            return _dot(keys(j, len(kinds)), qt[:, hh * bs:(hh + 1) * bs])

        pending = {t: score(t) for t in range(min(LOOKAHEAD, len(tasks)))}
        for t, (u, hh) in enumerate(tasks):
            if t + LOOKAHEAD < len(tasks):
                pending[t + LOOKAHEAD] = score(t + LOOKAHEAD)
            scores = pending.pop(t)
            j, kinds = units[u]
            cols = slice(hh * bs, (hh + 1) * bs)
            far_bias = rb_ref[N_REL_BUCKETS - 1, g * group + hh] * LOG2_E
            parts = []
            run_max = None
            for b, kind in enumerate(kinds):
                if kind != OWN:
                    row = jnp.where(sel_ref[pl.ds(j + b, 1), cols] > 0.0, far_bias if kind == FAR else 0.0,
                                    MASK_VALUE)
                    row = jnp.broadcast_to(row, (SLAB, bs))
                for r0 in range(0, bs, SLAB):
                    part = scores[b * bs + r0:b * bs + r0 + SLAB]
                    if kind == OWN:
                        part = part + bias_ref[0, r0:r0 + SLAB, cols]
                    elif kind == PREV:
                        part = part + bias_ref[1, r0:r0 + SLAB, cols] + row
                    else:
                        part = part + row
                    parts.append(part)
                    run_max = part if run_max is None else jnp.maximum(run_max, part)
            m_new = jnp.maximum(m_cur[hh], jnp.max(run_max, axis=0, keepdims=True))
            alpha = jnp.exp2(m_cur[hh] - m_new)
            m_slab = jnp.broadcast_to(m_new, (SLAB, bs))
            packed = [jnp.exp2(part - m_slab).astype(BF16) for part in parts]
            vt_ones = jnp.concatenate([values_t(j, len(kinds)), jnp.ones((L_ROWS, len(kinds) * bs), BF16)], axis=0)
            m_cur[hh] = m_new
            acc_cur[hh] = alpha * acc_cur[hh] + _dot(vt_ones, jnp.concatenate(packed, axis=0))
        return jnp.concatenate(m_cur, axis=1), jnp.concatenate(acc_cur, axis=1)

    def load_state():
        return m_ref[...], acc_ref[...]

    def store_state(state):
        m_ref[...], acc_ref[...] = state

    empty = (jnp.full((1, nq), MASK_VALUE, F32), jnp.zeros((dh + L_ROWS, nq), F32))

    n_old = jnp.maximum(i - 1, 0)
    odd = n_old % 2 == 1

    @pl.when(i == 0)
    def _():
        store_state(attend(empty, [(i, (OWN,))]))

    @pl.when(jnp.logical_and(i >= 1, jnp.logical_not(odd)))
    def _():
        store_state(attend(empty, [(i - 1, (PREV, OWN))]))

    @pl.when(odd)
    def _():
        store_state(attend(empty, [(i - 2, (FAR, PREV, OWN))]))

    n_far = n_old - n_old % 2

    def far_units(j0, n_units):
        return [(j0 + 2 * u, (FAR, FAR)) for u in range(n_units)]

    def far_body(jj, carry):
        store_state(attend(load_state(), far_units(4 * jj, 2)))
        return carry

    lax.fori_loop(0, n_far // 4, far_body, 0)

    @pl.when(n_far % 4 == 2)
    def _():
        store_state(attend(load_state(), far_units(n_far - 2, 1)))

    out = acc_ref[0:dh, :] / acc_ref[dh:dh + 1, :]
    for hh in range(group):
        o_ref[:, hh * dh:(hh + 1) * dh] = out[:, hh * bs:(hh + 1) * bs].T.astype(BF16)


def moba_attention(qt, k, vt, k_means, rel_bias, *, n_heads, n_kv_heads):
    dh = ATT_HEAD_DIM
    t = k.shape[0]
    bs = MOBA_BLOCK
    nb = t // bs
    group = n_heads // n_kv_heads
    nq = group * bs
    assert REL_MAX_DISTANCE <= bs
    buckets = jnp.asarray(_bucket_tiles())
    return pl.pallas_call(
        functools.partial(_moba_kernel, group=group),
        grid=(n_kv_heads, nb),
        in_specs=[
            pl.BlockSpec(memory_space=pltpu.SMEM),
            pl.BlockSpec((group * dh, bs), lambda g, i: (g, i)),
            pl.BlockSpec((t, dh), lambda g, i: (0, g)),
            pl.BlockSpec((dh, t), lambda g, i: (g, 0)),
            pl.BlockSpec((nb, dh), lambda g, i: (0, g)),
            pl.BlockSpec((2, bs, bs), lambda g, i: (0, 0, 0)),
        ],
        out_specs=pl.BlockSpec((bs, group * dh), lambda g, i: (i, g)),
        out_shape=jax.ShapeDtypeStruct((t, n_heads * dh), BF16),
        scratch_shapes=[
            pltpu.VMEM((2, bs, nq), F32),
            pltpu.VMEM((nb, nq), F32),
            pltpu.VMEM((1, nq), F32),
            pltpu.VMEM((dh + MOBA_L_ROWS, nq), F32),
        ],
        compiler_params=_params("parallel", "arbitrary"),
        name="moba_attention",
    )(rel_bias, qt, k, vt, k_means, buckets)


def kernel(x, ffn1_norm, ffn1_w_in, ffn1_w_out, mix_norm, ffn2_norm, ffn2_w_in, ffn2_w_out,
           gdn_w_in, gdn_conv_w, gdn_a_log, gdn_dt_bias, gdn_out_norm, gdn_w_out,
           kv_norm, w_kv, moba_w_q, moba_w_o, rel_bias, final_norm):
    batch, seq, d_model = x.shape
    depth = ffn1_norm.shape[0]
    n_a = gdn_w_in.shape[0]
    n_v_heads = gdn_a_log.shape[1]
    value_dim = gdn_w_out.shape[1]
    conv_dim = gdn_conv_w.shape[2]
    key_dim = (conv_dim - value_dim) // 2
    rep = value_dim // key_dim
    n_heads = moba_w_q.shape[2] // ATT_HEAD_DIM
    kv_dim = w_kv.shape[1] // 2
    n_kv_heads = kv_dim // ATT_HEAD_DIM
    bf = lambda a: a.astype(BF16)
    tl = TILES
    gdn_w_in_t = jnp.swapaxes(gdn_w_in, 1, 2)
    gdn_w_out_bf = bf(gdn_w_out)
    moba_w_q_t = bf(jnp.swapaxes(moba_w_q, 1, 2))
    moba_w_o_bf = bf(moba_w_o)

    outs = []
    for bi in range(batch):
        h = x[bi]
        k_nat = vt = k_means = None
        for layer in range(depth):
            if layer == n_a:
                k_nat, vt, k_means = moba_kv(h, kv_norm, bf(w_kv[:, :kv_dim]), bf(w_kv[:, kv_dim:].T),
                                             tm=tl.kv_rows)
            h = ffn(h, ffn1_norm[layer], ffn1_w_in, ffn1_w_out, layer, tm=tl.ffn_rows, tf=tl.ffn_cols)
            if layer < n_a:
                main = conv_dim + value_dim
                n_ba = 2 * n_v_heads
                proj, ba = gdn_in_proj(h, mix_norm[layer], gdn_w_in_t, layer, n_main=main, n_tail=n_ba,
                                       tm=tl.gdn_in_rows, tn=tl.gdn_in_cols)
                beta, g = gdn_gates(ba, gdn_a_log[layer], gdn_dt_bias[layer], tt=tl.gates_rows)
                qkv = gdn_conv(proj, gdn_conv_w[layer], conv_dim=conv_dim, key_dim=key_dim,
                               tt=tl.conv_rows, tc=tl.conv_cols, strip=tl.conv_strip)
                g_rows = g.T.reshape(n_v_heads, 1, seq)
                wq, l2, u, glw = gdn_prep(qkv, g, beta, g_rows, n_heads=n_v_heads, rep=rep, tp=tl.prep_rows)
                gated = gdn_scan(wq, l2, u, glw, proj, conv_dim, gdn_out_norm[layer],
                                 pb=tl.scan_pairs, tp=tl.scan_rows)
                h = matmul_res(gated, gdn_w_out_bf, layer, h, tm=tl.gdn_out_rows, tn=tl.gdn_out_cols,
                               name="gdn_out")
            else:
                j = layer - n_a
                qt = norm_matmul_t(h, mix_norm[layer], moba_w_q_t, j, tm=tl.qt_rows, tn=tl.qt_cols,
                                   name="moba_qt", out_dtype=BF16, out_scale=MOBA_Q_SCALE)
                att = moba_attention(qt, k_nat, vt, k_means, rel_bias,
                                     n_heads=n_heads, n_kv_heads=n_kv_heads)
                h = matmul_res(att, moba_w_o_bf, j, h, tm=tl.moba_out_rows, tn=tl.moba_out_cols,
                               name="moba_out")
            last = layer == depth - 1
            h = ffn(h, ffn2_norm[layer], ffn2_w_in, ffn2_w_out, layer, final_norm if last else None,
                    tm=tl.ffn_rows, tf=tl.ffn_cols)
        outs.append(h)
    return jnp.stack(outs)
```

```python
import functools
import math
from typing import NamedTuple

import numpy as np
import jax
import jax.numpy as jnp
from jax import lax
from jax.experimental import pallas as pl
from jax.experimental.pallas import tpu as pltpu

F32 = jnp.float32
BF16 = jnp.bfloat16

RMS_EPS = 1e-6
GDN_HEAD_DIM = 128
GDN_CHUNK = 64
GDN_CONV_WIDTH = 4
ATT_HEAD_DIM = 128
MOBA_BLOCK = 256
MOBA_TOPK = 3
N_REL_BUCKETS = 32
REL_MAX_DISTANCE = 128
MASK_VALUE = -1e30
LOG2_E = 1.4426950408889634
MOBA_Q_SCALE = ATT_HEAD_DIM ** -0.5 * LOG2_E

V7X_VMEM_BYTES = 64 * 1024 * 1024
V7X_BF16_SUBLANE_TILE = 16
MOBA_L_ROWS = V7X_BF16_SUBLANE_TILE

VMEM_LIMIT_BYTES = V7X_VMEM_BYTES - 6 * 1024 * 1024


class Tiles(NamedTuple):
    ffn_rows: int = 1024
    ffn_cols: int = 512
    gdn_in_rows: int = 2048
    gdn_in_cols: int = 512
    gates_rows: int = 1024
    conv_rows: int = 1024
    conv_cols: int = 1024
    conv_strip: int = 16
    prep_rows: int = 2048
    scan_pairs: int = 16
    scan_rows: int = 256
    gdn_out_rows: int = 512
    gdn_out_cols: int = 1024
    kv_rows: int = 1024
    qt_rows: int = 1024
    qt_cols: int = 1024
    moba_out_rows: int = 1024
    moba_out_cols: int = 1024


TILES = Tiles()


def _params(*sem):
    return pltpu.CompilerParams(dimension_semantics=sem, vmem_limit_bytes=VMEM_LIMIT_BYTES)


def _rms_rows(x, g):
    ms = jnp.mean(x * x, axis=-1, keepdims=True)
    return x * lax.rsqrt(ms + RMS_EPS) * g


def _silu(x):
    return x * jax.nn.sigmoid(x)


def _dot(a, b):
    return jnp.dot(a, b, preferred_element_type=F32)


def _dot_nt(a, b):
    return lax.dot_general(a, b, (((1,), (1,)), ((), ())), preferred_element_type=F32)


def _row_block(tm, d):
    return pl.BlockSpec((tm, d), lambda i, j: (i, 0), pipeline_mode=pl.Buffered(1))


def _gdn_in_kernel(x_ref, g_ref, w_ref, wt_ref, o_ref, ot_ref, xn_ref):
    @pl.when(pl.program_id(1) == 0)
    def _():
        xn_ref[...] = _rms_rows(x_ref[...], g_ref[...]).astype(BF16)
        ot_ref[...] = _dot_nt(xn_ref[...], wt_ref[...].astype(BF16))

    o_ref[...] = _dot_nt(xn_ref[...], w_ref[...].astype(BF16))


def gdn_in_proj(x, g, w_t, layer, *, n_main, n_tail, tm, tn):
    t, d = x.shape
    assert n_main % tn == 0 and n_main % n_tail == 0
    return pl.pallas_call(
        _gdn_in_kernel,
        grid=(t // tm, n_main // tn),
        in_specs=[
            _row_block(tm, d),
            pl.BlockSpec((1, d), lambda i, j: (0, 0)),
            pl.BlockSpec((None, tn, d), lambda i, j: (layer, j, 0)),
            pl.BlockSpec((None, n_tail, d), lambda i, j: (layer, n_main // n_tail, 0)),
        ],
        out_specs=[pl.BlockSpec((tm, tn), lambda i, j: (i, j)),
                   pl.BlockSpec((tm, n_tail), lambda i, j: (i, 0))],
        out_shape=[jax.ShapeDtypeStruct((t, n_main), F32), jax.ShapeDtypeStruct((t, n_tail), F32)],
        scratch_shapes=[pltpu.VMEM((tm, d), BF16)],
        compiler_params=_params("parallel", "arbitrary"),
        name="gdn_in",
    )(x, g.reshape(1, d), w_t, w_t)


def _norm_matmul_t_kernel(x_ref, g_ref, wt_ref, o_ref, xn_ref, *, out_scale):
    @pl.when(pl.program_id(1) == 0)
    def _():
        xn_ref[...] = _rms_rows(x_ref[...], g_ref[...]).astype(BF16)

    y = _dot_nt(wt_ref[...], xn_ref[...])
    if out_scale != 1.0:
        y = y * out_scale
    o_ref[...] = y.astype(o_ref.dtype)


def norm_matmul_t(x, g, wt, layer, *, tm, tn, name, out_dtype=F32, out_scale=1.0):
    t, d = x.shape
    n = wt.shape[1]
    return pl.pallas_call(
        functools.partial(_norm_matmul_t_kernel, out_scale=out_scale),
        grid=(t // tm, n // tn),
        in_specs=[
            _row_block(tm, d),
            pl.BlockSpec((1, d), lambda i, j: (0, 0)),
            pl.BlockSpec((None, tn, d), lambda i, j: (layer, j, 0)),
        ],
        out_specs=pl.BlockSpec((tn, tm), lambda i, j: (j, i)),
        out_shape=jax.ShapeDtypeStruct((n, t), out_dtype),
        scratch_shapes=[pltpu.VMEM((tm, d), BF16)],
        compiler_params=_params("parallel", "arbitrary"),
        name=name,
    )(x, g.reshape(1, d), wt)


def _ffn_kernel(h_ref, g_ref, wg_ref, wu_ref, wo_ref, fg_ref, o_ref, xn_ref, *, final_norm):
    j = pl.program_id(1)
    tm = h_ref.shape[0]
    strip = min(tm, 256)

    @pl.when(j == 0)
    def _():
        for r0 in range(0, tm, strip):
            r = slice(r0, r0 + strip)
            xn_ref[r, :] = _rms_rows(h_ref[r, :], g_ref[...]).astype(BF16)
            o_ref[r, :] = jnp.zeros((strip, o_ref.shape[1]), F32)

    xn = xn_ref[...]
    tf = wg_ref.shape[1]
    half = min(tf, 256)
    acts = []
    for c0 in range(0, tf, half):
        gate = _dot(xn, wg_ref[:, c0:c0 + half].astype(BF16))
        up = _dot(xn, wu_ref[:, c0:c0 + half].astype(BF16))
        acts.append((_silu(gate) * up).astype(BF16))
    act = acts[0] if len(acts) == 1 else jnp.concatenate(acts, axis=1)
    o_ref[...] += _dot(act, wo_ref[...].astype(BF16))

    @pl.when(j == pl.num_programs(1) - 1)
    def _():
        for r0 in range(0, tm, strip):
            r = slice(r0, r0 + strip)
            y = h_ref[r, :] + 0.5 * o_ref[r, :]
            if final_norm:
                y = _rms_rows(y, fg_ref[...])
            o_ref[r, :] = y


def ffn(h, g, w_in, w_out, layer, final_g=None, *, tm, tf):
    t, d = h.shape
    f = w_out.shape[1]
    nf = f // tf
    final_norm = final_g is not None
    fg = (final_g if final_norm else g).reshape(1, d)
    return pl.pallas_call(
        functools.partial(_ffn_kernel, final_norm=final_norm),
        grid=(t // tm, nf),
        in_specs=[
            _row_block(tm, d),
            pl.BlockSpec((1, d), lambda i, j: (0, 0)),
            pl.BlockSpec((None, d, tf), lambda i, j: (layer, 0, j)),
            pl.BlockSpec((None, d, tf), lambda i, j: (layer, 0, j + nf)),
            pl.BlockSpec((None, tf, d), lambda i, j: (layer, j, 0)),
            pl.BlockSpec((1, d), lambda i, j: (0, 0)),
        ],
        out_specs=_row_block(tm, d),
        out_shape=jax.ShapeDtypeStruct((t, d), F32),
        scratch_shapes=[pltpu.VMEM((tm, d), BF16)],
        compiler_params=_params("parallel", "arbitrary"),
        name="ffn",
    )(h, g.reshape(1, d), w_in, w_in, w_out, fg)


def _matmul_res_kernel(a_ref, w_ref, r_ref, o_ref):
    o_ref[...] = r_ref[...] + _dot(a_ref[...], w_ref[...])


def matmul_res(a, w, layer, r, *, tm, tn, name):
    t, k = a.shape
    n = w.shape[2]
    return pl.pallas_call(
        _matmul_res_kernel,
        grid=(t // tm, n // tn),
        in_specs=[
            pl.BlockSpec((tm, k), lambda i, j: (i, 0)),
            pl.BlockSpec((None, k, tn), lambda i, j: (layer, 0, j)),
            pl.BlockSpec((tm, tn), lambda i, j: (i, j)),
        ],
        out_specs=pl.BlockSpec((tm, tn), lambda i, j: (i, j)),
        out_shape=jax.ShapeDtypeStruct((t, n), F32),
        compiler_params=_params("parallel", "arbitrary"),
        name=name,
    )(a, w, r)


def _conv_kernel(prev_ref, cur_ref, w_ref, o_ref, xs_ref, *, n_norm_blocks, strip):
    i = pl.program_id(0)
    j = pl.program_id(1)
    tt, tc = cur_ref.shape
    halo = prev_ref.shape[0]
    xs_ref[0:halo, :] = jnp.where(i == 0, 0.0, prev_ref[...])
    xs_ref[halo:, :] = cur_ref[...]
    w = w_ref[...]

    def conv_strip(r0):
        y = w[GDN_CONV_WIDTH - 1:GDN_CONV_WIDTH] * xs_ref[halo + r0:halo + r0 + strip, :]
        for s in range(1, GDN_CONV_WIDTH):
            tap = GDN_CONV_WIDTH - 1 - s
            y = y + w[tap:tap + 1] * xs_ref[halo + r0 - s:halo + r0 - s + strip, :]
        return _silu(y)

    @pl.when(j < n_norm_blocks)
    def _():
        for r0 in range(0, tt, strip):
            y = conv_strip(r0)
            for hh in range(tc // GDN_HEAD_DIM):
                seg = y[:, hh * GDN_HEAD_DIM:(hh + 1) * GDN_HEAD_DIM]
                ss = jnp.sum(seg * seg, axis=-1, keepdims=True)
                o_ref[r0:r0 + strip, hh * GDN_HEAD_DIM:(hh + 1) * GDN_HEAD_DIM] = seg * lax.rsqrt(ss + RMS_EPS)

    @pl.when(j >= n_norm_blocks)
    def _():
        for r0 in range(0, tt, strip):
            o_ref[r0:r0 + strip, :] = conv_strip(r0)


def gdn_conv(proj, conv_w, *, conv_dim, key_dim, tt, tc, strip):
    t = proj.shape[0]
    halo = 8
    assert halo >= GDN_CONV_WIDTH - 1
    return pl.pallas_call(
        functools.partial(_conv_kernel, n_norm_blocks=2 * key_dim // tc, strip=strip),
        grid=(t // tt, conv_dim // tc),
        in_specs=[
            pl.BlockSpec((halo, tc), lambda i, j: (jnp.maximum(i * (tt // halo) - 1, 0), j)),
            pl.BlockSpec((tt, tc), lambda i, j: (i, j)),
            pl.BlockSpec((GDN_CONV_WIDTH, tc), lambda i, j: (0, j)),
        ],
        out_specs=pl.BlockSpec((tt, tc), lambda i, j: (i, j)),
        out_shape=jax.ShapeDtypeStruct((t, conv_dim), F32),
        scratch_shapes=[pltpu.VMEM((tt + halo, tc), F32)],
        compiler_params=_params("parallel", "parallel"),
        name="gdn_conv",
    )(proj, proj, conv_w)


def _gates_kernel(ba_ref, alog_ref, dtb_ref, beta_ref, g_ref):
    nh = beta_ref.shape[1]
    ba = ba_ref[...]
    beta_ref[...] = jax.nn.sigmoid(ba[:, :nh])
    z = ba[:, nh:] + dtb_ref[...]
    softplus = jnp.maximum(z, 0.0) + jnp.log1p(jnp.exp(-jnp.abs(z)))
    g_ref[...] = -jnp.exp(alog_ref[...]) * softplus


def gdn_gates(ba, a_log, dt_bias, *, tt):
    t, two_h = ba.shape
    nh = two_h // 2
    return pl.pallas_call(
        _gates_kernel,
        grid=(t // tt,),
        in_specs=[
            pl.BlockSpec((tt, two_h), lambda i: (i, 0)),
            pl.BlockSpec((1, nh), lambda i: (0, 0)),
            pl.BlockSpec((1, nh), lambda i: (0, 0)),
        ],
        out_specs=[pl.BlockSpec((tt, nh), lambda i: (i, 0)), pl.BlockSpec((tt, nh), lambda i: (i, 0))],
        out_shape=[jax.ShapeDtypeStruct((t, nh), F32), jax.ShapeDtypeStruct((t, nh), F32)],
        compiler_params=_params("parallel"),
        name="gdn_gates",
    )(ba, a_log.reshape(1, nh), dt_bias.reshape(1, nh))


def _prep_kernel(q_ref, k_ref, v_ref, gsel_ref, bsel_ref, grow_ref, wq_ref, l2_ref, u_ref, glw_ref):
    kh = pl.program_id(0)
    tp, nh = gsel_ref.shape
    c = GDN_CHUNK
    dh = GDN_HEAD_DIM
    scale = dh ** -0.5
    lane_h = lax.broadcasted_iota(jnp.int32, (tp, nh), 1)

    def column(ref, head):
        return jnp.sum(jnp.where(lane_h == head, ref[...], 0.0), axis=1, keepdims=True)

    gcol = [column(gsel_ref, 2 * kh + hh) for hh in range(2)]
    bcol = [column(bsel_ref, 2 * kh + hh) for hh in range(2)]
    grow = [grow_ref[hh] for hh in range(2)]

    ii = lax.broadcasted_iota(jnp.int32, (c, 2 * c), 0)
    ll = lax.broadcasted_iota(jnp.int32, (c, 2 * c), 1)
    jj = jnp.bitwise_and(ll, c - 1)
    head_a = ll < c
    incl = ii >= jj
    strict = ii > jj
    wi = lax.broadcasted_iota(jnp.int32, (c, 4 * c), 0)
    wl = lax.broadcasted_iota(jnp.int32, (c, 4 * c), 1)
    eye_w = jnp.where(wi == jnp.bitwise_and(wl, c - 1), 1.0, 0.0).astype(F32)
    br = lax.broadcasted_iota(jnp.int32, (4 * c, 4 * c), 0)
    bc = lax.broadcasted_iota(jnp.int32, (4 * c, 4 * c), 1)
    same_block = (br // c) == (bc // c)
    strict_block = jnp.logical_and(same_block, br != bc)

    def block_diag(packed):
        return jnp.concatenate([packed] * 4, axis=0)

    def packed_matmul(a, b):
        bd = jnp.where(same_block, block_diag(b), 0.0).astype(BF16)
        return _dot(a.astype(BF16), bd)

    n_half = tp // (2 * c)
    lmats = [[] for _ in range(n_half)]
    rhs_blocks = [[] for _ in range(n_half)]
    for half in range(n_half):
        for y in range(2):
            n = 2 * half + y
            r = slice(n * c, (n + 1) * c)
            q = q_ref[r, :] * scale
            k = k_ref[r, :]
            g_pair = jnp.where(head_a, gcol[0][r], gcol[1][r])
            b_pair = jnp.where(head_a, bcol[0][r], bcol[1][r])
            grow_pair = jnp.concatenate([grow[0][:, r], grow[1][:, r]], axis=1)
            tri = jnp.where(incl, grow_pair, 0.0)
            gc = [jnp.sum(jnp.where(head_a, tri, 0.0), axis=1, keepdims=True),
                  jnp.sum(jnp.where(head_a, 0.0, tri), axis=1, keepdims=True)]
            gc_col = jnp.where(head_a, gc[0], gc[1])
            gc_row = jnp.sum(jnp.where(ii <= jj, g_pair, 0.0), axis=0, keepdims=True)
            decay = jnp.where(incl, jnp.exp(jnp.where(incl, gc_col - gc_row, 0.0)), 0.0)
            kb16 = k.astype(BF16)
            gram = _dot_nt(jnp.concatenate([q.astype(BF16), kb16], axis=0),
                           jnp.concatenate([kb16, kb16], axis=0))
            l2_ref[n, 0:c, :] = jnp.where(incl, gram[:c] * decay, 0.0).astype(BF16)
            lmats[half].append(jnp.where(strict, gram[c:] * b_pair * decay, 0.0))
            kups = []
            for hh in range(2):
                g_last = jnp.sum(gcol[hh][r], axis=0, keepdims=True)
                eg = jnp.exp(gc[hh])
                beta = bcol[hh][r]
                kups.append(k * jnp.exp(g_last - gc[hh]))
                wq_ref[n, c:2 * c, hh * dh:(hh + 1) * dh] = (q * eg).astype(BF16)
                glw_ref[n, :, hh * dh:(hh + 1) * dh] = jnp.broadcast_to(jnp.exp(g_last), (8, dh))
                rhs_blocks[half].append(
                    jnp.concatenate([v_ref[r, hh * dh:(hh + 1) * dh] * beta, k * (beta * eg)], axis=1))
            l2_ref[n, c:c + dh, :] = jnp.concatenate(kups, axis=0).T.astype(BF16)

    lws = [jnp.concatenate(lm, axis=1) for lm in lmats]
    wj = jnp.bitwise_and(wl, c - 1)
    tinvs = [eye_w - jnp.where(wi // 2 == wj // 2, lw, 0.0) for lw in lws]
    s = 2
    while s < c:
        join = jnp.logical_and(wi // (2 * s) == wj // (2 * s), wi // s > wj // s)
        xe = [packed_matmul(x, jnp.where(join, lw, 0.0)) for x, lw in zip(tinvs, lws)]
        tinvs = [x - packed_matmul(y, x) for x, y in zip(tinvs, xe)]
        s *= 2
    for half in range(n_half):
        rhs_all = jnp.concatenate(rhs_blocks[half], axis=0)
        t_off = jnp.where(strict_block, block_diag(tinvs[half]), 0.0).astype(BF16)
        uw = rhs_all + _dot(t_off, rhs_all.astype(BF16))
        for y in range(2):
            n = 2 * half + y
            for hh in range(2):
                blk = uw[(2 * y + hh) * c:(2 * y + hh + 1) * c]
                u_ref[n * c:(n + 1) * c, hh * dh:(hh + 1) * dh] = blk[:, :dh]
                wq_ref[n, 0:c, hh * dh:(hh + 1) * dh] = blk[:, dh:].astype(BF16)


def gdn_prep(qkv, g, beta, g_rows, *, n_heads, rep, tp):
    assert rep == 2
    t = qkv.shape[0]
    dh = GDN_HEAD_DIM
    c = GDN_CHUNK
    nk = n_heads // rep
    nc = tp // c
    return pl.pallas_call(
        _prep_kernel,
        grid=(nk, t // tp),
        in_specs=[
            pl.BlockSpec((tp, dh), lambda h, i: (i, h)),
            pl.BlockSpec((tp, dh), lambda h, i: (i, nk + h)),
            pl.BlockSpec((tp, 2 * dh), lambda h, i: (i, nk + h)),
            pl.BlockSpec((tp, n_heads), lambda h, i: (i, 0)),
            pl.BlockSpec((tp, n_heads), lambda h, i: (i, 0)),
            pl.BlockSpec((2, 1, tp), lambda h, i: (h, 0, i)),
        ],
        out_specs=[
            pl.BlockSpec((None, nc, 2 * c, 2 * dh), lambda h, i: (h, i, 0, 0)),
            pl.BlockSpec((None, nc, c + dh, 2 * c), lambda h, i: (h, i, 0, 0)),
            pl.BlockSpec((None, tp, 2 * dh), lambda h, i: (h, i, 0)),
            pl.BlockSpec((None, nc, 8, 2 * dh), lambda h, i: (h, i, 0, 0)),
        ],
        out_shape=[
            jax.ShapeDtypeStruct((nk, t // c, 2 * c, 2 * dh), BF16),
            jax.ShapeDtypeStruct((nk, t // c, c + dh, 2 * c), BF16),
            jax.ShapeDtypeStruct((nk, t, 2 * dh), F32),
            jax.ShapeDtypeStruct((nk, t // c, 8, 2 * dh), F32),
        ],
        compiler_params=_params("parallel", "parallel"),
        name="gdn_prep",
    )(qkv, qkv, qkv, g, beta, g_rows)


def _scan_kernel(wq_ref, l2_ref, u_ref, glw_ref, z_ref, gn_ref, o_ref, s_ref):
    pb, nc = wq_ref.shape[0], wq_ref.shape[1]
    gn = gn_ref[...]
    c = GDN_CHUNK
    dh = GDN_HEAD_DIM

    @pl.when(pl.program_id(1) == 0)
    def _():
        s_ref[...] = jnp.zeros_like(s_ref)

    def pair_diag(x):
        z = jnp.zeros((x.shape[0], dh), x.dtype)
        return jnp.concatenate([jnp.concatenate([x[:, :dh], z], axis=1),
                                jnp.concatenate([z, x[:, dh:]], axis=1)], axis=0)

    states = [s_ref[p] for p in range(pb)]
    for n in range(nc):
        r = slice(n * c, (n + 1) * c)
        for p in range(pb):
            ws = _dot(wq_ref[p, n], pair_diag(states[p].astype(BF16)))
            v_new = (u_ref[p, r, :] - ws[:c]).astype(BF16)
            upd = _dot(l2_ref[p, n], pair_diag(v_new))
            o = ws[c:] + upd[:c]
            states[p] = states[p] * glw_ref[p, n, 0:1, :] + upd[c:]
            for hh in range(2):
                cols = slice((2 * p + hh) * dh, (2 * p + hh + 1) * dh)
                gated = _rms_rows(o[:, hh * dh:(hh + 1) * dh], gn) * _silu(z_ref[r, cols])
                o_ref[r, cols] = gated.astype(BF16)
    for p in range(pb):
        s_ref[p] = states[p]


def gdn_scan(wq, l2, u, glw, proj, z_col0, out_norm, *, pb, tp):
    nk, t, two_dh = u.shape
    c = GDN_CHUNK
    nc = tp // c
    width = pb * two_dh
    blk4 = lambda a: pl.BlockSpec((pb, nc) + a.shape[2:], lambda h, i: (h, i, 0, 0))
    return pl.pallas_call(
        _scan_kernel,
        grid=(nk // pb, t // tp),
        in_specs=[blk4(wq), blk4(l2), pl.BlockSpec((pb, tp, two_dh), lambda h, i: (h, i, 0)), blk4(glw),
                  pl.BlockSpec((tp, width), lambda h, i: (i, z_col0 // width + h)),
                  pl.BlockSpec((1, GDN_HEAD_DIM), lambda h, i: (0, 0))],
        out_specs=pl.BlockSpec((tp, width), lambda h, i: (i, h)),
        out_shape=jax.ShapeDtypeStruct((t, nk * two_dh), BF16),
        scratch_shapes=[pltpu.VMEM((pb, two_dh // 2, two_dh), F32)],
        compiler_params=_params("parallel", "arbitrary"),
        name="gdn_scan",
    )(wq, l2, u, glw, proj, out_norm.reshape(1, GDN_HEAD_DIM))


def _moba_kv_kernel(x_ref, g_ref, wk_ref, wvt_ref, kb_ref, vt_ref, mean_ref):
    xn = _rms_rows(x_ref[...], g_ref[...]).astype(BF16)
    k = _dot(xn, wk_ref[...])
    kb_ref[...] = k.astype(BF16)
    for b in range(mean_ref.shape[0]):
        mean_ref[b] = jnp.mean(k[b * MOBA_BLOCK:(b + 1) * MOBA_BLOCK], axis=0, keepdims=True)
    vt_ref[...] = _dot_nt(wvt_ref[...], xn).astype(BF16)


def moba_kv(x, g, wk, wvt, *, tm):
    t, d = x.shape
    kv = wk.shape[1]
    nb = t // MOBA_BLOCK
    bpt = tm // MOBA_BLOCK
    kb, vt, means = pl.pallas_call(
        _moba_kv_kernel,
        grid=(t // tm,),
        in_specs=[
            pl.BlockSpec((tm, d), lambda i: (i, 0)),
            pl.BlockSpec((1, d), lambda i: (0, 0)),
            pl.BlockSpec((d, kv), lambda i: (0, 0)),
            pl.BlockSpec((kv, d), lambda i: (0, 0)),
        ],
        out_specs=[pl.BlockSpec((tm, kv), lambda i: (i, 0)),
                   pl.BlockSpec((kv, tm), lambda i: (0, i)),
                   pl.BlockSpec((bpt, 1, kv), lambda i: (i, 0, 0))],
        out_shape=[jax.ShapeDtypeStruct((t, kv), BF16), jax.ShapeDtypeStruct((kv, t), BF16),
                   jax.ShapeDtypeStruct((nb, 1, kv), F32)],
        compiler_params=_params("parallel"),
        name="moba_kv",
    )(x, g.reshape(1, d), wk, wvt)
    return kb, vt, means.reshape(nb, kv)


def _rel_bucket_table(n_dist):
    n = np.arange(n_dist)
    max_exact = N_REL_BUCKETS // 2
    ratio = np.log(np.maximum(n, max_exact).astype(np.float32) / np.float32(max_exact)) \
        / np.float32(math.log(REL_MAX_DISTANCE / max_exact))
    large = np.minimum(max_exact + (ratio.astype(np.float32) * (N_REL_BUCKETS - max_exact)).astype(np.int32),
                       N_REL_BUCKETS - 1)
    return np.where(n < max_exact, n, large).astype(np.int32)


def _bucket_tiles():
    table = _rel_bucket_table(2 * MOBA_BLOCK)
    a = np.arange(MOBA_BLOCK)[:, None]
    b = np.arange(MOBA_BLOCK)[None, :]
    d_own = b - a
    own = np.where(d_own >= 0, table[np.maximum(d_own, 0)], -1)
    prev = table[MOBA_BLOCK + b - a]
    return np.stack([own, prev]).astype(np.int32)


def _moba_kernel(rb_ref, qt_ref, k_ref, vt_ref, km_ref, bucket_ref, o_ref,
                 bias_ref, sel_ref, m_ref, acc_ref, *, group):
    g = pl.program_id(0)
    i = pl.program_id(1)
    bs = MOBA_BLOCK
    dh = ATT_HEAD_DIM
    nb = km_ref.shape[0]
    nq = group * bs

    @pl.when(i == 0)
    def _():
        for hh in range(group):
            for t in range(2):
                bk = bucket_ref[t]
                bias = jnp.full(bk.shape, MASK_VALUE, F32)
                for b in range(N_REL_BUCKETS):
                    bias = jnp.where(bk == b, rb_ref[b, g * group + hh] * LOG2_E, bias)
                bias_ref[t, :, hh * bs:(hh + 1) * bs] = bias

    qt = jnp.concatenate([qt_ref[hh * dh:(hh + 1) * dh, :] for hh in range(group)], axis=1)

    blk = lax.broadcasted_iota(jnp.int32, (nb, nq), 0)
    gate = jnp.where(blk < i, _dot(km_ref[...].astype(BF16), qt), -jnp.inf)
    sel = jnp.zeros((nb, nq), F32)
    for r in range(MOBA_TOPK):
        mx = jnp.max(gate, axis=0, keepdims=True)
        first = jnp.min(jnp.where(gate == mx, blk, nb), axis=0, keepdims=True)
        pick = blk == first
        sel = jnp.where(jnp.logical_and(pick, r < i), 1.0, sel)
        gate = jnp.where(pick, -jnp.inf, gate)
    sel_ref[...] = sel

    def keys(j, nblk=1):
        return k_ref[pl.ds(pl.multiple_of(j * bs, bs), nblk * bs), :]

    def values_t(j, nblk=1):
        return vt_ref[:, pl.ds(pl.multiple_of(j * bs, bs), nblk * bs)]

    OWN, PREV, FAR = 0, 1, 2
    SLAB = 32
    LOOKAHEAD = 4
    L_ROWS = MOBA_L_ROWS

    def attend(state, units):
        m_all, acc_all = state
        m_cur = [m_all[:, hh * bs:(hh + 1) * bs] for hh in range(group)]
        acc_cur = [acc_all[:, hh * bs:(hh + 1) * bs] for hh in range(group)]
        tasks = [(u, hh) for u in range(len(units)) for hh in range(group)]

        def score(t):
            u, hh = tasks[t]
            j, kinds = units[u]
            return _dot(keys(j, len(kinds)), qt[:, hh * bs:(hh + 1) * bs])

        pending = {t: score(t) for t in range(min(LOOKAHEAD, len(tasks)))}
        for t, (u, hh) in enumerate(tasks):
            if t + LOOKAHEAD < len(tasks):
                pending[t + LOOKAHEAD] = score(t + LOOKAHEAD)
            scores = pending.pop(t)
            j, kinds = units[u]
            cols = slice(hh * bs, (hh + 1) * bs)
            far_bias = rb_ref[N_REL_BUCKETS - 1, g * group + hh] * LOG2_E
            parts = []
            run_max = None
            for b, kind in enumerate(kinds):
                if kind != OWN:
                    row = jnp.where(sel_ref[pl.ds(j + b, 1), cols] > 0.0, far_bias if kind == FAR else 0.0,
                                    MASK_VALUE)
                    row = jnp.broadcast_to(row, (SLAB, bs))
                for r0 in range(0, bs, SLAB):
                    part = scores[b * bs + r0:b * bs + r0 + SLAB]
                    if kind == OWN:
                        part = part + bias_ref[0, r0:r0 + SLAB, cols]
                    elif kind == PREV:
                        part = part + bias_ref[1, r0:r0 + SLAB, cols] + row
                    else:
                        part = part + row
                    parts.append(part)
                    run_max = part if run_max is None else jnp.maximum(run_max, part)
            m_new = jnp.maximum(m_cur[hh], jnp.max(run_max, axis=0, keepdims=True))
            alpha = jnp.exp2(m_cur[hh] - m_new)
            m_slab = jnp.broadcast_to(m_new, (SLAB, bs))
            packed = [jnp.exp2(part - m_slab).astype(BF16) for part in parts]
            vt_ones = jnp.concatenate([values_t(j, len(kinds)), jnp.ones((L_ROWS, len(kinds) * bs), BF16)], axis=0)
            m_cur[hh] = m_new
            acc_cur[hh] = alpha * acc_cur[hh] + _dot(vt_ones, jnp.concatenate(packed, axis=0))
        return jnp.concatenate(m_cur, axis=1), jnp.concatenate(acc_cur, axis=1)

    def load_state():
        return m_ref[...], acc_ref[...]

    def store_state(state):
        m_ref[...], acc_ref[...] = state

    empty = (jnp.full((1, nq), MASK_VALUE, F32), jnp.zeros((dh + L_ROWS, nq), F32))

    n_old = jnp.maximum(i - 1, 0)
    odd = n_old % 2 == 1

    @pl.when(i == 0)
    def _():
        store_state(attend(empty, [(i, (OWN,))]))

    @pl.when(jnp.logical_and(i >= 1, jnp.logical_not(odd)))
    def _():
        store_state(attend(empty, [(i - 1, (PREV, OWN))]))

    @pl.when(odd)
    def _():
        store_state(attend(empty, [(i - 2, (FAR, PREV, OWN))]))

    n_far = n_old - n_old % 2

    def far_units(j0, n_units):
        return [(j0 + 2 * u, (FAR, FAR)) for u in range(n_units)]

    def far_body(jj, carry):
        store_state(attend(load_state(), far_units(4 * jj, 2)))
        return carry

    lax.fori_loop(0, n_far // 4, far_body, 0)

    @pl.when(n_far % 4 == 2)
    def _():
        store_state(attend(load_state(), far_units(n_far - 2, 1)))

    out = acc_ref[0:dh, :] / acc_ref[dh:dh + 1, :]
    for hh in range(group):
        o_ref[:, hh * dh:(hh + 1) * dh] = out[:, hh * bs:(hh + 1) * bs].T.astype(BF16)


def moba_attention(qt, k, vt, k_means, rel_bias, *, n_heads, n_kv_heads):
    dh = ATT_HEAD_DIM
    t = k.shape[0]
    bs = MOBA_BLOCK
    nb = t // bs
    group = n_heads // n_kv_heads
    nq = group * bs
    assert REL_MAX_DISTANCE <= bs
    buckets = jnp.asarray(_bucket_tiles())
    return pl.pallas_call(
        functools.partial(_moba_kernel, group=group),
        grid=(n_kv_heads, nb),
        in_specs=[
            pl.BlockSpec(memory_space=pltpu.SMEM),
            pl.BlockSpec((group * dh, bs), lambda g, i: (g, i)),
            pl.BlockSpec((t, dh), lambda g, i: (0, g)),
            pl.BlockSpec((dh, t), lambda g, i: (g, 0)),
            pl.BlockSpec((nb, dh), lambda g, i: (0, g)),
            pl.BlockSpec((2, bs, bs), lambda g, i: (0, 0, 0)),
        ],
        out_specs=pl.BlockSpec((bs, group * dh), lambda g, i: (i, g)),
        out_shape=jax.ShapeDtypeStruct((t, n_heads * dh), BF16),
        scratch_shapes=[
            pltpu.VMEM((2, bs, nq), F32),
            pltpu.VMEM((nb, nq), F32),
            pltpu.VMEM((1, nq), F32),
            pltpu.VMEM((dh + MOBA_L_ROWS, nq), F32),
        ],
        compiler_params=_params("parallel", "arbitrary"),
        name="moba_attention",
    )(rel_bias, qt, k, vt, k_means, buckets)


def kernel(x, ffn1_norm, ffn1_w_in, ffn1_w_out, mix_norm, ffn2_norm, ffn2_w_in, ffn2_w_out,
           gdn_w_in, gdn_conv_w, gdn_a_log, gdn_dt_bias, gdn_out_norm, gdn_w_out,
           kv_norm, w_kv, moba_w_q, moba_w_o, rel_bias, final_norm):
    batch, seq, d_model = x.shape
    depth = ffn1_norm.shape[0]
    n_a = gdn_w_in.shape[0]
    n_v_heads = gdn_a_log.shape[1]
    value_dim = gdn_w_out.shape[1]
    conv_dim = gdn_conv_w.shape[2]
    key_dim = (conv_dim - value_dim) // 2
    rep = value_dim // key_dim
    n_heads = moba_w_q.shape[2] // ATT_HEAD_DIM
    kv_dim = w_kv.shape[1] // 2
    n_kv_heads = kv_dim // ATT_HEAD_DIM
    bf = lambda a: a.astype(BF16)
    tl = TILES
    gdn_w_in_t = jnp.swapaxes(gdn_w_in, 1, 2)
    gdn_w_out_bf = bf(gdn_w_out)
    moba_w_q_t = bf(jnp.swapaxes(moba_w_q, 1, 2))
    moba_w_o_bf = bf(moba_w_o)

    outs = []
    for bi in range(batch):
        h = x[bi]
        k_nat = vt = k_means = None
        for layer in range(depth):
            if layer == n_a:
                k_nat, vt, k_means = moba_kv(h, kv_norm, bf(w_kv[:, :kv_dim]), bf(w_kv[:, kv_dim:].T),
                                             tm=tl.kv_rows)
            h = ffn(h, ffn1_norm[layer], ffn1_w_in, ffn1_w_out, layer, tm=tl.ffn_rows, tf=tl.ffn_cols)
            if layer < n_a:
                main = conv_dim + value_dim
                n_ba = 2 * n_v_heads
                proj, ba = gdn_in_proj(h, mix_norm[layer], gdn_w_in_t, layer, n_main=main, n_tail=n_ba,
                                       tm=tl.gdn_in_rows, tn=tl.gdn_in_cols)
                beta, g = gdn_gates(ba, gdn_a_log[layer], gdn_dt_bias[layer], tt=tl.gates_rows)
                qkv = gdn_conv(proj, gdn_conv_w[layer], conv_dim=conv_dim, key_dim=key_dim,
                               tt=tl.conv_rows, tc=tl.conv_cols, strip=tl.conv_strip)
                g_rows = g.T.reshape(n_v_heads, 1, seq)
                wq, l2, u, glw = gdn_prep(qkv, g, beta, g_rows, n_heads=n_v_heads, rep=rep, tp=tl.prep_rows)
                gated = gdn_scan(wq, l2, u, glw, proj, conv_dim, gdn_out_norm[layer],
                                 pb=tl.scan_pairs, tp=tl.scan_rows)
                h = matmul_res(gated, gdn_w_out_bf, layer, h, tm=tl.gdn_out_rows, tn=tl.gdn_out_cols,
                               name="gdn_out")
            else:
                j = layer - n_a
                qt = norm_matmul_t(h, mix_norm[layer], moba_w_q_t, j, tm=tl.qt_rows, tn=tl.qt_cols,
                                   name="moba_qt", out_dtype=BF16, out_scale=MOBA_Q_SCALE)
                att = moba_attention(qt, k_nat, vt, k_means, rel_bias,
                                     n_heads=n_heads, n_kv_heads=n_kv_heads)
                h = matmul_res(att, moba_w_o_bf, j, h, tm=tl.moba_out_rows, tn=tl.moba_out_cols,
                               name="moba_out")
            last = layer == depth - 1
            h = ffn(h, ffn2_norm[layer], ffn2_w_in, ffn2_w_out, layer, final_norm if last else None,
                    tm=tl.ffn_rows, tf=tl.ffn_cols)
        outs.append(h)
    return jnp.stack(outs)
```

```python
import functools
import math
from typing import NamedTuple

import numpy as np
import jax
import jax.numpy as jnp
from jax import lax
from jax.experimental import pallas as pl
from jax.experimental.pallas import tpu as pltpu

F32 = jnp.float32
BF16 = jnp.bfloat16

RMS_EPS = 1e-6
GDN_HEAD_DIM = 128
GDN_CHUNK = 64
GDN_CONV_WIDTH = 4
ATT_HEAD_DIM = 128
MOBA_BLOCK = 256
MOBA_TOPK = 3
N_REL_BUCKETS = 32
REL_MAX_DISTANCE = 128
MASK_VALUE = -1e30
LOG2_E = 1.4426950408889634
MOBA_Q_SCALE = ATT_HEAD_DIM ** -0.5 * LOG2_E

V7X_VMEM_BYTES = 64 * 1024 * 1024
V7X_BF16_SUBLANE_TILE = 16
MOBA_L_ROWS = V7X_BF16_SUBLANE_TILE

VMEM_LIMIT_BYTES = V7X_VMEM_BYTES - 6 * 1024 * 1024


class Tiles(NamedTuple):
    ffn_rows: int = 1024
    ffn_cols: int = 512
    gdn_in_rows: int = 2048
    gdn_in_cols: int = 512
    gates_rows: int = 1024
    conv_rows: int = 1024
    conv_cols: int = 1024
    conv_strip: int = 16
    prep_rows: int = 2048
    scan_pairs: int = 16
    scan_rows: int = 256
    gdn_out_rows: int = 1024
    gdn_out_cols: int = 1024
    kv_rows: int = 1024
    qt_rows: int = 2048
    qt_cols: int = 1024
    moba_out_rows: int = 1024
    moba_out_cols: int = 1024


TILES = Tiles()


def _params(*sem):
    return pltpu.CompilerParams(dimension_semantics=sem, vmem_limit_bytes=VMEM_LIMIT_BYTES)


def _rms_rows(x, g):
    ms = jnp.mean(x * x, axis=-1, keepdims=True)
    return x * lax.rsqrt(ms + RMS_EPS) * g


def _silu(x):
    return x * jax.nn.sigmoid(x)


def _dot(a, b):
    return jnp.dot(a, b, preferred_element_type=F32)


def _dot_nt(a, b):
    return lax.dot_general(a, b, (((1,), (1,)), ((), ())), preferred_element_type=F32)


def _row_block(tm, d):
    return pl.BlockSpec((tm, d), lambda i, j: (i, 0), pipeline_mode=pl.Buffered(1))


def _gdn_in_kernel(x_ref, g_ref, w_ref, wt_ref, o_ref, ot_ref, xn_ref):
    @pl.when(pl.program_id(1) == 0)
    def _():
        xn_ref[...] = _rms_rows(x_ref[...], g_ref[...]).astype(BF16)
        ot_ref[...] = _dot_nt(xn_ref[...], wt_ref[...].astype(BF16))

    o_ref[...] = _dot_nt(xn_ref[...], w_ref[...].astype(BF16))


def gdn_in_proj(x, g, w_t, layer, *, n_main, n_tail, tm, tn):
    t, d = x.shape
    assert n_main % tn == 0 and n_main % n_tail == 0
    return pl.pallas_call(
        _gdn_in_kernel,
        grid=(t // tm, n_main // tn),
        in_specs=[
            _row_block(tm, d),
            pl.BlockSpec((1, d), lambda i, j: (0, 0)),
            pl.BlockSpec((None, tn, d), lambda i, j: (layer, j, 0)),
            pl.BlockSpec((None, n_tail, d), lambda i, j: (layer, n_main // n_tail, 0)),
        ],
        out_specs=[pl.BlockSpec((tm, tn), lambda i, j: (i, j)),
                   pl.BlockSpec((tm, n_tail), lambda i, j: (i, 0))],
        out_shape=[jax.ShapeDtypeStruct((t, n_main), F32), jax.ShapeDtypeStruct((t, n_tail), F32)],
        scratch_shapes=[pltpu.VMEM((tm, d), BF16)],
        compiler_params=_params("parallel", "arbitrary"),
        name="gdn_in",
    )(x, g.reshape(1, d), w_t, w_t)


def _norm_matmul_t_kernel(x_ref, g_ref, wt_ref, o_ref, xn_ref, *, out_scale):
    @pl.when(pl.program_id(1) == 0)
    def _():
        xn_ref[...] = _rms_rows(x_ref[...], g_ref[...]).astype(BF16)

    y = _dot_nt(wt_ref[...], xn_ref[...])
    if out_scale != 1.0:
        y = y * out_scale
    o_ref[...] = y.astype(o_ref.dtype)


def norm_matmul_t(x, g, wt, layer, *, tm, tn, name, out_dtype=F32, out_scale=1.0):
    t, d = x.shape
    n = wt.shape[1]
    return pl.pallas_call(
        functools.partial(_norm_matmul_t_kernel, out_scale=out_scale),
        grid=(t // tm, n // tn),
        in_specs=[
            _row_block(tm, d),
            pl.BlockSpec((1, d), lambda i, j: (0, 0)),
            pl.BlockSpec((None, tn, d), lambda i, j: (layer, j, 0)),
        ],
        out_specs=pl.BlockSpec((tn, tm), lambda i, j: (j, i)),
        out_shape=jax.ShapeDtypeStruct((n, t), out_dtype),
        scratch_shapes=[pltpu.VMEM((tm, d), BF16)],
        compiler_params=_params("parallel", "arbitrary"),
        name=name,
    )(x, g.reshape(1, d), wt)


def _ffn_kernel(h_ref, g_ref, wg_ref, wu_ref, wo_ref, fg_ref, o_ref, xn_ref, *, final_norm):
    j = pl.program_id(1)
    tm = h_ref.shape[0]
    strip = min(tm, 256)

    @pl.when(j == 0)
    def _():
        for r0 in range(0, tm, strip):
            r = slice(r0, r0 + strip)
            xn_ref[r, :] = _rms_rows(h_ref[r, :], g_ref[...]).astype(BF16)
            o_ref[r, :] = jnp.zeros((strip, o_ref.shape[1]), F32)

    xn = xn_ref[...]
    tf = wg_ref.shape[1]
    half = min(tf, 256)
    acts = []
    for c0 in range(0, tf, half):
        gate = _dot(xn, wg_ref[:, c0:c0 + half].astype(BF16))
        up = _dot(xn, wu_ref[:, c0:c0 + half].astype(BF16))
        acts.append((_silu(gate) * up).astype(BF16))
    act = acts[0] if len(acts) == 1 else jnp.concatenate(acts, axis=1)
    o_ref[...] += _dot(act, wo_ref[...].astype(BF16))

    @pl.when(j == pl.num_programs(1) - 1)
    def _():
        for r0 in range(0, tm, strip):
            r = slice(r0, r0 + strip)
            y = h_ref[r, :] + 0.5 * o_ref[r, :]
            if final_norm:
                y = _rms_rows(y, fg_ref[...])
            o_ref[r, :] = y


def ffn(h, g, w_in, w_out, layer, final_g=None, *, tm, tf):
    t, d = h.shape
    f = w_out.shape[1]
    nf = f // tf
    final_norm = final_g is not None
    fg = (final_g if final_norm else g).reshape(1, d)
    return pl.pallas_call(
        functools.partial(_ffn_kernel, final_norm=final_norm),
        grid=(t // tm, nf),
        in_specs=[
            _row_block(tm, d),
            pl.BlockSpec((1, d), lambda i, j: (0, 0)),
            pl.BlockSpec((None, d, tf), lambda i, j: (layer, 0, j)),
            pl.BlockSpec((None, d, tf), lambda i, j: (layer, 0, j + nf)),
            pl.BlockSpec((None, tf, d), lambda i, j: (layer, j, 0)),
            pl.BlockSpec((1, d), lambda i, j: (0, 0)),
        ],
        out_specs=_row_block(tm, d),
        out_shape=jax.ShapeDtypeStruct((t, d), F32),
        scratch_shapes=[pltpu.VMEM((tm, d), BF16)],
        compiler_params=_params("parallel", "arbitrary"),
        name="ffn",
    )(h, g.reshape(1, d), w_in, w_in, w_out, fg)


def _matmul_res_kernel(a_ref, w_ref, r_ref, o_ref):
    o_ref[...] = r_ref[...] + _dot(a_ref[...], w_ref[...])


def matmul_res(a, w, layer, r, *, tm, tn, name):
    t, k = a.shape
    n = w.shape[2]
    return pl.pallas_call(
        _matmul_res_kernel,
        grid=(t // tm, n // tn),
        in_specs=[
            pl.BlockSpec((tm, k), lambda i, j: (i, 0)),
            pl.BlockSpec((None, k, tn), lambda i, j: (layer, 0, j)),
            pl.BlockSpec((tm, tn), lambda i, j: (i, j)),
        ],
        out_specs=pl.BlockSpec((tm, tn), lambda i, j: (i, j)),
        out_shape=jax.ShapeDtypeStruct((t, n), F32),
        compiler_params=_params("parallel", "arbitrary"),
        name=name,
    )(a, w, r)


def _conv_kernel(prev_ref, cur_ref, w_ref, o_ref, xs_ref, *, n_norm_blocks, strip):
    i = pl.program_id(0)
    j = pl.program_id(1)
    tt, tc = cur_ref.shape
    halo = prev_ref.shape[0]
    xs_ref[0:halo, :] = jnp.where(i == 0, 0.0, prev_ref[...])
    xs_ref[halo:, :] = cur_ref[...]
    w = w_ref[...]

    def conv_strip(r0):
        y = w[GDN_CONV_WIDTH - 1:GDN_CONV_WIDTH] * xs_ref[halo + r0:halo + r0 + strip, :]
        for s in range(1, GDN_CONV_WIDTH):
            tap = GDN_CONV_WIDTH - 1 - s
            y = y + w[tap:tap + 1] * xs_ref[halo + r0 - s:halo + r0 - s + strip, :]
        return _silu(y)

    @pl.when(j < n_norm_blocks)
    def _():
        for r0 in range(0, tt, strip):
            y = conv_strip(r0)
            for hh in range(tc // GDN_HEAD_DIM):
                seg = y[:, hh * GDN_HEAD_DIM:(hh + 1) * GDN_HEAD_DIM]
                ss = jnp.sum(seg * seg, axis=-1, keepdims=True)
                o_ref[r0:r0 + strip, hh * GDN_HEAD_DIM:(hh + 1) * GDN_HEAD_DIM] = seg * lax.rsqrt(ss + RMS_EPS)

    @pl.when(j >= n_norm_blocks)
    def _():
        for r0 in range(0, tt, strip):
            o_ref[r0:r0 + strip, :] = conv_strip(r0)


def gdn_conv(proj, conv_w, *, conv_dim, key_dim, tt, tc, strip):
    t = proj.shape[0]
    halo = 8
    assert halo >= GDN_CONV_WIDTH - 1
    return pl.pallas_call(
        functools.partial(_conv_kernel, n_norm_blocks=2 * key_dim // tc, strip=strip),
        grid=(t // tt, conv_dim // tc),
        in_specs=[
            pl.BlockSpec((halo, tc), lambda i, j: (jnp.maximum(i * (tt // halo) - 1, 0), j)),
            pl.BlockSpec((tt, tc), lambda i, j: (i, j)),
            pl.BlockSpec((GDN_CONV_WIDTH, tc), lambda i, j: (0, j)),
        ],
        out_specs=pl.BlockSpec((tt, tc), lambda i, j: (i, j)),
        out_shape=jax.ShapeDtypeStruct((t, conv_dim), F32),
        scratch_shapes=[pltpu.VMEM((tt + halo, tc), F32)],
        compiler_params=_params("parallel", "parallel"),
        name="gdn_conv",
    )(proj, proj, conv_w)


def _gates_kernel(ba_ref, alog_ref, dtb_ref, beta_ref, g_ref):
    nh = beta_ref.shape[1]
    ba = ba_ref[...]
    beta_ref[...] = jax.nn.sigmoid(ba[:, :nh])
    z = ba[:, nh:] + dtb_ref[...]
    softplus = jnp.maximum(z, 0.0) + jnp.log1p(jnp.exp(-jnp.abs(z)))
    g_ref[...] = -jnp.exp(alog_ref[...]) * softplus


def gdn_gates(ba, a_log, dt_bias, *, tt):
    t, two_h = ba.shape
    nh = two_h // 2
    return pl.pallas_call(
        _gates_kernel,
        grid=(t // tt,),
        in_specs=[
            pl.BlockSpec((tt, two_h), lambda i: (i, 0)),
            pl.BlockSpec((1, nh), lambda i: (0, 0)),
            pl.BlockSpec((1, nh), lambda i: (0, 0)),
        ],
        out_specs=[pl.BlockSpec((tt, nh), lambda i: (i, 0)), pl.BlockSpec((tt, nh), lambda i: (i, 0))],
        out_shape=[jax.ShapeDtypeStruct((t, nh), F32), jax.ShapeDtypeStruct((t, nh), F32)],
        compiler_params=_params("parallel"),
        name="gdn_gates",
    )(ba, a_log.reshape(1, nh), dt_bias.reshape(1, nh))


def _prep_kernel(q_ref, k_ref, v_ref, gsel_ref, bsel_ref, grow_ref, wq_ref, l2_ref, u_ref, glw_ref):
    kh = pl.program_id(0)
    tp, nh = gsel_ref.shape
    c = GDN_CHUNK
    dh = GDN_HEAD_DIM
    scale = dh ** -0.5
    lane_h = lax.broadcasted_iota(jnp.int32, (tp, nh), 1)

    def column(ref, head):
        return jnp.sum(jnp.where(lane_h == head, ref[...], 0.0), axis=1, keepdims=True)

    gcol = [column(gsel_ref, 2 * kh + hh) for hh in range(2)]
    bcol = [column(bsel_ref, 2 * kh + hh) for hh in range(2)]
    grow = [grow_ref[hh] for hh in range(2)]

    ii = lax.broadcasted_iota(jnp.int32, (c, 2 * c), 0)
    ll = lax.broadcasted_iota(jnp.int32, (c, 2 * c), 1)
    jj = jnp.bitwise_and(ll, c - 1)
    head_a = ll < c
    incl = ii >= jj
    strict = ii > jj
    wi = lax.broadcasted_iota(jnp.int32, (c, 4 * c), 0)
    wl = lax.broadcasted_iota(jnp.int32, (c, 4 * c), 1)
    eye_w = jnp.where(wi == jnp.bitwise_and(wl, c - 1), 1.0, 0.0).astype(F32)
    br = lax.broadcasted_iota(jnp.int32, (4 * c, 4 * c), 0)
    bc = lax.broadcasted_iota(jnp.int32, (4 * c, 4 * c), 1)
    same_block = (br // c) == (bc // c)
    strict_block = jnp.logical_and(same_block, br != bc)

    def block_diag(packed):
        return jnp.concatenate([packed] * 4, axis=0)

    def packed_matmul(a, b):
        bd = jnp.where(same_block, block_diag(b), 0.0).astype(BF16)
        return _dot(a.astype(BF16), bd)

    n_half = tp // (2 * c)
    lmats = [[] for _ in range(n_half)]
    rhs_blocks = [[] for _ in range(n_half)]
    for half in range(n_half):
        for y in range(2):
            n = 2 * half + y
            r = slice(n * c, (n + 1) * c)
            q = q_ref[r, :] * scale
            k = k_ref[r, :]
            g_pair = jnp.where(head_a, gcol[0][r], gcol[1][r])
            b_pair = jnp.where(head_a, bcol[0][r], bcol[1][r])
            grow_pair = jnp.concatenate([grow[0][:, r], grow[1][:, r]], axis=1)
            tri = jnp.where(incl, grow_pair, 0.0)
            gc = [jnp.sum(jnp.where(head_a, tri, 0.0), axis=1, keepdims=True),
                  jnp.sum(jnp.where(head_a, 0.0, tri), axis=1, keepdims=True)]
            gc_col = jnp.where(head_a, gc[0], gc[1])
            gc_row = jnp.sum(jnp.where(ii <= jj, g_pair, 0.0), axis=0, keepdims=True)
            decay = jnp.where(incl, jnp.exp(jnp.where(incl, gc_col - gc_row, 0.0)), 0.0)
            kb16 = k.astype(BF16)
            gram = _dot_nt(jnp.concatenate([q.astype(BF16), kb16], axis=0),
                           jnp.concatenate([kb16, kb16], axis=0))
            l2_ref[n, 0:c, :] = jnp.where(incl, gram[:c] * decay, 0.0).astype(BF16)
            lmats[half].append(jnp.where(strict, gram[c:] * b_pair * decay, 0.0))
            kups = []
            for hh in range(2):
                g_last = jnp.sum(gcol[hh][r], axis=0, keepdims=True)
                eg = jnp.exp(gc[hh])
                beta = bcol[hh][r]
                kups.append(k * jnp.exp(g_last - gc[hh]))
                wq_ref[n, c:2 * c, hh * dh:(hh + 1) * dh] = (q * eg).astype(BF16)
                glw_ref[n, :, hh * dh:(hh + 1) * dh] = jnp.broadcast_to(jnp.exp(g_last), (8, dh))
                rhs_blocks[half].append(
                    jnp.concatenate([v_ref[r, hh * dh:(hh + 1) * dh] * beta, k * (beta * eg)], axis=1))
            l2_ref[n, c:c + dh, :] = jnp.concatenate(kups, axis=0).T.astype(BF16)

    lws = [jnp.concatenate(lm, axis=1) for lm in lmats]
    wj = jnp.bitwise_and(wl, c - 1)
    tinvs = [eye_w - jnp.where(wi // 2 == wj // 2, lw, 0.0) for lw in lws]
    s = 2
    while s < c:
        join = jnp.logical_and(wi // (2 * s) == wj // (2 * s), wi // s > wj // s)
        xe = [packed_matmul(x, jnp.where(join, lw, 0.0)) for x, lw in zip(tinvs, lws)]
        tinvs = [x - packed_matmul(y, x) for x, y in zip(tinvs, xe)]
        s *= 2
    for half in range(n_half):
        rhs_all = jnp.concatenate(rhs_blocks[half], axis=0)
        t_off = jnp.where(strict_block, block_diag(tinvs[half]), 0.0).astype(BF16)
        uw = rhs_all + _dot(t_off, rhs_all.astype(BF16))
        for y in range(2):
            n = 2 * half + y
            for hh in range(2):
                blk = uw[(2 * y + hh) * c:(2 * y + hh + 1) * c]
                u_ref[n * c:(n + 1) * c, hh * dh:(hh + 1) * dh] = blk[:, :dh]
                wq_ref[n, 0:c, hh * dh:(hh + 1) * dh] = blk[:, dh:].astype(BF16)


def gdn_prep(qkv, g, beta, g_rows, *, n_heads, rep, tp):
    assert rep == 2
    t = qkv.shape[0]
    dh = GDN_HEAD_DIM
    c = GDN_CHUNK
    nk = n_heads // rep
    nc = tp // c
    return pl.pallas_call(
        _prep_kernel,
        grid=(nk, t // tp),
        in_specs=[
            pl.BlockSpec((tp, dh), lambda h, i: (i, h)),
            pl.BlockSpec((tp, dh), lambda h, i: (i, nk + h)),
            pl.BlockSpec((tp, 2 * dh), lambda h, i: (i, nk + h)),
            pl.BlockSpec((tp, n_heads), lambda h, i: (i, 0)),
            pl.BlockSpec((tp, n_heads), lambda h, i: (i, 0)),
            pl.BlockSpec((2, 1, tp), lambda h, i: (h, 0, i)),
        ],
        out_specs=[
            pl.BlockSpec((None, nc, 2 * c, 2 * dh), lambda h, i: (h, i, 0, 0)),
            pl.BlockSpec((None, nc, c + dh, 2 * c), lambda h, i: (h, i, 0, 0)),
            pl.BlockSpec((None, tp, 2 * dh), lambda h, i: (h, i, 0)),
            pl.BlockSpec((None, nc, 8, 2 * dh), lambda h, i: (h, i, 0, 0)),
        ],
        out_shape=[
            jax.ShapeDtypeStruct((nk, t // c, 2 * c, 2 * dh), BF16),
            jax.ShapeDtypeStruct((nk, t // c, c + dh, 2 * c), BF16),
            jax.ShapeDtypeStruct((nk, t, 2 * dh), F32),
            jax.ShapeDtypeStruct((nk, t // c, 8, 2 * dh), F32),
        ],
        compiler_params=_params("parallel", "parallel"),
        name="gdn_prep",
    )(qkv, qkv, qkv, g, beta, g_rows)


def _scan_kernel(wq_ref, l2_ref, u_ref, glw_ref, z_ref, gn_ref, o_ref, s_ref):
    pb, nc = wq_ref.shape[0], wq_ref.shape[1]
    gn = gn_ref[...]
    c = GDN_CHUNK
    dh = GDN_HEAD_DIM

    @pl.when(pl.program_id(1) == 0)
    def _():
        s_ref[...] = jnp.zeros_like(s_ref)

    def pair_diag(x):
        z = jnp.zeros((x.shape[0], dh), x.dtype)
        return jnp.concatenate([jnp.concatenate([x[:, :dh], z], axis=1),
                                jnp.concatenate([z, x[:, dh:]], axis=1)], axis=0)

    states = [s_ref[p] for p in range(pb)]
    for n in range(nc):
        r = slice(n * c, (n + 1) * c)
        for p in range(pb):
            ws = _dot(wq_ref[p, n], pair_diag(states[p].astype(BF16)))
            v_new = (u_ref[p, r, :] - ws[:c]).astype(BF16)
            upd = _dot(l2_ref[p, n], pair_diag(v_new))
            o = ws[c:] + upd[:c]
            states[p] = states[p] * glw_ref[p, n, 0:1, :] + upd[c:]
            for hh in range(2):
                cols = slice((2 * p + hh) * dh, (2 * p + hh + 1) * dh)
                gated = _rms_rows(o[:, hh * dh:(hh + 1) * dh], gn) * _silu(z_ref[r, cols])
                o_ref[r, cols] = gated.astype(BF16)
    for p in range(pb):
        s_ref[p] = states[p]


def gdn_scan(wq, l2, u, glw, proj, z_col0, out_norm, *, pb, tp):
    nk, t, two_dh = u.shape
    c = GDN_CHUNK
    nc = tp // c
    width = pb * two_dh
    blk4 = lambda a: pl.BlockSpec((pb, nc) + a.shape[2:], lambda h, i: (h, i, 0, 0))
    return pl.pallas_call(
        _scan_kernel,
        grid=(nk // pb, t // tp),
        in_specs=[blk4(wq), blk4(l2), pl.BlockSpec((pb, tp, two_dh), lambda h, i: (h, i, 0)), blk4(glw),
                  pl.BlockSpec((tp, width), lambda h, i: (i, z_col0 // width + h)),
                  pl.BlockSpec((1, GDN_HEAD_DIM), lambda h, i: (0, 0))],
        out_specs=pl.BlockSpec((tp, width), lambda h, i: (i, h)),
        out_shape=jax.ShapeDtypeStruct((t, nk * two_dh), BF16),
        scratch_shapes=[pltpu.VMEM((pb, two_dh // 2, two_dh), F32)],
        compiler_params=_params("parallel", "arbitrary"),
        name="gdn_scan",
    )(wq, l2, u, glw, proj, out_norm.reshape(1, GDN_HEAD_DIM))


def _moba_kv_kernel(x_ref, g_ref, wk_ref, wvt_ref, kb_ref, vt_ref, mean_ref):
    xn = _rms_rows(x_ref[...], g_ref[...]).astype(BF16)
    k = _dot(xn, wk_ref[...])
    kb_ref[...] = k.astype(BF16)
    for b in range(mean_ref.shape[0]):
        mean_ref[b] = jnp.mean(k[b * MOBA_BLOCK:(b + 1) * MOBA_BLOCK], axis=0, keepdims=True)
    vt_ref[...] = _dot_nt(wvt_ref[...], xn).astype(BF16)


def moba_kv(x, g, wk, wvt, *, tm):
    t, d = x.shape
    kv = wk.shape[1]
    nb = t // MOBA_BLOCK
    bpt = tm // MOBA_BLOCK
    kb, vt, means = pl.pallas_call(
        _moba_kv_kernel,
        grid=(t // tm,),
        in_specs=[
            pl.BlockSpec((tm, d), lambda i: (i, 0)),
            pl.BlockSpec((1, d), lambda i: (0, 0)),
            pl.BlockSpec((d, kv), lambda i: (0, 0)),
            pl.BlockSpec((kv, d), lambda i: (0, 0)),
        ],
        out_specs=[pl.BlockSpec((tm, kv), lambda i: (i, 0)),
                   pl.BlockSpec((kv, tm), lambda i: (0, i)),
                   pl.BlockSpec((bpt, 1, kv), lambda i: (i, 0, 0))],
        out_shape=[jax.ShapeDtypeStruct((t, kv), BF16), jax.ShapeDtypeStruct((kv, t), BF16),
                   jax.ShapeDtypeStruct((nb, 1, kv), F32)],
        compiler_params=_params("parallel"),
        name="moba_kv",
    )(x, g.reshape(1, d), wk, wvt)
    return kb, vt, means.reshape(nb, kv)


def _rel_bucket_table(n_dist):
    n = np.arange(n_dist)
    max_exact = N_REL_BUCKETS // 2
    ratio = np.log(np.maximum(n, max_exact).astype(np.float32) / np.float32(max_exact)) \
        / np.float32(math.log(REL_MAX_DISTANCE / max_exact))
    large = np.minimum(max_exact + (ratio.astype(np.float32) * (N_REL_BUCKETS - max_exact)).astype(np.int32),
                       N_REL_BUCKETS - 1)
    return np.where(n < max_exact, n, large).astype(np.int32)


def _bucket_tiles():
    table = _rel_bucket_table(2 * MOBA_BLOCK)
    a = np.arange(MOBA_BLOCK)[:, None]
    b = np.arange(MOBA_BLOCK)[None, :]
    d_own = b - a
    own = np.where(d_own >= 0, table[np.maximum(d_own, 0)], -1)
    prev = table[MOBA_BLOCK + b - a]
    return np.stack([own, prev]).astype(np.int32)


def _moba_kernel(rb_ref, qt_ref, k_ref, vt_ref, km_ref, bucket_ref, o_ref,
                 bias_ref, sel_ref, m_ref, acc_ref, *, group):
    g = pl.program_id(0)
    i = pl.program_id(1)
    bs = MOBA_BLOCK
    dh = ATT_HEAD_DIM
    nb = km_ref.shape[0]
    nq = group * bs

    @pl.when(i == 0)
    def _():
        for hh in range(group):
            for t in range(2):
                bk = bucket_ref[t]
                bias = jnp.full(bk.shape, MASK_VALUE, F32)
                for b in range(N_REL_BUCKETS):
                    bias = jnp.where(bk == b, rb_ref[b, g * group + hh] * LOG2_E, bias)
                bias_ref[t, :, hh * bs:(hh + 1) * bs] = bias

    qt = jnp.concatenate([qt_ref[hh * dh:(hh + 1) * dh, :] for hh in range(group)], axis=1)

    blk = lax.broadcasted_iota(jnp.int32, (nb, nq), 0)
    gate = jnp.where(blk < i, _dot(km_ref[...].astype(BF16), qt), -jnp.inf)
    sel = jnp.zeros((nb, nq), F32)
    for r in range(MOBA_TOPK):
        mx = jnp.max(gate, axis=0, keepdims=True)
        first = jnp.min(jnp.where(gate == mx, blk, nb), axis=0, keepdims=True)
        pick = blk == first
        sel = jnp.where(jnp.logical_and(pick, r < i), 1.0, sel)
        gate = jnp.where(pick, -jnp.inf, gate)
    sel_ref[...] = sel

    def keys(j, nblk=1):
        return k_ref[pl.ds(pl.multiple_of(j * bs, bs), nblk * bs), :]

    def values_t(j, nblk=1):
        return vt_ref[:, pl.ds(pl.multiple_of(j * bs, bs), nblk * bs)]

    OWN, PREV, FAR = 0, 1, 2
    SLAB = 32
    LOOKAHEAD = 4
    L_ROWS = MOBA_L_ROWS

    def attend(state, units):
        m_all, acc_all = state
        m_cur = [m_all[:, hh * bs:(hh + 1) * bs] for hh in range(group)]
        acc_cur = [acc_all[:, hh * bs:(hh + 1) * bs] for hh in range(group)]
        tasks = [(u, hh) for u in range(len(units)) for hh in range(group)]

        def score(t):
            u, hh = tasks[t]
            j, kinds = units[u]
            return _dot(keys(j, len(kinds)), qt[:, hh * bs:(hh + 1) * bs])

        pending = {t: score(t) for t in range(min(LOOKAHEAD, len(tasks)))}
        for t, (u, hh) in enumerate(tasks):
            if t + LOOKAHEAD < len(tasks):
                pending[t + LOOKAHEAD] = score(t + LOOKAHEAD)
            scores = pending.pop(t)
            j, kinds = units[u]
            cols = slice(hh * bs, (hh + 1) * bs)
            far_bias = rb_ref[N_REL_BUCKETS - 1, g * group + hh] * LOG2_E
            parts = []
            run_max = None
            for b, kind in enumerate(kinds):
                if kind != OWN:
                    row = jnp.where(sel_ref[pl.ds(j + b, 1), cols] > 0.0, far_bias if kind == FAR else 0.0,
                                    MASK_VALUE)
                    row = jnp.broadcast_to(row, (SLAB, bs))
                for r0 in range(0, bs, SLAB):
                    part = scores[b * bs + r0:b * bs + r0 + SLAB]
                    if kind == OWN:
                        part = part + bias_ref[0, r0:r0 + SLAB, cols]
                    elif kind == PREV:
                        part = part + bias_ref[1, r0:r0 + SLAB, cols] + row
                    else:
                        part = part + row
                    parts.append(part)
                    run_max = part if run_max is None else jnp.maximum(run_max, part)
            m_new = jnp.maximum(m_cur[hh], jnp.max(run_max, axis=0, keepdims=True))
            alpha = jnp.exp2(m_cur[hh] - m_new)
            m_slab = jnp.broadcast_to(m_new, (SLAB, bs))
            packed = [jnp.exp2(part - m_slab).astype(BF16) for part in parts]
            vt_ones = jnp.concatenate([values_t(j, len(kinds)), jnp.ones((L_ROWS, len(kinds) * bs), BF16)], axis=0)
            m_cur[hh] = m_new
            acc_cur[hh] = alpha * acc_cur[hh] + _dot(vt_ones, jnp.concatenate(packed, axis=0))
        return jnp.concatenate(m_cur, axis=1), jnp.concatenate(acc_cur, axis=1)

    def load_state():
        return m_ref[...], acc_ref[...]

    def store_state(state):
        m_ref[...], acc_ref[...] = state

    empty = (jnp.full((1, nq), MASK_VALUE, F32), jnp.zeros((dh + L_ROWS, nq), F32))

    n_old = jnp.maximum(i - 1, 0)
    odd = n_old % 2 == 1

    @pl.when(i == 0)
    def _():
        store_state(attend(empty, [(i, (OWN,))]))

    @pl.when(jnp.logical_and(i >= 1, jnp.logical_not(odd)))
    def _():
        store_state(attend(empty, [(i - 1, (PREV, OWN))]))

    @pl.when(odd)
    def _():
        store_state(attend(empty, [(i - 2, (FAR, PREV, OWN))]))

    n_far = n_old - n_old % 2

    def far_units(j0, n_units):
        return [(j0 + 2 * u, (FAR, FAR)) for u in range(n_units)]

    def far_body(jj, carry):
        store_state(attend(load_state(), far_units(4 * jj, 2)))
        return carry

    lax.fori_loop(0, n_far // 4, far_body, 0)

    @pl.when(n_far % 4 == 2)
    def _():
        store_state(attend(load_state(), far_units(n_far - 2, 1)))

    out = acc_ref[0:dh, :] / acc_ref[dh:dh + 1, :]
    for hh in range(group):
        o_ref[:, hh * dh:(hh + 1) * dh] = out[:, hh * bs:(hh + 1) * bs].T.astype(BF16)


def moba_attention(qt, k, vt, k_means, rel_bias, *, n_heads, n_kv_heads):
    dh = ATT_HEAD_DIM
    t = k.shape[0]
    bs = MOBA_BLOCK
    nb = t // bs
    group = n_heads // n_kv_heads
    nq = group * bs
    assert REL_MAX_DISTANCE <= bs
    buckets = jnp.asarray(_bucket_tiles())
    return pl.pallas_call(
        functools.partial(_moba_kernel, group=group),
        grid=(n_kv_heads, nb),
        in_specs=[
            pl.BlockSpec(memory_space=pltpu.SMEM),
            pl.BlockSpec((group * dh, bs), lambda g, i: (g, i)),
            pl.BlockSpec((t, dh), lambda g, i: (0, g)),
            pl.BlockSpec((dh, t), lambda g, i: (g, 0)),
            pl.BlockSpec((nb, dh), lambda g, i: (0, g)),
            pl.BlockSpec((2, bs, bs), lambda g, i: (0, 0, 0)),
        ],
        out_specs=pl.BlockSpec((bs, group * dh), lambda g, i: (i, g)),
        out_shape=jax.ShapeDtypeStruct((t, n_heads * dh), BF16),
        scratch_shapes=[
            pltpu.VMEM((2, bs, nq), F32),
            pltpu.VMEM((nb, nq), F32),
            pltpu.VMEM((1, nq), F32),
            pltpu.VMEM((dh + MOBA_L_ROWS, nq), F32),
        ],
        compiler_params=_params("parallel", "arbitrary"),
        name="moba_attention",
    )(rel_bias, qt, k, vt, k_means, buckets)


def kernel(x, ffn1_norm, ffn1_w_in, ffn1_w_out, mix_norm, ffn2_norm, ffn2_w_in, ffn2_w_out,
           gdn_w_in, gdn_conv_w, gdn_a_log, gdn_dt_bias, gdn_out_norm, gdn_w_out,
           kv_norm, w_kv, moba_w_q, moba_w_o, rel_bias, final_norm):
    batch, seq, d_model = x.shape
    depth = ffn1_norm.shape[0]
    n_a = gdn_w_in.shape[0]
    n_v_heads = gdn_a_log.shape[1]
    value_dim = gdn_w_out.shape[1]
    conv_dim = gdn_conv_w.shape[2]
    key_dim = (conv_dim - value_dim) // 2
    rep = value_dim // key_dim
    n_heads = moba_w_q.shape[2] // ATT_HEAD_DIM
    kv_dim = w_kv.shape[1] // 2
    n_kv_heads = kv_dim // ATT_HEAD_DIM
    bf = lambda a: a.astype(BF16)
    tl = TILES
    gdn_w_in_t = jnp.swapaxes(gdn_w_in, 1, 2)
    gdn_w_out_bf = bf(gdn_w_out)
    moba_w_q_t = bf(jnp.swapaxes(moba_w_q, 1, 2))
    moba_w_o_bf = bf(moba_w_o)

    outs = []
    for bi in range(batch):
        h = x[bi]
        k_nat = vt = k_means = None
        for layer in range(depth):
            if layer == n_a:
                k_nat, vt, k_means = moba_kv(h, kv_norm, bf(w_kv[:, :kv_dim]), bf(w_kv[:, kv_dim:].T),
                                             tm=tl.kv_rows)
            h = ffn(h, ffn1_norm[layer], ffn1_w_in, ffn1_w_out, layer, tm=tl.ffn_rows, tf=tl.ffn_cols)
            if layer < n_a:
                main = conv_dim + value_dim
                n_ba = 2 * n_v_heads
                proj, ba = gdn_in_proj(h, mix_norm[layer], gdn_w_in_t, layer, n_main=main, n_tail=n_ba,
                                       tm=tl.gdn_in_rows, tn=tl.gdn_in_cols)
                beta, g = gdn_gates(ba, gdn_a_log[layer], gdn_dt_bias[layer], tt=tl.gates_rows)
                qkv = gdn_conv(proj, gdn_conv_w[layer], conv_dim=conv_dim, key_dim=key_dim,
                               tt=tl.conv_rows, tc=tl.conv_cols, strip=tl.conv_strip)
                g_rows = g.T.reshape(n_v_heads, 1, seq)
                wq, l2, u, glw = gdn_prep(qkv, g, beta, g_rows, n_heads=n_v_heads, rep=rep, tp=tl.prep_rows)
                gated = gdn_scan(wq, l2, u, glw, proj, conv_dim, gdn_out_norm[layer],
                                 pb=tl.scan_pairs, tp=tl.scan_rows)
                h = matmul_res(gated, gdn_w_out_bf, layer, h, tm=tl.gdn_out_rows, tn=tl.gdn_out_cols,
                               name="gdn_out")
            else:
                j = layer - n_a
                qt = norm_matmul_t(h, mix_norm[layer], moba_w_q_t, j, tm=tl.qt_rows, tn=tl.qt_cols,
                                   name="moba_qt", out_dtype=BF16, out_scale=MOBA_Q_SCALE)
                att = moba_attention(qt, k_nat, vt, k_means, rel_bias,
                                     n_heads=n_heads, n_kv_heads=n_kv_heads)
                h = matmul_res(att, moba_w_o_bf, j, h, tm=tl.moba_out_rows, tn=tl.moba_out_cols,
                               name="moba_out")
            last = layer == depth - 1
            h = ffn(h, ffn2_norm[layer], ffn2_w_in, ffn2_w_out, layer, final_norm if last else None,
                    tm=tl.ffn_rows, tf=tl.ffn_cols)
        outs.append(h)
    return jnp.stack(outs)
```

```python
import functools
import math
from typing import NamedTuple

import numpy as np
import jax
import jax.numpy as jnp
from jax import lax
from jax.experimental import pallas as pl
from jax.experimental.pallas import tpu as pltpu

F32 = jnp.float32
BF16 = jnp.bfloat16

RMS_EPS = 1e-6
GDN_HEAD_DIM = 128
GDN_CHUNK = 64
GDN_CONV_WIDTH = 4
ATT_HEAD_DIM = 128
MOBA_BLOCK = 256
MOBA_TOPK = 3
N_REL_BUCKETS = 32
REL_MAX_DISTANCE = 128
MASK_VALUE = -1e30
LOG2_E = 1.4426950408889634
MOBA_Q_SCALE = ATT_HEAD_DIM ** -0.5 * LOG2_E

V7X_VMEM_BYTES = 64 * 1024 * 1024
V7X_SUBLANES = 8
V7X_BF16_SUBLANE_TILE = 16
MOBA_L_ROWS = V7X_BF16_SUBLANE_TILE

VMEM_LIMIT_BYTES = V7X_VMEM_BYTES - 6 * 1024 * 1024


class Tiles(NamedTuple):
    ffn_rows: int = 1024
    ffn_cols: int = 512
    gdn_in_rows: int = 2048
    gdn_in_cols: int = 512
    gates_rows: int = 1024
    conv_rows: int = 2048
    conv_cols: int = 1024
    conv_strip: int = 16
    prep_rows: int = 2048
    scan_pairs: int = 16
    scan_rows: int = 256
    gdn_out_rows: int = 1024
    gdn_out_cols: int = 1024
    kv_rows: int = 1024
    qt_rows: int = 2048
    qt_cols: int = 1024
    moba_out_rows: int = 2048
    moba_out_cols: int = 512


TILES = Tiles()


def _params(*sem):
    return pltpu.CompilerParams(dimension_semantics=sem, vmem_limit_bytes=VMEM_LIMIT_BYTES)


def _rms_rows(x, g):
    ms = jnp.mean(x * x, axis=-1, keepdims=True)
    return x * lax.rsqrt(ms + RMS_EPS) * g


def _silu(x):
    return x * jax.nn.sigmoid(x)


def _dot(a, b):
    return jnp.dot(a, b, preferred_element_type=F32)


def _dot_nt(a, b):
    return lax.dot_general(a, b, (((1,), (1,)), ((), ())), preferred_element_type=F32)


def _row_block(tm, d):
    return pl.BlockSpec((tm, d), lambda i, j: (i, 0), pipeline_mode=pl.Buffered(1))


def _gdn_in_kernel(x_ref, g_ref, w_ref, wt_ref, o_ref, ot_ref, xn_ref):
    @pl.when(pl.program_id(1) == 0)
    def _():
        xn_ref[...] = _rms_rows(x_ref[...], g_ref[...]).astype(BF16)
        ot_ref[...] = _dot_nt(xn_ref[...], wt_ref[...].astype(BF16))

    o_ref[...] = _dot_nt(xn_ref[...], w_ref[...].astype(BF16))


def gdn_in_proj(x, g, w_t, layer, *, n_main, n_tail, tm, tn):
    t, d = x.shape
    assert n_main % tn == 0 and n_main % n_tail == 0
    return pl.pallas_call(
        _gdn_in_kernel,
        grid=(t // tm, n_main // tn),
        in_specs=[
            _row_block(tm, d),
            pl.BlockSpec((1, d), lambda i, j: (0, 0)),
            pl.BlockSpec((None, tn, d), lambda i, j: (layer, j, 0)),
            pl.BlockSpec((None, n_tail, d), lambda i, j: (layer, n_main // n_tail, 0)),
        ],
        out_specs=[pl.BlockSpec((tm, tn), lambda i, j: (i, j)),
                   pl.BlockSpec((tm, n_tail), lambda i, j: (i, 0))],
        out_shape=[jax.ShapeDtypeStruct((t, n_main), F32), jax.ShapeDtypeStruct((t, n_tail), F32)],
        scratch_shapes=[pltpu.VMEM((tm, d), BF16)],
        compiler_params=_params("parallel", "arbitrary"),
        name="gdn_in",
    )(x, g.reshape(1, d), w_t, w_t)


def _norm_matmul_t_kernel(x_ref, g_ref, wt_ref, o_ref, xn_ref, *, out_scale):
    @pl.when(pl.program_id(1) == 0)
    def _():
        xn_ref[...] = _rms_rows(x_ref[...], g_ref[...]).astype(BF16)

    y = _dot_nt(wt_ref[...], xn_ref[...])
    if out_scale != 1.0:
        y = y * out_scale
    o_ref[...] = y.astype(o_ref.dtype)


def norm_matmul_t(x, g, wt, layer, *, tm, tn, name, out_dtype=F32, out_scale=1.0):
    t, d = x.shape
    n = wt.shape[1]
    return pl.pallas_call(
        functools.partial(_norm_matmul_t_kernel, out_scale=out_scale),
        grid=(t // tm, n // tn),
        in_specs=[
            _row_block(tm, d),
            pl.BlockSpec((1, d), lambda i, j: (0, 0)),
            pl.BlockSpec((None, tn, d), lambda i, j: (layer, j, 0)),
        ],
        out_specs=pl.BlockSpec((tn, tm), lambda i, j: (j, i)),
        out_shape=jax.ShapeDtypeStruct((n, t), out_dtype),
        scratch_shapes=[pltpu.VMEM((tm, d), BF16)],
        compiler_params=_params("parallel", "arbitrary"),
        name=name,
    )(x, g.reshape(1, d), wt)


def _ffn_kernel(h_ref, g_ref, wg_ref, wu_ref, wo_ref, fg_ref, o_ref, xn_ref, *, final_norm):
    j = pl.program_id(1)
    tm = h_ref.shape[0]
    strip = min(tm, 256)

    @pl.when(j == 0)
    def _():
        for r0 in range(0, tm, strip):
            r = slice(r0, r0 + strip)
            xn_ref[r, :] = _rms_rows(h_ref[r, :], g_ref[...]).astype(BF16)
            o_ref[r, :] = jnp.zeros((strip, o_ref.shape[1]), F32)

    xn = xn_ref[...]
    tf = wg_ref.shape[1]
    half = min(tf, 256)
    acts = []
    for c0 in range(0, tf, half):
        gate = _dot(xn, wg_ref[:, c0:c0 + half].astype(BF16))
        up = _dot(xn, wu_ref[:, c0:c0 + half].astype(BF16))
        acts.append((_silu(gate) * up).astype(BF16))
    act = acts[0] if len(acts) == 1 else jnp.concatenate(acts, axis=1)
    o_ref[...] += _dot(act, wo_ref[...].astype(BF16))

    @pl.when(j == pl.num_programs(1) - 1)
    def _():
        for r0 in range(0, tm, strip):
            r = slice(r0, r0 + strip)
            y = h_ref[r, :] + 0.5 * o_ref[r, :]
            if final_norm:
                y = _rms_rows(y, fg_ref[...])
            o_ref[r, :] = y


def ffn(h, g, w_in, w_out, layer, final_g=None, *, tm, tf):
    t, d = h.shape
    f = w_out.shape[1]
    nf = f // tf
    final_norm = final_g is not None
    fg = (final_g if final_norm else g).reshape(1, d)
    return pl.pallas_call(
        functools.partial(_ffn_kernel, final_norm=final_norm),
        grid=(t // tm, nf),
        in_specs=[
            _row_block(tm, d),
            pl.BlockSpec((1, d), lambda i, j: (0, 0)),
            pl.BlockSpec((None, d, tf), lambda i, j: (layer, 0, j)),
            pl.BlockSpec((None, d, tf), lambda i, j: (layer, 0, j + nf)),
            pl.BlockSpec((None, tf, d), lambda i, j: (layer, j, 0)),
            pl.BlockSpec((1, d), lambda i, j: (0, 0)),
        ],
        out_specs=_row_block(tm, d),
        out_shape=jax.ShapeDtypeStruct((t, d), F32),
        scratch_shapes=[pltpu.VMEM((tm, d), BF16)],
        compiler_params=_params("parallel", "arbitrary"),
        name="ffn",
    )(h, g.reshape(1, d), w_in, w_in, w_out, fg)


def _matmul_res_kernel(a_ref, w_ref, r_ref, o_ref):
    o_ref[...] = r_ref[...] + _dot(a_ref[...], w_ref[...])


def matmul_res(a, w, layer, r, *, tm, tn, name):
    t, k = a.shape
    n = w.shape[2]
    return pl.pallas_call(
        _matmul_res_kernel,
        grid=(t // tm, n // tn),
        in_specs=[
            pl.BlockSpec((tm, k), lambda i, j: (i, 0)),
            pl.BlockSpec((None, k, tn), lambda i, j: (layer, 0, j)),
            pl.BlockSpec((tm, tn), lambda i, j: (i, j)),
        ],
        out_specs=pl.BlockSpec((tm, tn), lambda i, j: (i, j)),
        out_shape=jax.ShapeDtypeStruct((t, n), F32),
        compiler_params=_params("parallel", "arbitrary"),
        name=name,
    )(a, w, r)


def _conv_kernel(prev_ref, cur_ref, w_ref, o_ref, xs_ref, *, n_norm_blocks, strip):
    i = pl.program_id(0)
    j = pl.program_id(1)
    tt, tc = cur_ref.shape
    halo = prev_ref.shape[0]
    xs_ref[0:halo, :] = jnp.where(i == 0, 0.0, prev_ref[...])
    xs_ref[halo:, :] = cur_ref[...]
    w = w_ref[...]

    def conv_strip(r0):
        y = w[GDN_CONV_WIDTH - 1:GDN_CONV_WIDTH] * xs_ref[halo + r0:halo + r0 + strip, :]
        for s in range(1, GDN_CONV_WIDTH):
            tap = GDN_CONV_WIDTH - 1 - s
            y = y + w[tap:tap + 1] * xs_ref[halo + r0 - s:halo + r0 - s + strip, :]
        return _silu(y)

    @pl.when(j < n_norm_blocks)
    def _():
        for r0 in range(0, tt, strip):
            y = conv_strip(r0)
            for hh in range(tc // GDN_HEAD_DIM):
                seg = y[:, hh * GDN_HEAD_DIM:(hh + 1) * GDN_HEAD_DIM]
                ss = jnp.sum(seg * seg, axis=-1, keepdims=True)
                o_ref[r0:r0 + strip, hh * GDN_HEAD_DIM:(hh + 1) * GDN_HEAD_DIM] = seg * lax.rsqrt(ss + RMS_EPS)

    @pl.when(j >= n_norm_blocks)
    def _():
        for r0 in range(0, tt, strip):
            o_ref[r0:r0 + strip, :] = conv_strip(r0)


def gdn_conv(proj, conv_w, *, conv_dim, key_dim, tt, tc, strip):
    t = proj.shape[0]
    halo = V7X_SUBLANES
    assert halo >= GDN_CONV_WIDTH - 1
    return pl.pallas_call(
        functools.partial(_conv_kernel, n_norm_blocks=2 * key_dim // tc, strip=strip),
        grid=(t // tt, conv_dim // tc),
        in_specs=[
            pl.BlockSpec((halo, tc), lambda i, j: (jnp.maximum(i * (tt // halo) - 1, 0), j)),
            pl.BlockSpec((tt, tc), lambda i, j: (i, j)),
            pl.BlockSpec((GDN_CONV_WIDTH, tc), lambda i, j: (0, j)),
        ],
        out_specs=pl.BlockSpec((tt, tc), lambda i, j: (i, j)),
        out_shape=jax.ShapeDtypeStruct((t, conv_dim), F32),
        scratch_shapes=[pltpu.VMEM((tt + halo, tc), F32)],
        compiler_params=_params("parallel", "parallel"),
        name="gdn_conv",
    )(proj, proj, conv_w)


def _gates_kernel(ba_ref, alog_ref, dtb_ref, beta_ref, g_ref):
    nh = beta_ref.shape[1]
    ba = ba_ref[...]
    beta_ref[...] = jax.nn.sigmoid(ba[:, :nh])
    z = ba[:, nh:] + dtb_ref[...]
    softplus = jnp.maximum(z, 0.0) + jnp.log1p(jnp.exp(-jnp.abs(z)))
    g_ref[...] = -jnp.exp(alog_ref[...]) * softplus


def gdn_gates(ba, a_log, dt_bias, *, tt):
    t, two_h = ba.shape
    nh = two_h // 2
    return pl.pallas_call(
        _gates_kernel,
        grid=(t // tt,),
        in_specs=[
            pl.BlockSpec((tt, two_h), lambda i: (i, 0)),
            pl.BlockSpec((1, nh), lambda i: (0, 0)),
            pl.BlockSpec((1, nh), lambda i: (0, 0)),
        ],
        out_specs=[pl.BlockSpec((tt, nh), lambda i: (i, 0)), pl.BlockSpec((tt, nh), lambda i: (i, 0))],
        out_shape=[jax.ShapeDtypeStruct((t, nh), F32), jax.ShapeDtypeStruct((t, nh), F32)],
        compiler_params=_params("parallel"),
        name="gdn_gates",
    )(ba, a_log.reshape(1, nh), dt_bias.reshape(1, nh))


def _prep_kernel(q_ref, k_ref, v_ref, gsel_ref, bsel_ref, grow_ref, wq_ref, l2_ref, u_ref, glw_ref):
    kh = pl.program_id(0)
    tp, nh = gsel_ref.shape
    c = GDN_CHUNK
    dh = GDN_HEAD_DIM
    scale = dh ** -0.5
    lane_h = lax.broadcasted_iota(jnp.int32, (tp, nh), 1)

    def column(ref, head):
        return jnp.sum(jnp.where(lane_h == head, ref[...], 0.0), axis=1, keepdims=True)

    gcol = [column(gsel_ref, 2 * kh + hh) for hh in range(2)]
    bcol = [column(bsel_ref, 2 * kh + hh) for hh in range(2)]
    grow = [grow_ref[hh] for hh in range(2)]

    ii = lax.broadcasted_iota(jnp.int32, (c, 2 * c), 0)
    ll = lax.broadcasted_iota(jnp.int32, (c, 2 * c), 1)
    jj = jnp.bitwise_and(ll, c - 1)
    head_a = ll < c
    incl = ii >= jj
    strict = ii > jj
    wi = lax.broadcasted_iota(jnp.int32, (c, 4 * c), 0)
    wl = lax.broadcasted_iota(jnp.int32, (c, 4 * c), 1)
    eye_w = jnp.where(wi == jnp.bitwise_and(wl, c - 1), 1.0, 0.0).astype(F32)
    br = lax.broadcasted_iota(jnp.int32, (4 * c, 4 * c), 0)
    bc = lax.broadcasted_iota(jnp.int32, (4 * c, 4 * c), 1)
    same_block = (br // c) == (bc // c)
    strict_block = jnp.logical_and(same_block, br != bc)

    def block_diag(packed):
        return jnp.concatenate([packed] * 4, axis=0)

    def packed_matmul(a, b):
        bd = jnp.where(same_block, block_diag(b), 0.0).astype(BF16)
        return _dot(a.astype(BF16), bd)

    n_half = tp // (2 * c)
    lmats = [[] for _ in range(n_half)]
    rhs_blocks = [[] for _ in range(n_half)]
    for half in range(n_half):
        for y in range(2):
            n = 2 * half + y
            r = slice(n * c, (n + 1) * c)
            q = q_ref[r, :] * scale
            k = k_ref[r, :]
            g_pair = jnp.where(head_a, gcol[0][r], gcol[1][r])
            b_pair = jnp.where(head_a, bcol[0][r], bcol[1][r])
            grow_pair = jnp.concatenate([grow[0][:, r], grow[1][:, r]], axis=1)
            tri = jnp.where(incl, grow_pair, 0.0)
            gc = [jnp.sum(jnp.where(head_a, tri, 0.0), axis=1, keepdims=True),
                  jnp.sum(jnp.where(head_a, 0.0, tri), axis=1, keepdims=True)]
            gc_col = jnp.where(head_a, gc[0], gc[1])
            gc_row = jnp.sum(jnp.where(ii <= jj, g_pair, 0.0), axis=0, keepdims=True)
            decay = jnp.where(incl, jnp.exp(jnp.where(incl, gc_col - gc_row, 0.0)), 0.0)
            kb16 = k.astype(BF16)
            gram = _dot_nt(jnp.concatenate([q.astype(BF16), kb16], axis=0),
                           jnp.concatenate([kb16, kb16], axis=0))
            l2_ref[n, 0:c, :] = jnp.where(incl, gram[:c] * decay, 0.0).astype(BF16)
            lmats[half].append(jnp.where(strict, gram[c:] * b_pair * decay, 0.0))
            kups = []
            for hh in range(2):
                g_last = jnp.sum(gcol[hh][r], axis=0, keepdims=True)
                eg = jnp.exp(gc[hh])
                beta = bcol[hh][r]
                kups.append(k * jnp.exp(g_last - gc[hh]))
                wq_ref[n, c:2 * c, hh * dh:(hh + 1) * dh] = (q * eg).astype(BF16)
                glw_ref[n, :, hh * dh:(hh + 1) * dh] = jnp.broadcast_to(jnp.exp(g_last), (V7X_SUBLANES, dh))
                rhs_blocks[half].append(
                    jnp.concatenate([v_ref[r, hh * dh:(hh + 1) * dh] * beta, k * (beta * eg)], axis=1))
            l2_ref[n, c:c + dh, :] = jnp.concatenate(kups, axis=0).T.astype(BF16)

    lws = [jnp.concatenate(lm, axis=1) for lm in lmats]
    wj = jnp.bitwise_and(wl, c - 1)
    tinvs = [eye_w - jnp.where(wi // 2 == wj // 2, lw, 0.0) for lw in lws]
    s = 2
    while s < c:
        join = jnp.logical_and(wi // (2 * s) == wj // (2 * s), wi // s > wj // s)
        xe = [packed_matmul(x, jnp.where(join, lw, 0.0)) for x, lw in zip(tinvs, lws)]
        tinvs = [x - packed_matmul(y, x) for x, y in zip(tinvs, xe)]
        s *= 2
    for half in range(n_half):
        rhs_all = jnp.concatenate(rhs_blocks[half], axis=0)
        t_off = jnp.where(strict_block, block_diag(tinvs[half]), 0.0).astype(BF16)
        uw = rhs_all + _dot(t_off, rhs_all.astype(BF16))
        for y in range(2):
            n = 2 * half + y
            for hh in range(2):
                blk = uw[(2 * y + hh) * c:(2 * y + hh + 1) * c]
                u_ref[n * c:(n + 1) * c, hh * dh:(hh + 1) * dh] = blk[:, :dh]
                wq_ref[n, 0:c, hh * dh:(hh + 1) * dh] = blk[:, dh:].astype(BF16)


def gdn_prep(qkv, g, beta, g_rows, *, n_heads, rep, tp):
    assert rep == 2
    t = qkv.shape[0]
    dh = GDN_HEAD_DIM
    c = GDN_CHUNK
    nk = n_heads // rep
    nc = tp // c
    return pl.pallas_call(
        _prep_kernel,
        grid=(nk, t // tp),
        in_specs=[
            pl.BlockSpec((tp, dh), lambda h, i: (i, h)),
            pl.BlockSpec((tp, dh), lambda h, i: (i, nk + h)),
            pl.BlockSpec((tp, 2 * dh), lambda h, i: (i, nk + h)),
            pl.BlockSpec((tp, n_heads), lambda h, i: (i, 0)),
            pl.BlockSpec((tp, n_heads), lambda h, i: (i, 0)),
            pl.BlockSpec((2, 1, tp), lambda h, i: (h, 0, i)),
        ],
        out_specs=[
            pl.BlockSpec((None, nc, 2 * c, 2 * dh), lambda h, i: (h, i, 0, 0)),
            pl.BlockSpec((None, nc, c + dh, 2 * c), lambda h, i: (h, i, 0, 0)),
            pl.BlockSpec((None, tp, 2 * dh), lambda h, i: (h, i, 0)),
            pl.BlockSpec((None, nc, V7X_SUBLANES, 2 * dh), lambda h, i: (h, i, 0, 0)),
        ],
        out_shape=[
            jax.ShapeDtypeStruct((nk, t // c, 2 * c, 2 * dh), BF16),
            jax.ShapeDtypeStruct((nk, t // c, c + dh, 2 * c), BF16),
            jax.ShapeDtypeStruct((nk, t, 2 * dh), F32),
            jax.ShapeDtypeStruct((nk, t // c, V7X_SUBLANES, 2 * dh), F32),
        ],
        compiler_params=_params("parallel", "parallel"),
        name="gdn_prep",
    )(qkv, qkv, qkv, g, beta, g_rows)


def _scan_kernel(wq_ref, l2_ref, u_ref, glw_ref, z_ref, gn_ref, o_ref, s_ref):
    pb, nc = wq_ref.shape[0], wq_ref.shape[1]
    gn = gn_ref[...]
    c = GDN_CHUNK
    dh = GDN_HEAD_DIM

    @pl.when(pl.program_id(1) == 0)
    def _():
        s_ref[...] = jnp.zeros_like(s_ref)

    def pair_diag(x):
        z = jnp.zeros((x.shape[0], dh), x.dtype)
        return jnp.concatenate([jnp.concatenate([x[:, :dh], z], axis=1),
                                jnp.concatenate([z, x[:, dh:]], axis=1)], axis=0)

    states = [s_ref[p] for p in range(pb)]
    for n in range(nc):
        r = slice(n * c, (n + 1) * c)
        for p in range(pb):
            ws = _dot(wq_ref[p, n], pair_diag(states[p].astype(BF16)))
            v_new = (u_ref[p, r, :] - ws[:c]).astype(BF16)
            upd = _dot(l2_ref[p, n], pair_diag(v_new))
            o = ws[c:] + upd[:c]
            states[p] = states[p] * glw_ref[p, n, 0:1, :] + upd[c:]
            for hh in range(2):
                cols = slice((2 * p + hh) * dh, (2 * p + hh + 1) * dh)
                gated = _rms_rows(o[:, hh * dh:(hh + 1) * dh], gn) * _silu(z_ref[r, cols])
                o_ref[r, cols] = gated.astype(BF16)
    for p in range(pb):
        s_ref[p] = states[p]


def gdn_scan(wq, l2, u, glw, proj, z_col0, out_norm, *, pb, tp):
    nk, t, two_dh = u.shape
    c = GDN_CHUNK
    nc = tp // c
    width = pb * two_dh
    blk4 = lambda a: pl.BlockSpec((pb, nc) + a.shape[2:], lambda h, i: (h, i, 0, 0))
    return pl.pallas_call(
        _scan_kernel,
        grid=(nk // pb, t // tp),
        in_specs=[blk4(wq), blk4(l2), pl.BlockSpec((pb, tp, two_dh), lambda h, i: (h, i, 0)), blk4(glw),
                  pl.BlockSpec((tp, width), lambda h, i: (i, z_col0 // width + h)),
                  pl.BlockSpec((1, GDN_HEAD_DIM), lambda h, i: (0, 0))],
        out_specs=pl.BlockSpec((tp, width), lambda h, i: (i, h)),
        out_shape=jax.ShapeDtypeStruct((t, nk * two_dh), BF16),
        scratch_shapes=[pltpu.VMEM((pb, two_dh // 2, two_dh), F32)],
        compiler_params=_params("parallel", "arbitrary"),
        name="gdn_scan",
    )(wq, l2, u, glw, proj, out_norm.reshape(1, GDN_HEAD_DIM))


def _moba_kv_kernel(x_ref, g_ref, wk_ref, wvt_ref, kb_ref, vt_ref, mean_ref):
    xn = _rms_rows(x_ref[...], g_ref[...]).astype(BF16)
    k = _dot(xn, wk_ref[...])
    kb_ref[...] = k.astype(BF16)
    for b in range(mean_ref.shape[0]):
        mean_ref[b] = jnp.mean(k[b * MOBA_BLOCK:(b + 1) * MOBA_BLOCK], axis=0, keepdims=True)
    vt_ref[...] = _dot_nt(wvt_ref[...], xn).astype(BF16)


def moba_kv(x, g, wk, wvt, *, tm):
    t, d = x.shape
    kv = wk.shape[1]
    nb = t // MOBA_BLOCK
    bpt = tm // MOBA_BLOCK
    kb, vt, means = pl.pallas_call(
        _moba_kv_kernel,
        grid=(t // tm,),
        in_specs=[
            pl.BlockSpec((tm, d), lambda i: (i, 0)),
            pl.BlockSpec((1, d), lambda i: (0, 0)),
            pl.BlockSpec((d, kv), lambda i: (0, 0)),
            pl.BlockSpec((kv, d), lambda i: (0, 0)),
        ],
        out_specs=[pl.BlockSpec((tm, kv), lambda i: (i, 0)),
                   pl.BlockSpec((kv, tm), lambda i: (0, i)),
                   pl.BlockSpec((bpt, 1, kv), lambda i: (i, 0, 0))],
        out_shape=[jax.ShapeDtypeStruct((t, kv), BF16), jax.ShapeDtypeStruct((kv, t), BF16),
                   jax.ShapeDtypeStruct((nb, 1, kv), F32)],
        compiler_params=_params("parallel"),
        name="moba_kv",
    )(x, g.reshape(1, d), wk, wvt)
    return kb, vt, means.reshape(nb, kv)


def _rel_bucket_table(n_dist):
    n = np.arange(n_dist)
    max_exact = N_REL_BUCKETS // 2
    ratio = np.log(np.maximum(n, max_exact).astype(np.float32) / np.float32(max_exact)) \
        / np.float32(math.log(REL_MAX_DISTANCE / max_exact))
    large = np.minimum(max_exact + (ratio.astype(np.float32) * (N_REL_BUCKETS - max_exact)).astype(np.int32),
                       N_REL_BUCKETS - 1)
    return np.where(n < max_exact, n, large).astype(np.int32)


def _bucket_tiles():
    table = _rel_bucket_table(2 * MOBA_BLOCK)
    a = np.arange(MOBA_BLOCK)[:, None]
    b = np.arange(MOBA_BLOCK)[None, :]
    d_own = b - a
    own = np.where(d_own >= 0, table[np.maximum(d_own, 0)], -1)
    prev = table[MOBA_BLOCK + b - a]
    return np.stack([own, prev]).astype(np.int32)


def _moba_kernel(rb_ref, qt_ref, k_ref, vt_ref, km_ref, bucket_ref, o_ref,
                 bias_ref, sel_ref, m_ref, acc_ref, *, group):
    g = pl.program_id(0)
    i = pl.program_id(1)
    bs = MOBA_BLOCK
    dh = ATT_HEAD_DIM
    nb = km_ref.shape[0]
    nq = group * bs

    @pl.when(i == 0)
    def _():
        for hh in range(group):
            for t in range(2):
                bk = bucket_ref[t]
                bias = jnp.full(bk.shape, MASK_VALUE, F32)
                for b in range(N_REL_BUCKETS):
                    bias = jnp.where(bk == b, rb_ref[b, g * group + hh] * LOG2_E, bias)
                bias_ref[t, :, hh * bs:(hh + 1) * bs] = bias

    qt = jnp.concatenate([qt_ref[hh * dh:(hh + 1) * dh, :] for hh in range(group)], axis=1)

    blk = lax.broadcasted_iota(jnp.int32, (nb, nq), 0)
    gate = jnp.where(blk < i, _dot(km_ref[...].astype(BF16), qt), -jnp.inf)
    sel = jnp.zeros((nb, nq), F32)
    for r in range(MOBA_TOPK):
        mx = jnp.max(gate, axis=0, keepdims=True)
        first = jnp.min(jnp.where(gate == mx, blk, nb), axis=0, keepdims=True)
        pick = blk == first
        sel = jnp.where(jnp.logical_and(pick, r < i), 1.0, sel)
        gate = jnp.where(pick, -jnp.inf, gate)
    sel_ref[...] = sel

    def keys(j, nblk=1):
        return k_ref[pl.ds(pl.multiple_of(j * bs, bs), nblk * bs), :]

    def values_t(j, nblk=1):
        return vt_ref[:, pl.ds(pl.multiple_of(j * bs, bs), nblk * bs)]

    OWN, PREV, FAR = 0, 1, 2
    SLAB = 32
    LOOKAHEAD = 4
    L_ROWS = MOBA_L_ROWS

    def attend(state, units):
        m_all, acc_all = state
        m_cur = [m_all[:, hh * bs:(hh + 1) * bs] for hh in range(group)]
        acc_cur = [acc_all[:, hh * bs:(hh + 1) * bs] for hh in range(group)]
        tasks = [(u, hh) for u in range(len(units)) for hh in range(group)]

        def score(t):
            u, hh = tasks[t]
            j, kinds = units[u]
            return _dot(keys(j, len(kinds)), qt[:, hh * bs:(hh + 1) * bs])

        pending = {t: score(t) for t in range(min(LOOKAHEAD, len(tasks)))}
        for t, (u, hh) in enumerate(tasks):
            if t + LOOKAHEAD < len(tasks):
                pending[t + LOOKAHEAD] = score(t + LOOKAHEAD)
            scores = pending.pop(t)
            j, kinds = units[u]
            cols = slice(hh * bs, (hh + 1) * bs)
            far_bias = rb_ref[N_REL_BUCKETS - 1, g * group + hh] * LOG2_E
            parts = []
            run_max = None
            for b, kind in enumerate(kinds):
                if kind != OWN:
                    row = jnp.where(sel_ref[pl.ds(j + b, 1), cols] > 0.0, far_bias if kind == FAR else 0.0,
                                    MASK_VALUE)
                    row = jnp.broadcast_to(row, (SLAB, bs))
                for r0 in range(0, bs, SLAB):
                    part = scores[b * bs + r0:b * bs + r0 + SLAB]
                    if kind == OWN:
                        part = part + bias_ref[0, r0:r0 + SLAB, cols]
                    elif kind == PREV:
                        part = part + bias_ref[1, r0:r0 + SLAB, cols] + row
                    else:
                        part = part + row
                    parts.append(part)
                    run_max = part if run_max is None else jnp.maximum(run_max, part)
            m_new = jnp.maximum(m_cur[hh], jnp.max(run_max, axis=0, keepdims=True))
            alpha = jnp.exp2(m_cur[hh] - m_new)
            m_slab = jnp.broadcast_to(m_new, (SLAB, bs))
            packed = [jnp.exp2(part - m_slab).astype(BF16) for part in parts]
            vt_ones = jnp.concatenate([values_t(j, len(kinds)), jnp.ones((L_ROWS, len(kinds) * bs), BF16)], axis=0)
            m_cur[hh] = m_new
            acc_cur[hh] = alpha * acc_cur[hh] + _dot(vt_ones, jnp.concatenate(packed, axis=0))
        return jnp.concatenate(m_cur, axis=1), jnp.concatenate(acc_cur, axis=1)

    def load_state():
        return m_ref[...], acc_ref[...]

    def store_state(state):
        m_ref[...], acc_ref[...] = state

    empty = (jnp.full((1, nq), MASK_VALUE, F32), jnp.zeros((dh + L_ROWS, nq), F32))

    n_old = jnp.maximum(i - 1, 0)
    odd = n_old % 2 == 1

    @pl.when(i == 0)
    def _():
        store_state(attend(empty, [(i, (OWN,))]))

    @pl.when(jnp.logical_and(i >= 1, jnp.logical_not(odd)))
    def _():
        store_state(attend(empty, [(i - 1, (PREV, OWN))]))

    @pl.when(odd)
    def _():
        store_state(attend(empty, [(i - 2, (FAR, PREV, OWN))]))

    n_far = n_old - n_old % 2

    def far_units(j0, n_units):
        return [(j0 + 2 * u, (FAR, FAR)) for u in range(n_units)]

    def far_body(jj, carry):
        store_state(attend(load_state(), far_units(4 * jj, 2)))
        return carry

    lax.fori_loop(0, n_far // 4, far_body, 0)

    @pl.when(n_far % 4 == 2)
    def _():
        store_state(attend(load_state(), far_units(n_far - 2, 1)))

    out = acc_ref[0:dh, :] / acc_ref[dh:dh + 1, :]
    for hh in range(group):
        o_ref[:, hh * dh:(hh + 1) * dh] = out[:, hh * bs:(hh + 1) * bs].T.astype(BF16)


def moba_attention(qt, k, vt, k_means, rel_bias, *, n_heads, n_kv_heads):
    dh = ATT_HEAD_DIM
    t = k.shape[0]
    bs = MOBA_BLOCK
    nb = t // bs
    group = n_heads // n_kv_heads
    nq = group * bs
    assert REL_MAX_DISTANCE <= bs
    buckets = jnp.asarray(_bucket_tiles())
    return pl.pallas_call(
        functools.partial(_moba_kernel, group=group),
        grid=(n_kv_heads, nb),
        in_specs=[
            pl.BlockSpec(memory_space=pltpu.SMEM),
            pl.BlockSpec((group * dh, bs), lambda g, i: (g, i)),
            pl.BlockSpec((t, dh), lambda g, i: (0, g)),
            pl.BlockSpec((dh, t), lambda g, i: (g, 0)),
            pl.BlockSpec((nb, dh), lambda g, i: (0, g)),
            pl.BlockSpec((2, bs, bs), lambda g, i: (0, 0, 0)),
        ],
        out_specs=pl.BlockSpec((bs, group * dh), lambda g, i: (i, g)),
        out_shape=jax.ShapeDtypeStruct((t, n_heads * dh), BF16),
        scratch_shapes=[
            pltpu.VMEM((2, bs, nq), F32),
            pltpu.VMEM((nb, nq), F32),
            pltpu.VMEM((1, nq), F32),
            pltpu.VMEM((dh + MOBA_L_ROWS, nq), F32),
        ],
        compiler_params=_params("parallel", "arbitrary"),
        name="moba_attention",
    )(rel_bias, qt, k, vt, k_means, buckets)


def kernel(x, ffn1_norm, ffn1_w_in, ffn1_w_out, mix_norm, ffn2_norm, ffn2_w_in, ffn2_w_out,
           gdn_w_in, gdn_conv_w, gdn_a_log, gdn_dt_bias, gdn_out_norm, gdn_w_out,
           kv_norm, w_kv, moba_w_q, moba_w_o, rel_bias, final_norm):
    batch, seq, d_model = x.shape
    depth = ffn1_norm.shape[0]
    n_a = gdn_w_in.shape[0]
    n_v_heads = gdn_a_log.shape[1]
    value_dim = gdn_w_out.shape[1]
    conv_dim = gdn_conv_w.shape[2]
    key_dim = (conv_dim - value_dim) // 2
    rep = value_dim // key_dim
    n_heads = moba_w_q.shape[2] // ATT_HEAD_DIM
    kv_dim = w_kv.shape[1] // 2
    n_kv_heads = kv_dim // ATT_HEAD_DIM
    bf = lambda a: a.astype(BF16)
    tl = TILES
    gdn_w_in_t = jnp.swapaxes(gdn_w_in, 1, 2)
    gdn_w_out_bf = bf(gdn_w_out)
    moba_w_q_t = bf(jnp.swapaxes(moba_w_q, 1, 2))
    moba_w_o_bf = bf(moba_w_o)

    outs = []
    for bi in range(batch):
        h = x[bi]
        k_nat = vt = k_means = None
        for layer in range(depth):
            if layer == n_a:
                k_nat, vt, k_means = moba_kv(h, kv_norm, bf(w_kv[:, :kv_dim]), bf(w_kv[:, kv_dim:].T),
                                             tm=tl.kv_rows)
            h = ffn(h, ffn1_norm[layer], ffn1_w_in, ffn1_w_out, layer, tm=tl.ffn_rows, tf=tl.ffn_cols)
            if layer < n_a:
                main = conv_dim + value_dim
                n_ba = 2 * n_v_heads
                proj, ba = gdn_in_proj(h, mix_norm[layer], gdn_w_in_t, layer, n_main=main, n_tail=n_ba,
                                       tm=tl.gdn_in_rows, tn=tl.gdn_in_cols)
                beta, g = gdn_gates(ba, gdn_a_log[layer], gdn_dt_bias[layer], tt=tl.gates_rows)
                qkv = gdn_conv(proj, gdn_conv_w[layer], conv_dim=conv_dim, key_dim=key_dim,
                               tt=tl.conv_rows, tc=tl.conv_cols, strip=tl.conv_strip)
                g_rows = g.T.reshape(n_v_heads, 1, seq)
                wq, l2, u, glw = gdn_prep(qkv, g, beta, g_rows, n_heads=n_v_heads, rep=rep, tp=tl.prep_rows)
                gated = gdn_scan(wq, l2, u, glw, proj, conv_dim, gdn_out_norm[layer],
                                 pb=tl.scan_pairs, tp=tl.scan_rows)
                h = matmul_res(gated, gdn_w_out_bf, layer, h, tm=tl.gdn_out_rows, tn=tl.gdn_out_cols,
                               name="gdn_out")
            else:
                j = layer - n_a
                qt = norm_matmul_t(h, mix_norm[layer], moba_w_q_t, j, tm=tl.qt_rows, tn=tl.qt_cols,
                                   name="moba_qt", out_dtype=BF16, out_scale=MOBA_Q_SCALE)
                att = moba_attention(qt, k_nat, vt, k_means, rel_bias,
                                     n_heads=n_heads, n_kv_heads=n_kv_heads)
                h = matmul_res(att, moba_w_o_bf, j, h, tm=tl.moba_out_rows, tn=tl.moba_out_cols,
                               name="moba_out")
            last = layer == depth - 1
            h = ffn(h, ffn2_norm[layer], ffn2_w_in, ffn2_w_out, layer, final_norm if last else None,
                    tm=tl.ffn_rows, tf=tl.ffn_cols)
        outs.append(h)
    return jnp.stack(outs)
```

```python
import functools
import math
from typing import NamedTuple

import numpy as np
import jax
import jax.numpy as jnp
from jax import lax
from jax.experimental import pallas as pl
from jax.experimental.pallas import tpu as pltpu

F32 = jnp.float32
BF16 = jnp.bfloat16

RMS_EPS = 1e-6
GDN_HEAD_DIM = 128
GDN_CHUNK = 64
GDN_CONV_WIDTH = 4
PREP_GROUPS = 2
ATT_HEAD_DIM = 128
MOBA_BLOCK = 256
MOBA_TOPK = 3
N_REL_BUCKETS = 32
REL_MAX_DISTANCE = 128
MASK_VALUE = -1e30
LOG2_E = 1.4426950408889634
MOBA_Q_SCALE = ATT_HEAD_DIM ** -0.5 * LOG2_E

V7X_VMEM_BYTES = 64 * 1024 * 1024
V7X_SUBLANES = 8
V7X_BF16_SUBLANE_TILE = 16
MOBA_L_ROWS = V7X_BF16_SUBLANE_TILE

VMEM_LIMIT_BYTES = V7X_VMEM_BYTES - 6 * 1024 * 1024


class Tiles(NamedTuple):
    ffn_rows: int = 1024
    ffn_cols: int = 512
    gdn_in_rows: int = 2048
    gdn_in_cols: int = 512
    gates_rows: int = 1024
    conv_rows: int = 2048
    conv_cols: int = 1024
    conv_strip: int = 16
    prep_rows: int = 2048
    scan_pairs: int = 16
    scan_rows: int = 256
    gdn_out_rows: int = 1024
    gdn_out_cols: int = 1024
    kv_rows: int = 1024
    qt_rows: int = 2048
    qt_cols: int = 1024
    moba_out_rows: int = 2048
    moba_out_cols: int = 512


TILES = Tiles()


def _params(*sem):
    return pltpu.CompilerParams(dimension_semantics=sem, vmem_limit_bytes=VMEM_LIMIT_BYTES)


def _rms_rows(x, g):
    ms = jnp.mean(x * x, axis=-1, keepdims=True)
    return x * lax.rsqrt(ms + RMS_EPS) * g


def _silu(x):
    return x * jax.nn.sigmoid(x)


def _dot(a, b):
    return jnp.dot(a, b, preferred_element_type=F32)


def _dot_nt(a, b):
    return lax.dot_general(a, b, (((1,), (1,)), ((), ())), preferred_element_type=F32)


def _row_block(tm, d):
    return pl.BlockSpec((tm, d), lambda i, j: (i, 0), pipeline_mode=pl.Buffered(1))


def _gdn_in_kernel(x_ref, g_ref, w_ref, wt_ref, o_ref, ot_ref, xn_ref):
    @pl.when(pl.program_id(1) == 0)
    def _():
        xn_ref[...] = _rms_rows(x_ref[...], g_ref[...]).astype(BF16)
        ot_ref[...] = _dot_nt(xn_ref[...], wt_ref[...].astype(BF16))

    o_ref[...] = _dot_nt(xn_ref[...], w_ref[...].astype(BF16))


def gdn_in_proj(x, g, w_t, layer, *, n_main, n_tail, tm, tn):
    t, d = x.shape
    assert n_main % tn == 0 and n_main % n_tail == 0
    return pl.pallas_call(
        _gdn_in_kernel,
        grid=(t // tm, n_main // tn),
        in_specs=[
            _row_block(tm, d),
            pl.BlockSpec((1, d), lambda i, j: (0, 0)),
            pl.BlockSpec((None, tn, d), lambda i, j: (layer, j, 0)),
            pl.BlockSpec((None, n_tail, d), lambda i, j: (layer, n_main // n_tail, 0)),
        ],
        out_specs=[pl.BlockSpec((tm, tn), lambda i, j: (i, j)),
                   pl.BlockSpec((tm, n_tail), lambda i, j: (i, 0))],
        out_shape=[jax.ShapeDtypeStruct((t, n_main), F32), jax.ShapeDtypeStruct((t, n_tail), F32)],
        scratch_shapes=[pltpu.VMEM((tm, d), BF16)],
        compiler_params=_params("parallel", "arbitrary"),
        name="gdn_in",
    )(x, g.reshape(1, d), w_t, w_t)


def _norm_matmul_t_kernel(x_ref, g_ref, wt_ref, o_ref, xn_ref, *, out_scale):
    @pl.when(pl.program_id(1) == 0)
    def _():
        xn_ref[...] = _rms_rows(x_ref[...], g_ref[...]).astype(BF16)

    y = _dot_nt(wt_ref[...], xn_ref[...])
    if out_scale != 1.0:
        y = y * out_scale
    o_ref[...] = y.astype(o_ref.dtype)


def norm_matmul_t(x, g, wt, layer, *, tm, tn, name, out_dtype=F32, out_scale=1.0):
    t, d = x.shape
    n = wt.shape[1]
    return pl.pallas_call(
        functools.partial(_norm_matmul_t_kernel, out_scale=out_scale),
        grid=(t // tm, n // tn),
        in_specs=[
            _row_block(tm, d),
            pl.BlockSpec((1, d), lambda i, j: (0, 0)),
            pl.BlockSpec((None, tn, d), lambda i, j: (layer, j, 0)),
        ],
        out_specs=pl.BlockSpec((tn, tm), lambda i, j: (j, i)),
        out_shape=jax.ShapeDtypeStruct((n, t), out_dtype),
        scratch_shapes=[pltpu.VMEM((tm, d), BF16)],
        compiler_params=_params("parallel", "arbitrary"),
        name=name,
    )(x, g.reshape(1, d), wt)


def _ffn_kernel(h_ref, g_ref, wg_ref, wu_ref, wo_ref, fg_ref, o_ref, xn_ref, *, final_norm):
    j = pl.program_id(1)
    tm = h_ref.shape[0]
    strip = min(tm, 256)

    @pl.when(j == 0)
    def _():
        for r0 in range(0, tm, strip):
            r = slice(r0, r0 + strip)
            xn_ref[r, :] = _rms_rows(h_ref[r, :], g_ref[...]).astype(BF16)
            o_ref[r, :] = jnp.zeros((strip, o_ref.shape[1]), F32)

    xn = xn_ref[...]
    tf = wg_ref.shape[1]
    half = min(tf, 256)
    acts = []
    for c0 in range(0, tf, half):
        gate = _dot(xn, wg_ref[:, c0:c0 + half].astype(BF16))
        up = _dot(xn, wu_ref[:, c0:c0 + half].astype(BF16))
        acts.append((_silu(gate) * up).astype(BF16))
    act = acts[0] if len(acts) == 1 else jnp.concatenate(acts, axis=1)
    o_ref[...] += _dot(act, wo_ref[...].astype(BF16))

    @pl.when(j == pl.num_programs(1) - 1)
    def _():
        for r0 in range(0, tm, strip):
            r = slice(r0, r0 + strip)
            y = h_ref[r, :] + 0.5 * o_ref[r, :]
            if final_norm:
                y = _rms_rows(y, fg_ref[...])
            o_ref[r, :] = y


def ffn(h, g, w_in, w_out, layer, final_g=None, *, tm, tf):
    t, d = h.shape
    f = w_out.shape[1]
    nf = f // tf
    final_norm = final_g is not None
    fg = (final_g if final_norm else g).reshape(1, d)
    return pl.pallas_call(
        functools.partial(_ffn_kernel, final_norm=final_norm),
        grid=(t // tm, nf),
        in_specs=[
            _row_block(tm, d),
            pl.BlockSpec((1, d), lambda i, j: (0, 0)),
            pl.BlockSpec((None, d, tf), lambda i, j: (layer, 0, j)),
            pl.BlockSpec((None, d, tf), lambda i, j: (layer, 0, j + nf)),
            pl.BlockSpec((None, tf, d), lambda i, j: (layer, j, 0)),
            pl.BlockSpec((1, d), lambda i, j: (0, 0)),
        ],
        out_specs=_row_block(tm, d),
        out_shape=jax.ShapeDtypeStruct((t, d), F32),
        scratch_shapes=[pltpu.VMEM((tm, d), BF16)],
        compiler_params=_params("parallel", "arbitrary"),
        name="ffn",
    )(h, g.reshape(1, d), w_in, w_in, w_out, fg)


def _matmul_res_kernel(a_ref, w_ref, r_ref, o_ref):
    o_ref[...] = r_ref[...] + _dot(a_ref[...], w_ref[...])


def matmul_res(a, w, layer, r, *, tm, tn, name):
    t, k = a.shape
    n = w.shape[2]
    return pl.pallas_call(
        _matmul_res_kernel,
        grid=(t // tm, n // tn),
        in_specs=[
            pl.BlockSpec((tm, k), lambda i, j: (i, 0)),
            pl.BlockSpec((None, k, tn), lambda i, j: (layer, 0, j)),
            pl.BlockSpec((tm, tn), lambda i, j: (i, j)),
        ],
        out_specs=pl.BlockSpec((tm, tn), lambda i, j: (i, j)),
        out_shape=jax.ShapeDtypeStruct((t, n), F32),
        compiler_params=_params("parallel", "arbitrary"),
        name=name,
    )(a, w, r)


def _conv_kernel(prev_ref, cur_ref, w_ref, o_ref, xs_ref, *, n_norm_blocks, strip):
    i = pl.program_id(0)
    j = pl.program_id(1)
    tt, tc = cur_ref.shape
    halo = prev_ref.shape[0]
    xs_ref[0:halo, :] = jnp.where(i == 0, 0.0, prev_ref[...])
    xs_ref[halo:, :] = cur_ref[...]
    w = w_ref[...]

    def conv_strip(r0):
        y = w[GDN_CONV_WIDTH - 1:GDN_CONV_WIDTH] * xs_ref[halo + r0:halo + r0 + strip, :]
        for s in range(1, GDN_CONV_WIDTH):
            tap = GDN_CONV_WIDTH - 1 - s
            y = y + w[tap:tap + 1] * xs_ref[halo + r0 - s:halo + r0 - s + strip, :]
        return _silu(y)

    @pl.when(j < n_norm_blocks)
    def _():
        for r0 in range(0, tt, strip):
            y = conv_strip(r0)
            for hh in range(tc // GDN_HEAD_DIM):
                seg = y[:, hh * GDN_HEAD_DIM:(hh + 1) * GDN_HEAD_DIM]
                ss = jnp.sum(seg * seg, axis=-1, keepdims=True)
                o_ref[r0:r0 + strip, hh * GDN_HEAD_DIM:(hh + 1) * GDN_HEAD_DIM] = seg * lax.rsqrt(ss + RMS_EPS)

    @pl.when(j >= n_norm_blocks)
    def _():
        for r0 in range(0, tt, strip):
            o_ref[r0:r0 + strip, :] = conv_strip(r0)


def gdn_conv(proj, conv_w, *, conv_dim, key_dim, tt, tc, strip):
    t = proj.shape[0]
    halo = V7X_SUBLANES
    assert halo >= GDN_CONV_WIDTH - 1
    return pl.pallas_call(
        functools.partial(_conv_kernel, n_norm_blocks=2 * key_dim // tc, strip=strip),
        grid=(t // tt, conv_dim // tc),
        in_specs=[
            pl.BlockSpec((halo, tc), lambda i, j: (jnp.maximum(i * (tt // halo) - 1, 0), j)),
            pl.BlockSpec((tt, tc), lambda i, j: (i, j)),
            pl.BlockSpec((GDN_CONV_WIDTH, tc), lambda i, j: (0, j)),
        ],
        out_specs=pl.BlockSpec((tt, tc), lambda i, j: (i, j)),
        out_shape=jax.ShapeDtypeStruct((t, conv_dim), F32),
        scratch_shapes=[pltpu.VMEM((tt + halo, tc), F32)],
        compiler_params=_params("parallel", "parallel"),
        name="gdn_conv",
    )(proj, proj, conv_w)


def _gates_kernel(ba_ref, alog_ref, dtb_ref, beta_ref, g_ref):
    nh = beta_ref.shape[1]
    ba = ba_ref[...]
    beta_ref[...] = jax.nn.sigmoid(ba[:, :nh])
    z = ba[:, nh:] + dtb_ref[...]
    softplus = jnp.maximum(z, 0.0) + jnp.log1p(jnp.exp(-jnp.abs(z)))
    g_ref[...] = -jnp.exp(alog_ref[...]) * softplus


def gdn_gates(ba, a_log, dt_bias, *, tt):
    t, two_h = ba.shape
    nh = two_h // 2
    return pl.pallas_call(
        _gates_kernel,
        grid=(t // tt,),
        in_specs=[
            pl.BlockSpec((tt, two_h), lambda i: (i, 0)),
            pl.BlockSpec((1, nh), lambda i: (0, 0)),
            pl.BlockSpec((1, nh), lambda i: (0, 0)),
        ],
        out_specs=[pl.BlockSpec((tt, nh), lambda i: (i, 0)), pl.BlockSpec((tt, nh), lambda i: (i, 0))],
        out_shape=[jax.ShapeDtypeStruct((t, nh), F32), jax.ShapeDtypeStruct((t, nh), F32)],
        compiler_params=_params("parallel"),
        name="gdn_gates",
    )(ba, a_log.reshape(1, nh), dt_bias.reshape(1, nh))


def _prep_kernel(q_ref, k_ref, v_ref, gsel_ref, bsel_ref, grow_ref, wq_ref, l2_ref, u_ref, glw_ref):
    kh = pl.program_id(0)
    tp, nh = gsel_ref.shape
    c = GDN_CHUNK
    dh = GDN_HEAD_DIM
    scale = dh ** -0.5
    lane_h = lax.broadcasted_iota(jnp.int32, (tp, nh), 1)

    def column(ref, head):
        return jnp.sum(jnp.where(lane_h == head, ref[...], 0.0), axis=1, keepdims=True)

    gcol = [column(gsel_ref, 2 * kh + hh) for hh in range(2)]
    bcol = [column(bsel_ref, 2 * kh + hh) for hh in range(2)]
    grow = [grow_ref[hh] for hh in range(2)]

    ii = lax.broadcasted_iota(jnp.int32, (c, 2 * c), 0)
    ll = lax.broadcasted_iota(jnp.int32, (c, 2 * c), 1)
    jj = jnp.bitwise_and(ll, c - 1)
    head_a = ll < c
    incl = ii >= jj
    strict = ii > jj
    wi = lax.broadcasted_iota(jnp.int32, (c, 4 * c), 0)
    wl = lax.broadcasted_iota(jnp.int32, (c, 4 * c), 1)
    eye_w = jnp.where(wi == jnp.bitwise_and(wl, c - 1), 1.0, 0.0).astype(F32)
    br = lax.broadcasted_iota(jnp.int32, (4 * c, 4 * c), 0)
    bc = lax.broadcasted_iota(jnp.int32, (4 * c, 4 * c), 1)
    same_block = (br // c) == (bc // c)
    strict_block = jnp.logical_and(same_block, br != bc)

    def block_diag(packed):
        return jnp.concatenate([packed] * 4, axis=0)

    def packed_matmul(a, b):
        bd = jnp.where(same_block, block_diag(b), 0.0).astype(BF16)
        return _dot(a.astype(BF16), bd)

    n_half = tp // (2 * c)
    n_group = max(n_half // PREP_GROUPS, 1)

    grams = []
    for n in range(2 * n_half):
        r = slice(n * c, (n + 1) * c)
        q16 = (q_ref[r, :] * scale).astype(BF16)
        k16 = k_ref[r, :].astype(BF16)
        grams.append(_dot_nt(jnp.concatenate([q16, k16], axis=0),
                             jnp.concatenate([k16, k16], axis=0)))

    def chunk_lmat(n):
        r = slice(n * c, (n + 1) * c)
        g_pair = jnp.where(head_a, gcol[0][r], gcol[1][r])
        b_pair = jnp.where(head_a, bcol[0][r], bcol[1][r])
        grow_pair = jnp.concatenate([grow[0][:, r], grow[1][:, r]], axis=1)
        tri = jnp.where(incl, grow_pair, 0.0)
        gc = [jnp.sum(jnp.where(head_a, tri, 0.0), axis=1, keepdims=True),
              jnp.sum(jnp.where(head_a, 0.0, tri), axis=1, keepdims=True)]
        gc_col = jnp.where(head_a, gc[0], gc[1])
        gc_row = jnp.sum(jnp.where(ii <= jj, g_pair, 0.0), axis=0, keepdims=True)
        decay = jnp.where(incl, jnp.exp(jnp.where(incl, gc_col - gc_row, 0.0)), 0.0)
        return jnp.where(strict, grams[n][c:] * b_pair * decay, 0.0), (gc, decay)

    def chunk_operands(n, gc, decay):
        r = slice(n * c, (n + 1) * c)
        q = q_ref[r, :] * scale
        k = k_ref[r, :]
        l2_ref[n, 0:c, :] = jnp.where(incl, grams[n][:c] * decay, 0.0).astype(BF16)
        kups, rhs = [], []
        for hh in range(2):
            g_last = jnp.sum(gcol[hh][r], axis=0, keepdims=True)
            eg = jnp.exp(gc[hh])
            beta = bcol[hh][r]
            kups.append(k * jnp.exp(g_last - gc[hh]))
            wq_ref[n, c:2 * c, hh * dh:(hh + 1) * dh] = (q * eg).astype(BF16)
            glw_ref[n, :, hh * dh:(hh + 1) * dh] = jnp.broadcast_to(jnp.exp(g_last), (V7X_SUBLANES, dh))
            rhs.append(jnp.concatenate([v_ref[r, hh * dh:(hh + 1) * dh] * beta, k * (beta * eg)], axis=1))
        l2_ref[n, c:c + dh, :] = jnp.concatenate(kups, axis=0).T.astype(BF16)
        return rhs

    wj = jnp.bitwise_and(wl, c - 1)

    def invert(lws):
        xs = [eye_w - jnp.where(wi // 2 == wj // 2, lw, 0.0) for lw in lws]
        s = 2
        while s < c:
            join = jnp.logical_and(wi // (2 * s) == wj // (2 * s), wi // s > wj // s)
            xe = [packed_matmul(x, jnp.where(join, lw, 0.0)) for x, lw in zip(xs, lws)]
            xs = [x - packed_matmul(y, x) for x, y in zip(xs, xe)]
            s *= 2
        return xs

    for h0 in range(0, n_half, n_group):
        halves = range(h0, min(h0 + n_group, n_half))
        lm = {n: chunk_lmat(n) for half in halves for n in (2 * half, 2 * half + 1)}
        tinvs = invert([jnp.concatenate([lm[2 * half][0], lm[2 * half + 1][0]], axis=1) for half in halves])
        rhs = {n: chunk_operands(n, *lm[n][1]) for n in lm}
        for half, tinv in zip(halves, tinvs):
            rhs_all = jnp.concatenate(rhs[2 * half] + rhs[2 * half + 1], axis=0)
            t_off = jnp.where(strict_block, block_diag(tinv), 0.0).astype(BF16)
            uw = rhs_all + _dot(t_off, rhs_all.astype(BF16))
            for y in range(2):
                n = 2 * half + y
                for hh in range(2):
                    blk = uw[(2 * y + hh) * c:(2 * y + hh + 1) * c]
                    u_ref[n * c:(n + 1) * c, hh * dh:(hh + 1) * dh] = blk[:, :dh]
                    wq_ref[n, 0:c, hh * dh:(hh + 1) * dh] = blk[:, dh:].astype(BF16)


def gdn_prep(qkv, g, beta, g_rows, *, n_heads, rep, tp):
    assert rep == 2
    t = qkv.shape[0]
    dh = GDN_HEAD_DIM
    c = GDN_CHUNK
    nk = n_heads // rep
    nc = tp // c
    return pl.pallas_call(
        _prep_kernel,
        grid=(nk, t // tp),
        in_specs=[
            pl.BlockSpec((tp, dh), lambda h, i: (i, h)),
            pl.BlockSpec((tp, dh), lambda h, i: (i, nk + h)),
            pl.BlockSpec((tp, 2 * dh), lambda h, i: (i, nk + h)),
            pl.BlockSpec((tp, n_heads), lambda h, i: (i, 0)),
            pl.BlockSpec((tp, n_heads), lambda h, i: (i, 0)),
            pl.BlockSpec((2, 1, tp), lambda h, i: (h, 0, i)),
        ],
        out_specs=[
            pl.BlockSpec((None, nc, 2 * c, 2 * dh), lambda h, i: (h, i, 0, 0)),
            pl.BlockSpec((None, nc, c + dh, 2 * c), lambda h, i: (h, i, 0, 0)),
            pl.BlockSpec((None, tp, 2 * dh), lambda h, i: (h, i, 0)),
            pl.BlockSpec((None, nc, V7X_SUBLANES, 2 * dh), lambda h, i: (h, i, 0, 0)),
        ],
        out_shape=[
            jax.ShapeDtypeStruct((nk, t // c, 2 * c, 2 * dh), BF16),
            jax.ShapeDtypeStruct((nk, t // c, c + dh, 2 * c), BF16),
            jax.ShapeDtypeStruct((nk, t, 2 * dh), F32),
            jax.ShapeDtypeStruct((nk, t // c, V7X_SUBLANES, 2 * dh), F32),
        ],
        compiler_params=_params("parallel", "parallel"),
        name="gdn_prep",
    )(qkv, qkv, qkv, g, beta, g_rows)


def _scan_kernel(wq_ref, l2_ref, u_ref, glw_ref, z_ref, gn_ref, o_ref, s_ref):
    pb, nc = wq_ref.shape[0], wq_ref.shape[1]
    gn = gn_ref[...]
    c = GDN_CHUNK
    dh = GDN_HEAD_DIM

    @pl.when(pl.program_id(1) == 0)
    def _():
        s_ref[...] = jnp.zeros_like(s_ref)

    def pair_diag(x):
        z = jnp.zeros((x.shape[0], dh), x.dtype)
        return jnp.concatenate([jnp.concatenate([x[:, :dh], z], axis=1),
                                jnp.concatenate([z, x[:, dh:]], axis=1)], axis=0)

    states = [s_ref[p] for p in range(pb)]
    for n in range(nc):
        r = slice(n * c, (n + 1) * c)
        for p in range(pb):
            ws = _dot(wq_ref[p, n], pair_diag(states[p].astype(BF16)))
            v_new = (u_ref[p, r, :] - ws[:c]).astype(BF16)
            upd = _dot(l2_ref[p, n], pair_diag(v_new))
            o = ws[c:] + upd[:c]
            states[p] = states[p] * glw_ref[p, n, 0:1, :] + upd[c:]
            for hh in range(2):
                cols = slice((2 * p + hh) * dh, (2 * p + hh + 1) * dh)
                gated = _rms_rows(o[:, hh * dh:(hh + 1) * dh], gn) * _silu(z_ref[r, cols])
                o_ref[r, cols] = gated.astype(BF16)
    for p in range(pb):
        s_ref[p] = states[p]


def gdn_scan(wq, l2, u, glw, proj, z_col0, out_norm, *, pb, tp):
    nk, t, two_dh = u.shape
    c = GDN_CHUNK
    nc = tp // c
    width = pb * two_dh
    blk4 = lambda a: pl.BlockSpec((pb, nc) + a.shape[2:], lambda h, i: (h, i, 0, 0))
    return pl.pallas_call(
        _scan_kernel,
        grid=(nk // pb, t // tp),
        in_specs=[blk4(wq), blk4(l2), pl.BlockSpec((pb, tp, two_dh), lambda h, i: (h, i, 0)), blk4(glw),
                  pl.BlockSpec((tp, width), lambda h, i: (i, z_col0 // width + h)),
                  pl.BlockSpec((1, GDN_HEAD_DIM), lambda h, i: (0, 0))],
        out_specs=pl.BlockSpec((tp, width), lambda h, i: (i, h)),
        out_shape=jax.ShapeDtypeStruct((t, nk * two_dh), BF16),
        scratch_shapes=[pltpu.VMEM((pb, two_dh // 2, two_dh), F32)],
        compiler_params=_params("parallel", "arbitrary"),
        name="gdn_scan",
    )(wq, l2, u, glw, proj, out_norm.reshape(1, GDN_HEAD_DIM))


def _moba_kv_kernel(x_ref, g_ref, wk_ref, wvt_ref, kb_ref, vt_ref, mean_ref):
    xn = _rms_rows(x_ref[...], g_ref[...]).astype(BF16)
    k = _dot(xn, wk_ref[...])
    kb_ref[...] = k.astype(BF16)
    for b in range(mean_ref.shape[0]):
        mean_ref[b] = jnp.mean(k[b * MOBA_BLOCK:(b + 1) * MOBA_BLOCK], axis=0, keepdims=True)
    vt_ref[...] = _dot_nt(wvt_ref[...], xn).astype(BF16)


def moba_kv(x, g, wk, wvt, *, tm):
    t, d = x.shape
    kv = wk.shape[1]
    nb = t // MOBA_BLOCK
    bpt = tm // MOBA_BLOCK
    kb, vt, means = pl.pallas_call(
        _moba_kv_kernel,
        grid=(t // tm,),
        in_specs=[
            pl.BlockSpec((tm, d), lambda i: (i, 0)),
            pl.BlockSpec((1, d), lambda i: (0, 0)),
            pl.BlockSpec((d, kv), lambda i: (0, 0)),
            pl.BlockSpec((kv, d), lambda i: (0, 0)),
        ],
        out_specs=[pl.BlockSpec((tm, kv), lambda i: (i, 0)),
                   pl.BlockSpec((kv, tm), lambda i: (0, i)),
                   pl.BlockSpec((bpt, 1, kv), lambda i: (i, 0, 0))],
        out_shape=[jax.ShapeDtypeStruct((t, kv), BF16), jax.ShapeDtypeStruct((kv, t), BF16),
                   jax.ShapeDtypeStruct((nb, 1, kv), F32)],
        compiler_params=_params("parallel"),
        name="moba_kv",
    )(x, g.reshape(1, d), wk, wvt)
    return kb, vt, means.reshape(nb, kv)


def _rel_bucket_table(n_dist):
    n = np.arange(n_dist)
    max_exact = N_REL_BUCKETS // 2
    ratio = np.log(np.maximum(n, max_exact).astype(np.float32) / np.float32(max_exact)) \
        / np.float32(math.log(REL_MAX_DISTANCE / max_exact))
    large = np.minimum(max_exact + (ratio.astype(np.float32) * (N_REL_BUCKETS - max_exact)).astype(np.int32),
                       N_REL_BUCKETS - 1)
    return np.where(n < max_exact, n, large).astype(np.int32)


def _bucket_tiles():
    table = _rel_bucket_table(2 * MOBA_BLOCK)
    a = np.arange(MOBA_BLOCK)[:, None]
    b = np.arange(MOBA_BLOCK)[None, :]
    d_own = b - a
    own = np.where(d_own >= 0, table[np.maximum(d_own, 0)], -1)
    prev = table[MOBA_BLOCK + b - a]
    return np.stack([own, prev]).astype(np.int32)


def _moba_kernel(rb_ref, qt_ref, k_ref, vt_ref, km_ref, bucket_ref, o_ref,
                 bias_ref, sel_ref, m_ref, acc_ref, *, group):
    g = pl.program_id(0)
    i = pl.program_id(1)
    bs = MOBA_BLOCK
    dh = ATT_HEAD_DIM
    nb = km_ref.shape[0]
    nq = group * bs

    @pl.when(i == 0)
    def _():
        for hh in range(group):
            for t in range(2):
                bk = bucket_ref[t]
                bias = jnp.full(bk.shape, MASK_VALUE, F32)
                for b in range(N_REL_BUCKETS):
                    bias = jnp.where(bk == b, rb_ref[b, g * group + hh] * LOG2_E, bias)
                bias_ref[t, :, hh * bs:(hh + 1) * bs] = bias

    qt = jnp.concatenate([qt_ref[hh * dh:(hh + 1) * dh, :] for hh in range(group)], axis=1)

    blk = lax.broadcasted_iota(jnp.int32, (nb, nq), 0)
    gate = jnp.where(blk < i, _dot(km_ref[...].astype(BF16), qt), -jnp.inf)
    sel = jnp.zeros((nb, nq), F32)
    for r in range(MOBA_TOPK):
        mx = jnp.max(gate, axis=0, keepdims=True)
        first = jnp.min(jnp.where(gate == mx, blk, nb), axis=0, keepdims=True)
        pick = blk == first
        sel = jnp.where(jnp.logical_and(pick, r < i), 1.0, sel)
        gate = jnp.where(pick, -jnp.inf, gate)
    sel_ref[...] = sel

    def keys(j, nblk=1):
        return k_ref[pl.ds(pl.multiple_of(j * bs, bs), nblk * bs), :]

    def values_t(j, nblk=1):
        return vt_ref[:, pl.ds(pl.multiple_of(j * bs, bs), nblk * bs)]

    OWN, PREV, FAR = 0, 1, 2
    SLAB = 32
    LOOKAHEAD = 4
    L_ROWS = MOBA_L_ROWS

    def attend(state, units):
        m_all, acc_all = state
        m_cur = [m_all[:, hh * bs:(hh + 1) * bs] for hh in range(group)]
        acc_cur = [acc_all[:, hh * bs:(hh + 1) * bs] for hh in range(group)]
        tasks = [(u, hh) for u in range(len(units)) for hh in range(group)]

        def score(t):
            u, hh = tasks[t]
            j, kinds = units[u]
            return _dot(keys(j, len(kinds)), qt[:, hh * bs:(hh + 1) * bs])

        pending = {t: score(t) for t in range(min(LOOKAHEAD, len(tasks)))}
        for t, (u, hh) in enumerate(tasks):
            if t + LOOKAHEAD < len(tasks):
                pending[t + LOOKAHEAD] = score(t + LOOKAHEAD)
            scores = pending.pop(t)
            j, kinds = units[u]
            cols = slice(hh * bs, (hh + 1) * bs)
            far_bias = rb_ref[N_REL_BUCKETS - 1, g * group + hh] * LOG2_E
            parts = []
            run_max = None
            for b, kind in enumerate(kinds):
                if kind != OWN:
                    row = jnp.where(sel_ref[pl.ds(j + b, 1), cols] > 0.0, far_bias if kind == FAR else 0.0,
                                    MASK_VALUE)
                    row = jnp.broadcast_to(row, (SLAB, bs))
                for r0 in range(0, bs, SLAB):
                    part = scores[b * bs + r0:b * bs + r0 + SLAB]
                    if kind == OWN:
                        part = part + bias_ref[0, r0:r0 + SLAB, cols]
                    elif kind == PREV:
                        part = part + bias_ref[1, r0:r0 + SLAB, cols] + row
                    else:
                        part = part + row
                    parts.append(part)
                    run_max = part if run_max is None else jnp.maximum(run_max, part)
            m_new = jnp.maximum(m_cur[hh], jnp.max(run_max, axis=0, keepdims=True))
            alpha = jnp.exp2(m_cur[hh] - m_new)
            m_slab = jnp.broadcast_to(m_new, (SLAB, bs))
            packed = [jnp.exp2(part - m_slab).astype(BF16) for part in parts]
            vt_ones = jnp.concatenate([values_t(j, len(kinds)), jnp.ones((L_ROWS, len(kinds) * bs), BF16)], axis=0)
            m_cur[hh] = m_new
            acc_cur[hh] = alpha * acc_cur[hh] + _dot(vt_ones, jnp.concatenate(packed, axis=0))
        return jnp.concatenate(m_cur, axis=1), jnp.concatenate(acc_cur, axis=1)

    def load_state():
        return m_ref[...], acc_ref[...]

    def store_state(state):
        m_ref[...], acc_ref[...] = state

    empty = (jnp.full((1, nq), MASK_VALUE, F32), jnp.zeros((dh + L_ROWS, nq), F32))

    n_old = jnp.maximum(i - 1, 0)
    odd = n_old % 2 == 1

    @pl.when(i == 0)
    def _():
        store_state(attend(empty, [(i, (OWN,))]))

    @pl.when(jnp.logical_and(i >= 1, jnp.logical_not(odd)))
    def _():
        store_state(attend(empty, [(i - 1, (PREV, OWN))]))

    @pl.when(odd)
    def _():
        store_state(attend(empty, [(i - 2, (FAR, PREV, OWN))]))

    n_far = n_old - n_old % 2

    def far_units(j0, n_units):
        return [(j0 + 2 * u, (FAR, FAR)) for u in range(n_units)]

    def far_body(jj, carry):
        store_state(attend(load_state(), far_units(4 * jj, 2)))
        return carry

    lax.fori_loop(0, n_far // 4, far_body, 0)

    @pl.when(n_far % 4 == 2)
    def _():
        store_state(attend(load_state(), far_units(n_far - 2, 1)))

    out = acc_ref[0:dh, :] / acc_ref[dh:dh + 1, :]
    for hh in range(group):
        o_ref[:, hh * dh:(hh + 1) * dh] = out[:, hh * bs:(hh + 1) * bs].T.astype(BF16)


def moba_attention(qt, k, vt, k_means, rel_bias, *, n_heads, n_kv_heads):
    dh = ATT_HEAD_DIM
    t = k.shape[0]
    bs = MOBA_BLOCK
    nb = t // bs
    group = n_heads // n_kv_heads
    nq = group * bs
    assert REL_MAX_DISTANCE <= bs
    buckets = jnp.asarray(_bucket_tiles())
    return pl.pallas_call(
        functools.partial(_moba_kernel, group=group),
        grid=(n_kv_heads, nb),
        in_specs=[
            pl.BlockSpec(memory_space=pltpu.SMEM),
            pl.BlockSpec((group * dh, bs), lambda g, i: (g, i)),
            pl.BlockSpec((t, dh), lambda g, i: (0, g)),
            pl.BlockSpec((dh, t), lambda g, i: (g, 0)),
            pl.BlockSpec((nb, dh), lambda g, i: (0, g)),
            pl.BlockSpec((2, bs, bs), lambda g, i: (0, 0, 0)),
        ],
        out_specs=pl.BlockSpec((bs, group * dh), lambda g, i: (i, g)),
        out_shape=jax.ShapeDtypeStruct((t, n_heads * dh), BF16),
        scratch_shapes=[
            pltpu.VMEM((2, bs, nq), F32),
            pltpu.VMEM((nb, nq), F32),
            pltpu.VMEM((1, nq), F32),
            pltpu.VMEM((dh + MOBA_L_ROWS, nq), F32),
        ],
        compiler_params=_params("parallel", "arbitrary"),
        name="moba_attention",
    )(rel_bias, qt, k, vt, k_means, buckets)


def kernel(x, ffn1_norm, ffn1_w_in, ffn1_w_out, mix_norm, ffn2_norm, ffn2_w_in, ffn2_w_out,
           gdn_w_in, gdn_conv_w, gdn_a_log, gdn_dt_bias, gdn_out_norm, gdn_w_out,
           kv_norm, w_kv, moba_w_q, moba_w_o, rel_bias, final_norm):
    batch, seq, d_model = x.shape
    depth = ffn1_norm.shape[0]
    n_a = gdn_w_in.shape[0]
    n_v_heads = gdn_a_log.shape[1]
    value_dim = gdn_w_out.shape[1]
    conv_dim = gdn_conv_w.shape[2]
    key_dim = (conv_dim - value_dim) // 2
    rep = value_dim // key_dim
    n_heads = moba_w_q.shape[2] // ATT_HEAD_DIM
    kv_dim = w_kv.shape[1] // 2
    n_kv_heads = kv_dim // ATT_HEAD_DIM
    bf = lambda a: a.astype(BF16)
    tl = TILES
    gdn_w_in_t = jnp.swapaxes(gdn_w_in, 1, 2)
    gdn_w_out_bf = bf(gdn_w_out)
    moba_w_q_t = bf(jnp.swapaxes(moba_w_q, 1, 2))
    moba_w_o_bf = bf(moba_w_o)

    outs = []
    for bi in range(batch):
        h = x[bi]
        k_nat = vt = k_means = None
        for layer in range(depth):
            if layer == n_a:
                k_nat, vt, k_means = moba_kv(h, kv_norm, bf(w_kv[:, :kv_dim]), bf(w_kv[:, kv_dim:].T),
                                             tm=tl.kv_rows)
            h = ffn(h, ffn1_norm[layer], ffn1_w_in, ffn1_w_out, layer, tm=tl.ffn_rows, tf=tl.ffn_cols)
            if layer < n_a:
                main = conv_dim + value_dim
                n_ba = 2 * n_v_heads
                proj, ba = gdn_in_proj(h, mix_norm[layer], gdn_w_in_t, layer, n_main=main, n_tail=n_ba,
                                       tm=tl.gdn_in_rows, tn=tl.gdn_in_cols)
                beta, g = gdn_gates(ba, gdn_a_log[layer], gdn_dt_bias[layer], tt=tl.gates_rows)
                qkv = gdn_conv(proj, gdn_conv_w[layer], conv_dim=conv_dim, key_dim=key_dim,
                               tt=tl.conv_rows, tc=tl.conv_cols, strip=tl.conv_strip)
                g_rows = g.T.reshape(n_v_heads, 1, seq)
                wq, l2, u, glw = gdn_prep(qkv, g, beta, g_rows, n_heads=n_v_heads, rep=rep, tp=tl.prep_rows)
                gated = gdn_scan(wq, l2, u, glw, proj, conv_dim, gdn_out_norm[layer],
                                 pb=tl.scan_pairs, tp=tl.scan_rows)
                h = matmul_res(gated, gdn_w_out_bf, layer, h, tm=tl.gdn_out_rows, tn=tl.gdn_out_cols,
                               name="gdn_out")
            else:
                j = layer - n_a
                qt = norm_matmul_t(h, mix_norm[layer], moba_w_q_t, j, tm=tl.qt_rows, tn=tl.qt_cols,
                                   name="moba_qt", out_dtype=BF16, out_scale=MOBA_Q_SCALE)
                att = moba_attention(qt, k_nat, vt, k_means, rel_bias,
                                     n_heads=n_heads, n_kv_heads=n_kv_heads)
                h = matmul_res(att, moba_w_o_bf, j, h, tm=tl.moba_out_rows, tn=tl.moba_out_cols,
                               name="moba_out")
            last = layer == depth - 1
            h = ffn(h, ffn2_norm[layer], ffn2_w_in, ffn2_w_out, layer, final_norm if last else None,
                    tm=tl.ffn_rows, tf=tl.ffn_cols)
        outs.append(h)
    return jnp.stack(outs)
```
